```python
import math
import jax, jax.numpy as jnp
from jax import lax
import numpy as np

D_MODEL = 1024
BATCH = 4
SEQ = 8192
DEPTH = 4

GRID_W = 64
CTX_LEN = 256
N_EVEN = (DEPTH + 1) // 2
N_ODD = DEPTH // 2

NA_HEADS = 8
NA_HEAD_DIM = 64
NA_W = NA_HEADS * NA_HEAD_DIM
NA_WIN_R = 8
NA_WIN_C = 16
NA_ROW_BLOCK = 2
RW_HEADS = 8
RW_HEAD_DIM = 64
RW_W = RW_HEADS * RW_HEAD_DIM
RW_DECAY_LORA = 64
RW_AAA_LORA = 64
RW_GATE_LORA = 128
RW_IN = 3 * RW_W + RW_DECAY_LORA + RW_AAA_LORA + RW_GATE_LORA
RW_SPLITS = (RW_W, 2 * RW_W, 3 * RW_W, 3 * RW_W + RW_DECAY_LORA, 3 * RW_W + RW_DECAY_LORA + RW_AAA_LORA)
RW_GN_EPS = 64e-5
SHIFT_WIDTH = 3
EVEN_IN = 3 * NA_W + RW_IN
EVEN_OUT = NA_W + RW_W
DA_HEADS = 8
DA_HEAD_DIM = 64
DA_QK = DA_HEADS * 2 * DA_HEAD_DIM
DA_V = DA_HEADS * 2 * DA_HEAD_DIM
ODD_IN = 2 * DA_QK + DA_V
Q_BLOCK = 128
ROPE_THETA = 10000.0
N_EXPERTS = 16
N_GROUPS = 4
EXP_PER_GROUP = N_EXPERTS // N_GROUPS
TOP_K = 2
D_EXPERT = 512
ALPHA = (2 * DEPTH) ** 0.25
BETA = (8 * DEPTH) ** -0.25
LN_EPS = 1e-5
F32 = jnp.float32

kernel_name = "hybrid_na_rwkv7_diffattn_grouped_moe_dit"


def _layer_norm(x, g, b):
    xf = x.astype(F32)
    mu = xf.mean(-1, keepdims=True)
    var = jnp.square(xf - mu).mean(-1, keepdims=True)
    return ((xf - mu) * lax.rsqrt(var + LN_EPS) * g + b).astype(x.dtype)


def _ada(cvec, w, b):
    return jnp.split(jax.nn.silu(cvec) @ w + b, 6, axis=-1)


def _modulate(h, shift, scale):
    return h * (1.0 + scale[:, None, :]) + shift[:, None, :]


def _axial_rope(z, pos_r, pos_c):
    half = z.shape[-1] // 2
    nf = half // 2
    inv = ROPE_THETA ** (-jnp.arange(nf, dtype=F32) / nf)

    def rot(za, pos):
        ang = pos.astype(F32)[:, None] * inv
        cos = jnp.cos(ang)[None, :, None, :]
        sin = jnp.sin(ang)[None, :, None, :]
        z1, z2 = za[..., :nf].astype(F32), za[..., nf:].astype(F32)
        return jnp.concatenate([z1 * cos - z2 * sin, z1 * sin + z2 * cos], -1)

    return jnp.concatenate([rot(z[..., :half], pos_r), rot(z[..., half:], pos_c)], -1).astype(z.dtype)


def _short_conv(z, w):
    pad = SHIFT_WIDTH // 2
    n = z.shape[1]
    zp = jnp.pad(z, ((0, 0), (pad, pad), (0, 0)))
    return sum(zp[:, j:j + n] * w[j] for j in range(SHIFT_WIDTH))


def _context_attention(q, k, v):
    B, L, H, Dh = q.shape
    s = jnp.einsum('bqhd,bkhd->bhqk', q, k).astype(F32) * Dh ** -0.5
    p = jax.nn.softmax(s, -1).astype(v.dtype)
    return jnp.einsum('bhqk,bkhd->bqhd', p, v).reshape(B, L, H * Dh)


def _na_attention(q, k, v, kc, vc, rpb):
    B, T, H, Dh = q.shape
    rows = T // GRID_W
    wr = min(NA_WIN_R, rows)
    r = np.arange(rows)
    row_idx = np.clip(r - wr // 2, 0, rows - wr)[:, None] + np.arange(wr)[None]
    c = np.arange(GRID_W)
    col_idx = np.clip(c - NA_WIN_C // 2, 0, GRID_W - NA_WIN_C)[:, None] + np.arange(NA_WIN_C)[None]
    d_row = row_idx - r[:, None] + (NA_WIN_R - 1)
    d_col = jnp.asarray(col_idx - c[:, None] + (NA_WIN_C - 1), jnp.int32)
    col_idx = jnp.asarray(col_idx, jnp.int32)
    n_blk = rows // NA_ROW_BLOCK
    row_idx_b = jnp.asarray(row_idx.reshape(n_blk, NA_ROW_BLOCK, wr), jnp.int32)
    d_row_b = jnp.asarray(d_row.reshape(n_blk, NA_ROW_BLOCK, wr), jnp.int32)
    qg = (q * Dh ** -0.5).reshape(B, rows, GRID_W, H, Dh)
    kg = k.reshape(B, rows, GRID_W, H, Dh)
    vg = v.reshape(B, rows, GRID_W, H, Dh)
    q_b = jnp.moveaxis(qg.reshape(B, n_blk, NA_ROW_BLOCK, GRID_W, H, Dh), 1, 0)
    n_loc = wr * NA_WIN_C

    def block(args):
        qb, ridx, dri = args
        kw = kg[:, ridx][:, :, :, col_idx]
        vw = vg[:, ridx][:, :, :, col_idx]
        bias = rpb[:, dri[:, :, None, None], d_col[None, None]]
        s_loc = jnp.einsum('bqchd,bqrcjhd->bhqcrj', qb, kw).astype(F32) + jnp.swapaxes(bias, 2, 3).astype(F32)
        s_ctx = jnp.einsum('bqchd,blhd->bhqcl', qb, kc).astype(F32)
        s = jnp.concatenate([s_loc.reshape(B, H, NA_ROW_BLOCK, GRID_W, n_loc), s_ctx], -1)
        p = jax.nn.softmax(s, -1).astype(v.dtype)
        p_loc = p[..., :n_loc].reshape(B, H, NA_ROW_BLOCK, GRID_W, wr, NA_WIN_C)
        return (jnp.einsum('bhqcrj,bqrcjhd->bqchd', p_loc, vw)
                + jnp.einsum('bhqcl,blhd->bqchd', p[..., n_loc:], vc))

    o = lax.map(block, (q_b, row_idx_b, d_row_b))
    return jnp.moveaxis(o, 0, 1).reshape(B, T, H * Dh)


def _rw_heads(z):
    return z.reshape(*z.shape[:-1], RW_HEADS, RW_HEAD_DIM)


def _rw_direction(p, w0, w_up, a0, a_up, k_k, k_a, r_k):
    r, k, v, xw, xa, _ = jnp.split(p.astype(F32), RW_SPLITS, axis=-1)
    decay = jnp.exp(-jnp.exp(-jax.nn.softplus(-(w0 + jnp.tanh(xw) @ w_up)) - 0.5))
    a = jax.nn.sigmoid(a0 + xa @ a_up)
    kk = _rw_heads(k * k_k)
    kk = kk / jnp.maximum(jnp.sqrt(jnp.sum(kk * kk, -1, keepdims=True)), 1e-12)
    k_eff = _rw_heads(k * (1.0 + (a - 1.0) * k_a))
    r, v, a = _rw_heads(r), _rw_heads(v), _rw_heads(a)
    bonus = jnp.sum(r * k_eff * r_k, -1, keepdims=True) * v
    return (r, _rw_heads(decay), k_eff, v, -kk, kk * a), bonus


def _rw_step(S, inp):
    r_t, w_t, k_t, v_t, a_t, b_t = inp
    sa = jnp.einsum('bhij,bhj->bhi', S, a_t)
    S = S * w_t[:, :, None, :] + sa[..., None] * b_t[:, :, None, :] + v_t[..., None] * k_t[:, :, None, :]
    return S, jnp.einsum('bhij,bhj->bhi', S, r_t)


def _rw_scan(S0, seq_inputs, reverse):
    xs = tuple(jnp.moveaxis(z, 1, 0) for z in seq_inputs)
    S, ys = lax.scan(_rw_step, S0, xs, reverse=reverse)
    return S, jnp.moveaxis(ys, 0, 1)


def _rwkv7_bidir(p_lat, p_ctx, w0, w_up, a0, a_up, g_up, k_k, k_a, r_k, gn_g, gn_b):
    B = p_lat.shape[0]
    S0 = jnp.zeros((B, RW_HEADS, RW_HEAD_DIM, RW_HEAD_DIM), F32)
    ys_lat, ys_ctx, bon_lat, bon_ctx = [], [], [], []
    for d, reverse in enumerate((False, True)):
        in_ctx, b_ctx = _rw_direction(p_ctx, w0[d], w_up[d], a0[d], a_up[d], k_k, k_a, r_k)
        in_lat, b_lat = _rw_direction(p_lat, w0[d], w_up[d], a0[d], a_up[d], k_k, k_a, r_k)
        S_ctx, y_c = _rw_scan(S0, in_ctx, reverse)
        _, y_l = _rw_scan(S_ctx, in_lat, reverse)
        ys_ctx.append(y_c); ys_lat.append(y_l); bon_ctx.append(b_ctx); bon_lat.append(b_lat)

    def finish(y, bonus, p):
        g = jax.nn.sigmoid(p[..., RW_SPLITS[-1]:].astype(F32)) @ g_up
        mu = y.mean(-1, keepdims=True)
        var = jnp.square(y - mu).mean(-1, keepdims=True)
        yn = ((y - mu) * lax.rsqrt(var + RW_GN_EPS)).reshape(*y.shape[:-2], RW_W) * gn_g + gn_b
        return ((yn + bonus.reshape(*bonus.shape[:-2], RW_W)) * g).astype(p.dtype)

    return (finish(ys_lat[0] + ys_lat[1], bon_lat[0] + bon_lat[1], p_lat),
            finish(ys_ctx[0] + ys_ctx[1], bon_ctx[0] + bon_ctx[1], p_ctx))


def _even_mixer(u_lat, u_ctx, w_in, w_out, conv_w, rpb, w0, w_up, a0, a_up, g_up, k_k, k_a, r_k, gn_g, gn_b):
    B, T, _ = u_lat.shape
    L = u_ctx.shape[1]
    p_lat = u_lat @ w_in
    p_ctx = u_ctx @ w_in

    def na_qkv(p, n):
        q, k, v = jnp.split(p[..., :3 * NA_W], 3, axis=-1)
        sh = (B, n, NA_HEADS, NA_HEAD_DIM)
        return q.reshape(sh), k.reshape(sh), v.reshape(sh)

    q, k, v = na_qkv(p_lat, T)
    qc, kc, vc = na_qkv(p_ctx, L)
    o_na_lat = _na_attention(q, k, v, kc, vc, rpb)
    o_na_ctx = _context_attention(qc, kc, vc)
    o_rw_lat, o_rw_ctx = _rwkv7_bidir(_short_conv(p_lat[..., 3 * NA_W:], conv_w),
                                      _short_conv(p_ctx[..., 3 * NA_W:], conv_w),
                                      w0, w_up, a0, a_up, g_up, k_k, k_a, r_k, gn_g, gn_b)
    y_lat = jnp.concatenate([o_na_lat, o_rw_lat], -1) @ w_out
    y_ctx = jnp.concatenate([o_na_ctx, o_rw_ctx], -1) @ w_out
    return y_lat, y_ctx


def _diff_mix(q, k, v, lam):
    s = jnp.einsum('bqhmd,bkhmd->bhmqk', q, k).astype(F32) * DA_HEAD_DIM ** -0.5
    p = jax.nn.softmax(s, -1)
    a = p[:, :, 0] - lam * p[:, :, 1]
    return jnp.einsum('bhqk,bkhe->bqhe', a.astype(v.dtype), v)


def _diff_attention_blocks(q, k, v, lam):
    B, T = q.shape[:2]
    nb = T // Q_BLOCK
    qb = jnp.moveaxis(q.reshape(B, nb, Q_BLOCK, *q.shape[2:]), 1, 0)
    o = lax.map(lambda blk: _diff_mix(blk, k, v, lam), qb)
    return jnp.moveaxis(o, 0, 1).reshape(B, T, *o.shape[3:])


def _odd_mixer(u_lat, u_ctx, w_in, w_out, lq1, lk1, lq2, lk2, subln_g, lambda_init, pos_r, pos_c, with_ctx_out):
    B, T, _ = u_lat.shape
    L = u_ctx.shape[1]
    lam = (jnp.exp(jnp.sum(lq1.astype(F32) * lk1.astype(F32)))
           - jnp.exp(jnp.sum(lq2.astype(F32) * lk2.astype(F32))) + lambda_init)

    def qkv(u, n):
        p = u @ w_in
        q = p[..., :DA_QK].reshape(B, n, DA_HEADS, 2, DA_HEAD_DIM)
        k = p[..., DA_QK:2 * DA_QK].reshape(B, n, DA_HEADS, 2, DA_HEAD_DIM)
        v = p[..., 2 * DA_QK:].reshape(B, n, DA_HEADS, 2 * DA_HEAD_DIM)
        return q, k, v

    def rot(z):
        return _axial_rope(z.reshape(B, T, 2 * DA_HEADS, DA_HEAD_DIM), pos_r, pos_c).reshape(z.shape)

    def head_norm(o):
        of = o.astype(F32)
        of = of * lax.rsqrt(jnp.mean(of * of, -1, keepdims=True) + 1e-5) * subln_g * (1.0 - lambda_init)
        return of.astype(o.dtype)

    q, k, v = qkv(u_lat, T)
    qc, kc, vc = qkv(u_ctx, L)
    q, k = rot(q), rot(k)
    k_all = jnp.concatenate([k, kc], axis=1)
    v_all = jnp.concatenate([v, vc], axis=1)
    y_lat = head_norm(_diff_attention_blocks(q, k_all, v_all, lam)).reshape(B, T, DA_V) @ w_out
    y_ctx = head_norm(_diff_mix(qc, kc, vc, lam)).reshape(B, L, DA_V) @ w_out if with_ctx_out else None
    return y_lat, y_ctx


def _moe(h, router_w, router_b, w_gate, w_up, w_down):
    s = jax.nn.sigmoid((h @ router_w).astype(F32))
    sel = (s + router_b.astype(F32)).reshape(-1, N_GROUPS, EXP_PER_GROUP)
    grp = jnp.argmax(lax.top_k(sel, TOP_K)[0].sum(-1), axis=-1)
    in_grp = jnp.take_along_axis(sel, grp[:, None, None], axis=1)[:, 0]
    _, idx = lax.top_k(in_grp, TOP_K)
    eid = grp[:, None] * EXP_PER_GROUP + idx
    gate = jnp.take_along_axis(s, eid, axis=-1)
    gate = gate / gate.sum(-1, keepdims=True)
    combine = jnp.einsum('nk,nke->ne', gate, jax.nn.one_hot(eid, N_EXPERTS, dtype=F32)).astype(h.dtype)
    out = jnp.zeros_like(h)
    for e in range(N_EXPERTS):
        he = jax.nn.silu(h @ w_gate[e]) * (h @ w_up[e])
        out = out + combine[:, e:e + 1] * (he @ w_down[e])
    return out


def setup_inputs(seed: int = 0) -> dict:
    key = jax.random.key(seed)
    keys = jax.random.split(key, 40)
    counter = iter(range(40))

    def nk():
        return keys[next(counter)]

    def nrm(shape, std):
        return jax.random.normal(nk(), shape, F32) * std

    D = D_MODEL
    even_scale = np.ones((EVEN_IN,), np.float32)
    even_scale[2 * NA_W:3 * NA_W] = BETA
    even_scale[3 * NA_W + 2 * RW_W:3 * NA_W + 3 * RW_W] = BETA
    odd_scale = np.ones((ODD_IN,), np.float32)
    odd_scale[2 * DA_QK:] = BETA
    return {
        "x": nrm((BATCH, SEQ, D), 1.0),
        "c": nrm((BATCH, D), 1.0),
        "ctx": nrm((BATCH, CTX_LEN, D), 1.0),
        "c_ctx": nrm((D,), 1.0),
        "ada_w": nrm((DEPTH, D, 6 * D), 0.5 * D ** -0.5),
        "ada_b": nrm((DEPTH, 6 * D), 0.02),
        "ln_g": 1.0 + nrm((DEPTH, 2, D), 0.02),
        "ln_b": nrm((DEPTH, 2, D), 0.02),
        "even_w_in": nrm((N_EVEN, D, EVEN_IN), D ** -0.5) * jnp.asarray(even_scale),
        "even_w_out": nrm((N_EVEN, EVEN_OUT, D), BETA * EVEN_OUT ** -0.5),
        "shift_w": jnp.array([0.3, 1.0, 0.3], F32)[None, :, None] + nrm((N_EVEN, SHIFT_WIDTH, RW_IN), 0.1),
        "na_rpb": nrm((N_EVEN, NA_HEADS, 2 * NA_WIN_R - 1, 2 * NA_WIN_C - 1), 0.1),
        "rw_w0": jax.random.uniform(nk(), (N_EVEN, 2, RW_W), F32, minval=-4.0, maxval=1.0),
        "rw_w_up": nrm((N_EVEN, 2, RW_DECAY_LORA, RW_W), 0.1 * RW_DECAY_LORA ** -0.5),
        "rw_a0": nrm((N_EVEN, 2, RW_W), 0.1),
        "rw_a_up": nrm((N_EVEN, 2, RW_AAA_LORA, RW_W), 0.5 * RW_AAA_LORA ** -0.5),
        "rw_g_up": nrm((N_EVEN, RW_GATE_LORA, RW_W), RW_GATE_LORA ** -0.5),
        "rw_k_k": 0.85 + nrm((N_EVEN, RW_W), 0.05),
        "rw_k_a": 1.0 + nrm((N_EVEN, RW_W), 0.05),
        "rw_r_k": nrm((N_EVEN, RW_HEADS, RW_HEAD_DIM), 0.1),
        "rw_gn_g": 1.0 + nrm((N_EVEN, RW_W), 0.02),
        "rw_gn_b": nrm((N_EVEN, RW_W), 0.02),
        "odd_w_in": nrm((N_ODD, D, ODD_IN), D ** -0.5) * jnp.asarray(odd_scale),
        "odd_w_out": nrm((N_ODD, DA_V, D), BETA * DA_V ** -0.5),
        "da_lq1": nrm((N_ODD, DA_HEAD_DIM), 0.1),
        "da_lk1": nrm((N_ODD, DA_HEAD_DIM), 0.1),
        "da_lq2": nrm((N_ODD, DA_HEAD_DIM), 0.1),
        "da_lk2": nrm((N_ODD, DA_HEAD_DIM), 0.1),
        "da_subln_g": 1.0 + nrm((N_ODD, 2 * DA_HEAD_DIM), 0.02),
        "router_w": nrm((D, N_EXPERTS), D ** -0.5),
        "router_b": nrm((N_EXPERTS,), 0.01),
        "exp_w_gate": nrm((DEPTH, N_EXPERTS, D, D_EXPERT), D ** -0.5),
        "exp_w_up": nrm((DEPTH, N_EXPERTS, D, D_EXPERT), D ** -0.5),
        "exp_w_down": nrm((DEPTH, N_EXPERTS, D_EXPERT, D), BETA * D_EXPERT ** -0.5),
    }


def reference(x, c, ctx, c_ctx, ada_w, ada_b, ln_g, ln_b, even_w_in, even_w_out, shift_w, na_rpb,
              rw_w0, rw_w_up, rw_a0, rw_a_up, rw_g_up, rw_k_k, rw_k_a, rw_r_k, rw_gn_g, rw_gn_b,
              odd_w_in, odd_w_out, da_lq1, da_lk1, da_lq2, da_lk2, da_subln_g,
              router_w, router_b, exp_w_gate, exp_w_up, exp_w_down):
    B, T, D = x.shape
    t = jnp.arange(T)
    pos_r, pos_c = t // GRID_W, t % GRID_W
    h_lat, h_ctx = x, ctx
    for l in range(DEPTH):
        last = l == DEPTH - 1
        m_lat = _ada(c, ada_w[l], ada_b[l])
        m_ctx = _ada(c_ctx[None], ada_w[l], ada_b[l])
        u_lat = _modulate(h_lat, m_lat[0], m_lat[1])
        u_ctx = _modulate(h_ctx, m_ctx[0], m_ctx[1])
        if l % 2 == 0:
            i = l // 2
            y_lat, y_ctx = _even_mixer(u_lat, u_ctx, even_w_in[i], even_w_out[i], shift_w[i], na_rpb[i],
                                       rw_w0[i], rw_w_up[i], rw_a0[i], rw_a_up[i], rw_g_up[i],
                                       rw_k_k[i], rw_k_a[i], rw_r_k[i], rw_gn_g[i], rw_gn_b[i])
        else:
            i = l // 2
            lambda_init = 0.8 - 0.6 * math.exp(-0.3 * l)
            y_lat, y_ctx = _odd_mixer(u_lat, u_ctx, odd_w_in[i], odd_w_out[i], da_lq1[i], da_lk1[i],
                                      da_lq2[i], da_lk2[i], da_subln_g[i], lambda_init, pos_r, pos_c,
                                      not last)
        h_lat = _layer_norm(ALPHA * h_lat + m_lat[2][:, None, :] * y_lat, ln_g[l, 0], ln_b[l, 0])
        u_lat = _modulate(h_lat, m_lat[3], m_lat[4])
        if last:
            f_lat = _moe(u_lat.reshape(-1, D), router_w, router_b,
                         exp_w_gate[l], exp_w_up[l], exp_w_down[l]).reshape(B, T, D)
        else:
            h_ctx = _layer_norm(ALPHA * h_ctx + m_ctx[2][:, None, :] * y_ctx, ln_g[l, 0], ln_b[l, 0])
            u_ctx = _modulate(h_ctx, m_ctx[3], m_ctx[4])
            L = u_ctx.shape[1]
            f = _moe(jnp.concatenate([u_lat.reshape(-1, D), u_ctx.reshape(-1, D)], 0), router_w, router_b,
                     exp_w_gate[l], exp_w_up[l], exp_w_down[l])
            f_lat = f[:B * T].reshape(B, T, D)
            h_ctx = _layer_norm(ALPHA * h_ctx + m_ctx[5][:, None, :] * f[B * T:].reshape(B, L, D),
                                ln_g[l, 1], ln_b[l, 1])
        h_lat = _layer_norm(ALPHA * h_lat + m_lat[5][:, None, :] * f_lat, ln_g[l, 1], ln_b[l, 1])
    return h_lat
```

```python
import functools
import math

import numpy as np
import jax
import jax.numpy as jnp
from jax import lax
from jax.experimental import pallas as pl
from jax.experimental.pallas import tpu as pltpu

F32 = jnp.float32
MM = jnp.bfloat16
HIGHEST = lax.Precision.HIGHEST

D = 1024
DEPTH = 4
GRID_W = 64
NA_HEADS = 8
NA_W = 512
NA_WIN_R = 8
NA_WIN_C = 16
NA_QB = 128
NA_KROWS = 9
NA_NLOC = NA_KROWS * GRID_W
RW_W = 512
RW_HEAD = 64
RW_IN = 1792
RW_GN_EPS = 64e-5
RW_CHUNK = 64
RW_GRP = 256
DA_HEADS = 8
DA_W = 1024
ROPE_THETA = 10000.0
N_EXPERTS = 16
N_GROUPS = 4
EPG = 4
D_EXPERT = 512
ALPHA = (2 * DEPTH) ** 0.25
LN_EPS = 1e-5
TM = 256
DA_TQ = 256
DA_TK = 512
NEG = -1e30
VMEM_LIMIT = 56 * 1024 * 1024


def _dot(a, b):
    return jnp.dot(a.astype(MM), b.astype(MM), preferred_element_type=F32)


def _dot_nt(a, b):
    return lax.dot_general(a.astype(MM), b.astype(MM), (((1,), (1,)), ((), ())), preferred_element_type=F32)


def _split(x):
    hi = x.astype(MM)
    lo = (x - hi.astype(F32)).astype(MM)
    return hi, lo


def _dot_split_lhs(x, w):
    hi, lo = _split(x)
    return jnp.dot(hi, w, preferred_element_type=F32) + jnp.dot(lo, w, preferred_element_type=F32)


def _dot_split_rhs(w, x):
    hi, lo = _split(x)
    return jnp.dot(w, hi, preferred_element_type=F32) + jnp.dot(w, lo, preferred_element_type=F32)


def _sigmoid(x):
    return 1.0 / (1.0 + jnp.exp(-x))


def _params(sem):
    return pltpu.CompilerParams(dimension_semantics=sem, vmem_limit_bytes=VMEM_LIMIT)


def _mod_row(mod_ref, row, j):
    return mod_ref[pl.ds(row, 1), j * D:(j + 1) * D]


def _layer_norm(z, g, b):
    mu = jnp.mean(z, axis=-1, keepdims=True)
    zc = z - mu
    var = jnp.mean(zc * zc, axis=-1, keepdims=True)
    return zc * lax.rsqrt(var + LN_EPS) * g + b


def _ada_kernel(cv_ref, w_ref, b_ref, o_ref):
    x = cv_ref[...]
    x = x * _sigmoid(x)
    o_ref[...] = jnp.dot(x, w_ref[...], precision=HIGHEST, preferred_element_type=F32) + b_ref[...]


def _ada_call(cv, ada_w, ada_b):
    R = cv.shape[0]
    tn = 1536
    return pl.pallas_call(
        _ada_kernel,
        grid=(DEPTH, 6 * D // tn),
        in_specs=[pl.BlockSpec((R, D), lambda l, j: (0, 0)),
                  pl.BlockSpec((None, D, tn), lambda l, j: (l, 0, j)),
                  pl.BlockSpec((None, 1, tn), lambda l, j: (l, 0, j))],
        out_specs=pl.BlockSpec((None, R, tn), lambda l, j: (l, 0, j)),
        out_shape=jax.ShapeDtypeStruct((DEPTH, R, 6 * D), F32),
        compiler_params=_params(("arbitrary", "arbitrary")),
        name="ada_mod",
    )(cv, ada_w, ada_b.reshape(DEPTH, 1, 6 * D))


def _even_in_kernel(h_ref, mod_ref, wna_ref, wrw_ref, qkv_ref, prw_ref, *, nB):
    b = pl.program_id(0)
    i = pl.program_id(1)
    row = jnp.where(i == 0, nB, b)
    u = (h_ref[...] * (1.0 + _mod_row(mod_ref, row, 1)) + _mod_row(mod_ref, row, 0)).astype(MM)
    res = jnp.dot(u, wna_ref[...], preferred_element_type=F32)
    qkv_ref[:, 0:NA_W] = (res[:, 0:NA_W] * 0.125).astype(qkv_ref.dtype)
    qkv_ref[:, NA_W:] = res[:, NA_W:].astype(qkv_ref.dtype)
    prw_ref[...] = jnp.dot(u, wrw_ref[...], preferred_element_type=F32)


def _even_in_call(h, mod, w_na, w_rw, nB):
    B, S, _ = h.shape
    R = mod.shape[0]
    return pl.pallas_call(
        functools.partial(_even_in_kernel, nB=nB),
        grid=(B, S // TM),
        in_specs=[pl.BlockSpec((None, TM, D), lambda b, i: (b, i, 0)),
                  pl.BlockSpec((R, 6 * D), lambda b, i: (0, 0)),
                  pl.BlockSpec((D, 3 * NA_W), lambda b, i: (0, 0)),
                  pl.BlockSpec((D, RW_IN), lambda b, i: (0, 0))],
        out_specs=[pl.BlockSpec((None, TM, 3 * NA_W), lambda b, i: (b, i, 0)),
                   pl.BlockSpec((None, TM, RW_IN), lambda b, i: (b, i, 0))],
        out_shape=[jax.ShapeDtypeStruct((B, S, 3 * NA_W), MM),
                   jax.ShapeDtypeStruct((B, S, RW_IN), F32)],
        compiler_params=_params(("parallel", "arbitrary")),
        name="even_in_proj",
    )(h, mod, w_na, w_rw)


def _na_tables(rows):
    reps = [0, 2, 4, rows - 4, rows - 2]
    qi = np.arange(NA_QB)
    kj = np.arange(NA_NLOC)
    dr = np.zeros((5, NA_QB, NA_NLOC), np.int32)
    dc = np.zeros((5, NA_QB, NA_NLOC), np.int32)
    ok = np.zeros((5, NA_QB, NA_NLOC), bool)
    for n, r0 in enumerate(reps):
        ws = int(np.clip(r0 - NA_WIN_R // 2, 0, rows - NA_KROWS))
        rq = (r0 + qi // GRID_W)[:, None]
        cq = (qi % GRID_W)[:, None]
        rk = (ws + kj // GRID_W)[None, :]
        ck = (kj % GRID_W)[None, :]
        rs = np.clip(rq - NA_WIN_R // 2, 0, rows - NA_WIN_R)
        cs = np.clip(cq - NA_WIN_C // 2, 0, GRID_W - NA_WIN_C)
        ok[n] = (rk >= rs) & (rk < rs + NA_WIN_R) & (ck >= cs) & (ck < cs + NA_WIN_C)
        dr[n] = np.clip(rk - rq + NA_WIN_R - 1, 0, 2 * NA_WIN_R - 2)
        dc[n] = np.clip(ck - cq + NA_WIN_C - 1, 0, 2 * NA_WIN_C - 2)
    return dr, dc, ok


def _na_kernel(q_ref, k_ref, v_ref, bias_ref, o_ref, *, L, rows):
    i = pl.program_id(2)
    nq = L // NA_QB
    lane = lax.broadcasted_iota(jnp.int32, (NA_QB, 128), 1)
    lo = lane < 64
    q = q_ref[...]
    zero = jnp.zeros_like(q)
    kc = k_ref[0:L, :]
    vc = v_ref[0:L, :]

    @pl.when(i < nq)
    def _():
        outs = []
        for hh in range(2):
            qm = jnp.where(lo if hh == 0 else jnp.logical_not(lo), q, zero)
            s = _dot_nt(qm, kc)
            m = jnp.max(s, axis=-1, keepdims=True)
            p = jnp.exp(s - m)
            l = jnp.sum(p, axis=-1, keepdims=True)
            outs.append(_dot(p, vc) * (1.0 / l))
        o_ref[...] = jnp.where(lo, outs[0], outs[1]).astype(o_ref.dtype)

    @pl.when(i >= nq)
    def _():
        r0 = 2 * (i - nq)
        ws = jnp.clip(r0 - NA_WIN_R // 2, 0, rows - NA_KROWS)
        start = pl.multiple_of(L + GRID_W * ws, GRID_W)
        kl = k_ref[pl.ds(start, NA_NLOC), :]
        vl = v_ref[pl.ds(start, NA_NLOC), :]
        outs = []
        for hh in range(2):
            qm = jnp.where(lo if hh == 0 else jnp.logical_not(lo), q, zero)
            s_loc = _dot_nt(qm, kl) + bias_ref[hh]
            s_ctx = _dot_nt(qm, kc)
            m = jnp.maximum(jnp.max(s_loc, axis=-1, keepdims=True), jnp.max(s_ctx, axis=-1, keepdims=True))
            p_loc = jnp.exp(s_loc - m)
            p_ctx = jnp.exp(s_ctx - m)
            l = jnp.sum(p_loc, axis=-1, keepdims=True) + jnp.sum(p_ctx, axis=-1, keepdims=True)
            outs.append((_dot(p_loc, vl) + _dot(p_ctx, vc)) * (1.0 / l))
        o_ref[...] = jnp.where(lo, outs[0], outs[1]).astype(o_ref.dtype)


def _na_call(qkv, bias, L):
    B, S, _ = qkv.shape
    rows = (S - L) // GRID_W
    nq = L // NA_QB
    nhp = NA_HEADS // 2

    def bias_idx(b, hp, i):
        r0 = 2 * (i - nq)
        c = jnp.where(r0 == 0, 0, jnp.where(r0 == 2, 1, jnp.where(r0 == rows - 4, 3, jnp.where(r0 == rows - 2, 4, 2))))
        return (jnp.where(i < nq, 2, c), hp, 0, 0)

    return pl.pallas_call(
        functools.partial(_na_kernel, L=L, rows=rows),
        grid=(B, nhp, S // NA_QB),
        in_specs=[pl.BlockSpec((None, NA_QB, 128), lambda b, hp, i: (b, i, hp)),
                  pl.BlockSpec((None, S, 128), lambda b, hp, i: (b, 0, nhp + hp)),
                  pl.BlockSpec((None, S, 128), lambda b, hp, i: (b, 0, 2 * nhp + hp)),
                  pl.BlockSpec((None, 2, NA_QB, NA_NLOC), bias_idx)],
        out_specs=pl.BlockSpec((None, NA_QB, 128), lambda b, hp, i: (b, i, hp)),
        out_shape=jax.ShapeDtypeStruct((B, S, NA_W), MM),
        compiler_params=_params(("parallel", "parallel", "arbitrary")),
        name="na_attention",
    )(qkv, qkv, qkv, bias)


def _rw_conv(p_ref, prev_ref, next_ref, c, conv_ref, nL, NC):
    p = p_ref[...]
    at_start = jnp.logical_or(c == 0, c == nL)
    at_end = jnp.logical_or(c == nL - 1, c == NC - 1)
    prow = jnp.where(at_start, 0.0, prev_ref[7:8, :])
    nrow = jnp.where(at_end, 0.0, next_ref[0:1, :])
    rid = lax.broadcasted_iota(jnp.int32, p.shape, 0)
    pm = jnp.where(rid == 0, prow, pltpu.roll(p, 1, axis=0))
    pp = jnp.where(rid == RW_CHUNK - 1, nrow, pltpu.roll(p, RW_CHUNK - 1, axis=0))
    return pm * conv_ref[0:1, :] + p * conv_ref[1:2, :] + pp * conv_ref[2:3, :]


def _rw_expand(z, bd):
    return jnp.where(bd, jnp.concatenate([z, z, z, z], axis=0), 0.0)


def _rw_direction(pc, d, first_row, w0_ref, wup_ref, a0_ref, aup_ref, kk_ref, ka_ref, ones_ref, cum_ref,
                  mask_ref, g_scr):
    r = pc[:, 0:RW_W]
    k = pc[:, RW_W:2 * RW_W]
    v = pc[:, 2 * RW_W:3 * RW_W]
    xwa = pc[:, 3 * RW_W:3 * RW_W + 128]
    z = w0_ref[d] + _dot(jnp.tanh(xwa), wup_ref[d])
    lw = -math.exp(-0.5) * _sigmoid(z)
    a = _sigmoid(a0_ref[d] + _dot(xwa, aup_ref[d]))
    kk = k * kk_ref[...]
    n2 = _dot_split_lhs(kk * kk, ones_ref[...])
    kk = kk * (1.0 / jnp.maximum(jnp.sqrt(n2), 1e-12))
    keff = k * (1.0 + (a - 1.0) * ka_ref[...])
    av = -kk
    bv = kk * a
    lc = _dot_split_rhs(cum_ref[d], lw)
    ltot = lc[RW_CHUNK - 1:RW_CHUNK, :] if first_row == 0 else lc[0:1, :]
    at = av * jnp.exp(lc - lw)
    rt = r * jnp.exp(lc)
    ginv = jnp.exp(-lc)
    bt = bv * ginv
    kt = keff * ginv
    grest = jnp.exp(ltot - lc)
    bh = bv * grest
    kh = keff * grest
    gc = jnp.exp(ltot)

    bd = mask_ref[0] > 0.5
    eye = mask_ref[1]
    ms = mask_ref[2 + 2 * d] > 0.5
    mi = mask_ref[3 + 2 * d] > 0.5
    ys = []
    for g in range(RW_W // RW_GRP):
        sl = slice(g * RW_GRP, (g + 1) * RW_GRP)
        e_at = _rw_expand(at[:, sl], bd).astype(MM)
        e_rt = _rw_expand(rt[:, sl], bd).astype(MM)
        e_bt = _rw_expand(bt[:, sl], bd).astype(MM)
        e_kt = _rw_expand(kt[:, sl], bd).astype(MM)
        e_bh = _rw_expand(bh[:, sl], bd).astype(MM)
        e_kh = _rw_expand(kh[:, sl], bd).astype(MM)
        e_v = _rw_expand(v[:, sl], bd)
        e_vt = e_v.T.astype(MM)
        ar = jnp.concatenate([e_at, e_rt], axis=0)
        xb = _dot_nt(e_bt, ar)
        xk = _dot_nt(e_kt, ar)
        xab = jnp.where(ms, xb[:, 0:RW_GRP], 0.0)
        xrb = jnp.where(mi, xb[:, RW_GRP:], 0.0)
        xak = jnp.where(ms, xk[:, 0:RW_GRP], 0.0)
        xrk = jnp.where(mi, xk[:, RW_GRP:], 0.0)
        tt = eye + xab
        xp = xab
        for _ in range(5):
            xp = _dot(xp, xp)
            tt = tt + _dot(tt, xp)
        gs = g_scr[d, g]
        gsb = gs.astype(MM)
        w1 = _dot_nt(gsb, e_at) + _dot(e_vt, xak)
        ut = _dot(w1, tt).astype(MM)
        yt = _dot_nt(gsb, e_rt) + _dot(ut, xrb) + _dot(e_vt, xrk)
        g_scr[d, g] = gs * gc[:, sl] + _dot(ut, e_bh) + _dot(e_vt, e_kh)
        ybd = yt.T
        ys.append(ybd[0:64] + ybd[64:128] + ybd[128:192] + ybd[192:256])
    return jnp.concatenate(ys, axis=1), keff, r, v


def _rw_kernel(pf_ref, pfp_ref, pfn_ref, pr_ref, prp_ref, prn_ref, conv_ref, w0_ref, wup_ref, a0_ref, aup_ref,
               gup_ref, kk_ref, ka_ref, rk_ref, ones_ref, cum_ref, mask_ref, yf_ref, yr_ref, bg_ref, g_scr,
               *, nL, NC):
    i = pl.program_id(1)

    @pl.when(i == 0)
    def _():
        g_scr[...] = jnp.zeros_like(g_scr)

    cf = i
    cr = jnp.where(i < nL, nL - 1 - i, NC - 1 - i + nL)
    args = (w0_ref, wup_ref, a0_ref, aup_ref, kk_ref, ka_ref, ones_ref, cum_ref, mask_ref, g_scr)

    pcf = _rw_conv(pf_ref, pfp_ref, pfn_ref, cf, conv_ref, nL, NC)
    y_f, keff_f, r_f, v_f = _rw_direction(pcf, 0, 0, *args)
    yf_ref[...] = y_f
    xwa = pcf[:, 3 * RW_W:3 * RW_W + 128]
    a_r = _sigmoid(a0_ref[1] + _dot(xwa, aup_ref[1]))
    keff_r = pcf[:, RW_W:2 * RW_W] * (1.0 + (a_r - 1.0) * ka_ref[...])
    bsum = _dot_split_lhs(r_f * rk_ref[...] * (keff_f + keff_r), ones_ref[...]) * v_f
    bg_ref[:, 0:RW_W] = bsum
    bg_ref[:, RW_W:] = _dot(_sigmoid(pcf[:, 3 * RW_W + 128:]), gup_ref[...])

    pcr = _rw_conv(pr_ref, prp_ref, prn_ref, cr, conv_ref, nL, NC)
    y_r, _, _, _ = _rw_direction(pcr, 1, RW_CHUNK - 1, *args)
    yr_ref[...] = y_r


def _rw_masks():
    n = 4 * RW_CHUNK
    idx = np.arange(n)
    hd = idx // RW_CHUNK
    t = idx % RW_CHUNK
    bd = hd[:, None] == hd[None, :]
    m = np.zeros((6, n, n), np.float32)
    m[0] = bd
    m[1] = np.eye(n)
    m[2] = bd & (t[:, None] < t[None, :])
    m[3] = bd & (t[:, None] <= t[None, :])
    m[4] = bd & (t[:, None] > t[None, :])
    m[5] = bd & (t[:, None] >= t[None, :])
    tt = np.arange(RW_CHUNK)
    cum = np.stack([tt[None, :] <= tt[:, None], tt[None, :] >= tt[:, None]]).astype(np.float32)
    ch = np.arange(RW_W) // RW_HEAD
    ones = (ch[:, None] == ch[None, :]).astype(np.float32)
    return m, cum, ones


def _rw_call(prw, conv_w, w0, wup_pad, a0, aup_pad, gup, kk, ka, rk, L):
    B, S, _ = prw.shape
    NC = S // RW_CHUNK
    nL = L // RW_CHUNK
    n8 = S // 8
    m, cum, ones = _rw_masks()

    def cr_of(i):
        return jnp.where(i < nL, nL - 1 - i, NC - 1 - i + nL)

    full = lambda shape: pl.BlockSpec(shape, lambda b, i: (0,) * len(shape))
    in_specs = [
        pl.BlockSpec((None, RW_CHUNK, RW_IN), lambda b, i: (b, i, 0)),
        pl.BlockSpec((None, 8, RW_IN), lambda b, i: (b, jnp.maximum(i * 8 - 1, 0), 0)),
        pl.BlockSpec((None, 8, RW_IN), lambda b, i: (b, jnp.minimum(i * 8 + 8, n8 - 1), 0)),
        pl.BlockSpec((None, RW_CHUNK, RW_IN), lambda b, i: (b, cr_of(i), 0)),
        pl.BlockSpec((None, 8, RW_IN), lambda b, i: (b, jnp.maximum(cr_of(i) * 8 - 1, 0), 0)),
        pl.BlockSpec((None, 8, RW_IN), lambda b, i: (b, jnp.minimum(cr_of(i) * 8 + 8, n8 - 1), 0)),
        full((3, RW_IN)), full((2, 1, RW_W)), full((2, 128, RW_W)), full((2, 1, RW_W)), full((2, 128, RW_W)),
        full((128, RW_W)), full((1, RW_W)), full((1, RW_W)), full((1, RW_W)),
        full((RW_W, RW_W)), full((2, RW_CHUNK, RW_CHUNK)), full((6, 4 * RW_CHUNK, 4 * RW_CHUNK)),
    ]
    out_specs = [
        pl.BlockSpec((None, RW_CHUNK, RW_W), lambda b, i: (b, i, 0)),
        pl.BlockSpec((None, RW_CHUNK, RW_W), lambda b, i: (b, cr_of(i), 0)),
        pl.BlockSpec((None, RW_CHUNK, 2 * RW_W), lambda b, i: (b, i, 0)),
    ]
    return pl.pallas_call(
        functools.partial(_rw_kernel, nL=nL, NC=NC),
        grid=(B, NC),
        in_specs=in_specs,
        out_specs=out_specs,
        out_shape=[jax.ShapeDtypeStruct((B, S, RW_W), F32),
                   jax.ShapeDtypeStruct((B, S, RW_W), F32),
                   jax.ShapeDtypeStruct((B, S, 2 * RW_W), F32)],
        scratch_shapes=[pltpu.VMEM((2, RW_W // RW_GRP, RW_GRP, RW_GRP), F32)],
        compiler_params=_params(("parallel", "arbitrary")),
        name="rwkv7_chunked",
    )(prw, prw, prw, prw, prw, prw, conv_w, w0, wup_pad, a0, aup_pad, gup, kk, ka, rk,
      jnp.asarray(ones, MM), jnp.asarray(cum, MM), jnp.asarray(m))


def _res_ln(h_ref, mod_ref, row, j, y, lng_ref, lnb_ref, o_ref):
    z = ALPHA * h_ref[...] + _mod_row(mod_ref, row, j) * y
    o_ref[...] = _layer_norm(z, lng_ref[...], lnb_ref[...])


def _even_out_kernel(h_ref, mod_ref, ona_ref, yf_ref, yr_ref, bg_ref, ones_ref, gng_ref, gnb_ref, wo_ref,
                     lng_ref, lnb_ref, o_ref, *, nB):
    b = pl.program_id(0)
    i = pl.program_id(1)
    row = jnp.where(i == 0, nB, b)
    y = yf_ref[...] + yr_ref[...]
    mu = _dot_split_lhs(y, ones_ref[...]) * (1.0 / RW_HEAD)
    yc = y - mu
    var = _dot_split_lhs(yc * yc, ones_ref[...]) * (1.0 / RW_HEAD)
    yn = yc * lax.rsqrt(var + RW_GN_EPS) * gng_ref[...] + gnb_ref[...]
    orw = (yn + bg_ref[:, 0:RW_W]) * bg_ref[:, RW_W:]
    yy = (jnp.dot(ona_ref[...], wo_ref[0:NA_W, :], preferred_element_type=F32)
          + jnp.dot(orw.astype(MM), wo_ref[NA_W:, :], preferred_element_type=F32))
    _res_ln(h_ref, mod_ref, row, 2, yy, lng_ref, lnb_ref, o_ref)


def _even_out_call(h, mod, ona, yf, yr, bg, gng, gnb, wo, lng, lnb, nB):
    B, S, _ = h.shape
    R = mod.shape[0]
    _, _, ones = _rw_masks()
    tile = lambda w: pl.BlockSpec((None, TM, w), lambda b, i: (b, i, 0))
    full = lambda shape: pl.BlockSpec(shape, lambda b, i: (0,) * len(shape))
    return pl.pallas_call(
        functools.partial(_even_out_kernel, nB=nB),
        grid=(B, S // TM),
        in_specs=[tile(D), full((R, 6 * D)), tile(NA_W), tile(RW_W), tile(RW_W), tile(2 * RW_W),
                  full((RW_W, RW_W)), full((1, RW_W)), full((1, RW_W)), full((D, D)), full((1, D)), full((1, D))],
        out_specs=tile(D),
        out_shape=jax.ShapeDtypeStruct((B, S, D), F32),
        compiler_params=_params(("parallel", "arbitrary")),
        name="even_out_proj_ln",
    )(h, mod, ona, yf, yr, bg, jnp.asarray(ones, MM), gng, gnb, wo, lng, lnb)


def _odd_out_kernel(h_ref, mod_ref, oda_ref, wo_ref, lng_ref, lnb_ref, o_ref, *, nB):
    b = pl.program_id(0)
    i = pl.program_id(1)
    row = jnp.where(i == 0, nB, b)
    yy = jnp.dot(oda_ref[...], wo_ref[...], preferred_element_type=F32)
    _res_ln(h_ref, mod_ref, row, 2, yy, lng_ref, lnb_ref, o_ref)


def _odd_out_call(h, mod, oda, wo, lng, lnb, nB):
    B, S, _ = h.shape
    R = mod.shape[0]
    tile = lambda w: pl.BlockSpec((None, TM, w), lambda b, i: (b, i, 0))
    full = lambda shape: pl.BlockSpec(shape, lambda b, i: (0,) * len(shape))
    return pl.pallas_call(
        functools.partial(_odd_out_kernel, nB=nB),
        grid=(B, S // TM),
        in_specs=[tile(D), full((R, 6 * D)), tile(DA_W), full((DA_W, D)), full((1, D)), full((1, D))],
        out_specs=tile(D),
        out_shape=jax.ShapeDtypeStruct((B, S, D), F32),
        compiler_params=_params(("parallel", "arbitrary")),
        name="odd_out_proj_ln",
    )(h, mod, oda, wo, lng, lnb)


def _odd_in_kernel(h_ref, mod_ref, w_ref, cos_ref, sin_ref, o_ref, *, nB):
    b = pl.program_id(0)
    i = pl.program_id(1)
    row = jnp.where(i == 0, nB, b)
    u = (h_ref[...] * (1.0 + _mod_row(mod_ref, row, 1)) + _mod_row(mod_ref, row, 0)).astype(MM)
    res = jnp.dot(u, w_ref[...], preferred_element_type=F32)
    cs = cos_ref[...]
    sn = sin_ref[...]
    lane = lax.broadcasted_iota(jnp.int32, cs.shape, 1)
    first = (lane % 32) < 16
    for j in range(2 * DA_W // 128):
        zj = res[:, j * 128:(j + 1) * 128]
        sw = jnp.where(first, pltpu.roll(zj, 112, axis=1), pltpu.roll(zj, 16, axis=1))
        rot = zj * cs + sw * sn
        if j < DA_W // 128:
            rot = rot * 0.125
        o_ref[:, j * 128:(j + 1) * 128] = rot.astype(o_ref.dtype)
    o_ref[:, 2 * DA_W:] = res[:, 2 * DA_W:].astype(o_ref.dtype)


def _odd_in_call(h, mod, w, cos_t, sin_t, nB):
    B, S, _ = h.shape
    R = mod.shape[0]
    return pl.pallas_call(
        functools.partial(_odd_in_kernel, nB=nB),
        grid=(B, S // TM),
        in_specs=[pl.BlockSpec((None, TM, D), lambda b, i: (b, i, 0)),
                  pl.BlockSpec((R, 6 * D), lambda b, i: (0, 0)),
                  pl.BlockSpec((D, 3 * DA_W), lambda b, i: (0, 0)),
                  pl.BlockSpec((TM, 128), lambda b, i: (i, 0)),
                  pl.BlockSpec((TM, 128), lambda b, i: (i, 0))],
        out_specs=pl.BlockSpec((None, TM, 3 * DA_W), lambda b, i: (b, i, 0)),
        out_shape=jax.ShapeDtypeStruct((B, S, 3 * DA_W), MM),
        compiler_params=_params(("parallel", "arbitrary")),
        name="odd_in_proj_rope",
    )(h, mod, w, cos_t, sin_t)


def _rope_tables(L, T):
    nf = 16
    inv = ROPE_THETA ** (-jnp.arange(nf, dtype=F32) / nf)
    t = jnp.arange(T)
    ang_r = (t // GRID_W).astype(F32)[:, None] * inv
    ang_c = (t % GRID_W).astype(F32)[:, None] * inv
    cos64 = jnp.concatenate([jnp.cos(ang_r), jnp.cos(ang_r), jnp.cos(ang_c), jnp.cos(ang_c)], -1)
    sin64 = jnp.concatenate([-jnp.sin(ang_r), jnp.sin(ang_r), -jnp.sin(ang_c), jnp.sin(ang_c)], -1)
    cos_t = jnp.concatenate([jnp.ones((L, 128), F32), jnp.tile(cos64, (1, 2))], 0)
    sin_t = jnp.concatenate([jnp.zeros((L, 128), F32), jnp.tile(sin64, (1, 2))], 0)
    return cos_t, sin_t


def _da_kernel(q_ref, k_ref, v_ref, lq1_ref, lk1_ref, lq2_ref, lk2_ref, sg_ref, o_ref, m_scr, l_scr, acc_scr,
               *, L, n_lat, lam_init):
    i = pl.program_id(2)
    q = q_ref[...]
    lane = lax.broadcasted_iota(jnp.int32, q.shape, 1)
    lo = lane < 64
    zero = jnp.zeros_like(q)
    qq = jnp.concatenate([jnp.where(lo, q, zero), jnp.where(lo, zero, q)], axis=0)

    def update(kb, vb, first):
        s = _dot_nt(qq, kb)
        m_cur = jnp.max(s, axis=-1, keepdims=True)
        if first:
            m_new = m_cur
            p = jnp.exp(s - m_new)
            l_scr[...] = jnp.sum(p, axis=-1, keepdims=True)
            acc_scr[...] = _dot(p, vb)
        else:
            m_old = m_scr[...]
            m_new = jnp.maximum(m_old, m_cur)
            alpha = jnp.exp(m_old - m_new)
            p = jnp.exp(s - m_new)
            l_scr[...] = alpha * l_scr[...] + jnp.sum(p, axis=-1, keepdims=True)
            acc_scr[...] = alpha * acc_scr[...] + _dot(p, vb)
        m_scr[...] = m_new

    update(k_ref[0:L, :], v_ref[0:L, :], True)

    @pl.when(i > 0)
    def _():
        def body(j, carry):
            start = pl.multiple_of(L + j * DA_TK, DA_TK)
            update(k_ref[pl.ds(start, DA_TK), :], v_ref[pl.ds(start, DA_TK), :], False)
            return carry
        lax.fori_loop(0, n_lat, body, 0)

    lam = (jnp.exp(jnp.sum(lq1_ref[...] * lk1_ref[...], axis=-1, keepdims=True))
           - jnp.exp(jnp.sum(lq2_ref[...] * lk2_ref[...], axis=-1, keepdims=True)) + lam_init)
    o_all = acc_scr[...] * (1.0 / l_scr[...])
    o = o_all[0:DA_TQ] - lam * o_all[DA_TQ:]
    o = o * lax.rsqrt(jnp.mean(o * o, axis=-1, keepdims=True) + 1e-5) * sg_ref[...] * (1.0 - lam_init)
    o_ref[...] = o.astype(o_ref.dtype)


def _da_call(qkv, lq1, lk1, lq2, lk2, sg, L, lam_init):
    B, S, _ = qkv.shape
    n_lat = (S - L) // DA_TK
    full = lambda shape: pl.BlockSpec(shape, lambda b, h, i: (0,) * len(shape))
    return pl.pallas_call(
        functools.partial(_da_kernel, L=L, n_lat=n_lat, lam_init=lam_init),
        grid=(B, DA_HEADS, S // DA_TQ),
        in_specs=[pl.BlockSpec((None, DA_TQ, 128), lambda b, h, i: (b, i, h)),
                  pl.BlockSpec((None, S, 128), lambda b, h, i: (b, 0, DA_HEADS + h)),
                  pl.BlockSpec((None, S, 128), lambda b, h, i: (b, 0, 2 * DA_HEADS + h)),
                  full((1, 64)), full((1, 64)), full((1, 64)), full((1, 64)), full((1, 128))],
        out_specs=pl.BlockSpec((None, DA_TQ, 128), lambda b, h, i: (b, i, h)),
        out_shape=jax.ShapeDtypeStruct((B, S, DA_W), MM),
        scratch_shapes=[pltpu.VMEM((2 * DA_TQ, 1), F32), pltpu.VMEM((2 * DA_TQ, 1), F32),
                        pltpu.VMEM((2 * DA_TQ, 128), F32)],
        compiler_params=_params(("parallel", "parallel", "arbitrary")),
        name="diff_attention",
    )(qkv, qkv, qkv, lq1, lk1, lq2, lk2, sg)


def _route(logits_t, rb):
    s = _sigmoid(logits_t)
    sel = s + rb
    x = [sel[e:e + 1, :] for e in range(N_EXPERTS)]
    tg = []
    for g in range(N_GROUPS):
        best = None
        for a in range(EPG):
            for b in range(a + 1, EPG):
                pair = x[EPG * g + a] + x[EPG * g + b]
                best = pair if best is None else jnp.maximum(best, pair)
        tg.append(best)
    rows = []
    for g in range(N_GROUPS):
        chosen = None
        for g2 in range(N_GROUPS):
            if g2 == g:
                continue
            c = (tg[g] > tg[g2]) if g2 < g else (tg[g] >= tg[g2])
            chosen = c if chosen is None else jnp.logical_and(chosen, c)
        for a in range(EPG):
            rank = jnp.zeros_like(x[0])
            for b in range(EPG):
                if b == a:
                    continue
                ahead = (x[EPG * g + b] >= x[EPG * g + a]) if b < a else (x[EPG * g + b] > x[EPG * g + a])
                rank = rank + jnp.where(ahead, 1.0, 0.0)
            picked = jnp.logical_and(chosen, rank < 1.5)
            rows.append(jnp.where(picked, s[EPG * g + a:EPG * g + a + 1, :], 0.0))
    comb = jnp.concatenate(rows, axis=0)
    return comb * (1.0 / jnp.sum(comb, axis=0, keepdims=True))


def _moe_kernel(h_ref, mod_ref, rwt_ref, rb_ref, wg_ref, wu_ref, wd_ref, lng_ref, lnb_ref, o_ref,
                u_scr, comb_scr, acc_scr, *, nB, tile0):
    b = pl.program_id(0)
    i = pl.program_id(1)
    e = pl.program_id(2)
    row = jnp.where(i + tile0 == 0, nB, b)

    @pl.when(e == 0)
    def _():
        u = h_ref[...] * (1.0 + _mod_row(mod_ref, row, 4)) + _mod_row(mod_ref, row, 3)
        u_scr[...] = u.astype(MM)
        logits_t = lax.dot_general(rwt_ref[...], u, (((1,), (1,)), ((), ())), precision=HIGHEST,
                                   preferred_element_type=F32)
        comb_t = _route(logits_t, rb_ref[...])
        comb_t = jnp.concatenate([comb_t, jnp.zeros((128 - N_EXPERTS, TM), F32)], axis=0)
        comb_scr[...] = comb_t.T
        acc_scr[...] = jnp.zeros_like(acc_scr)

    u = u_scr[...]
    lane = lax.broadcasted_iota(jnp.int32, (TM, 128), 1)
    w_e = jnp.sum(jnp.where(lane == e, comb_scr[...], 0.0), axis=-1, keepdims=True)
    hg = jnp.dot(u, wg_ref[...], preferred_element_type=F32)
    hu = jnp.dot(u, wu_ref[...], preferred_element_type=F32)
    he = hg * _sigmoid(hg) * hu * w_e
    acc_scr[...] += jnp.dot(he.astype(MM), wd_ref[...], preferred_element_type=F32)

    @pl.when(e == N_EXPERTS - 1)
    def _():
        _res_ln(h_ref, mod_ref, row, 5, acc_scr[...], lng_ref, lnb_ref, o_ref)


def _moe_call(h, mod, rwt, rb, wg, wu, wd, lng, lnb, nB, latent_only, L):
    B, S, _ = h.shape
    R = mod.shape[0]
    tile0 = L // TM if latent_only else 0
    n_tiles = S // TM - tile0
    full = lambda shape: pl.BlockSpec(shape, lambda b, i, e: (0,) * len(shape))
    return pl.pallas_call(
        functools.partial(_moe_kernel, nB=nB, tile0=tile0),
        grid=(B, n_tiles, N_EXPERTS),
        in_specs=[pl.BlockSpec((None, TM, D), lambda b, i, e: (b, i + tile0, 0)),
                  full((R, 6 * D)), full((N_EXPERTS, D)), full((N_EXPERTS, 1)),
                  pl.BlockSpec((None, D, D_EXPERT), lambda b, i, e: (e, 0, 0)),
                  pl.BlockSpec((None, D, D_EXPERT), lambda b, i, e: (e, 0, 0)),
                  pl.BlockSpec((None, D_EXPERT, D), lambda b, i, e: (e, 0, 0)),
                  full((1, D)), full((1, D))],
        out_specs=pl.BlockSpec((None, TM, D), lambda b, i, e: (b, i, 0)),
        out_shape=jax.ShapeDtypeStruct((B, n_tiles * TM, D), F32),
        scratch_shapes=[pltpu.VMEM((TM, D), MM), pltpu.VMEM((TM, 128), F32), pltpu.VMEM((TM, D), F32)],
        compiler_params=_params(("parallel", "parallel", "arbitrary")),
        name="moe_dense",
    )(h, mod, rwt, rb, wg, wu, wd, lng, lnb)


def kernel(x, c, ctx, c_ctx, ada_w, ada_b, ln_g, ln_b, even_w_in, even_w_out, shift_w, na_rpb, rw_w0, rw_w_up, rw_a0, rw_a_up, rw_g_up, rw_k_k, rw_k_a, rw_r_k, rw_gn_g, rw_gn_b, odd_w_in, odd_w_out, da_lq1, da_lk1, da_lq2, da_lk2, da_subln_g, router_w, router_b, exp_w_gate, exp_w_up, exp_w_down):
    B, T, _ = x.shape
    L = ctx.shape[1]
    assert L == TM and T % DA_TK == 0 and (T // GRID_W) >= 12
    rows = T // GRID_W
    R = -(-(B + 1) // 8) * 8

    cv = jnp.zeros((R, D), F32).at[:B].set(c).at[B].set(c_ctx)
    mods = _ada_call(cv, ada_w, ada_b)

    dr, dc, ok = _na_tables(rows)
    cos_t, sin_t = _rope_tables(L, T)
    rwt = router_w.T
    rb = router_b.reshape(N_EXPERTS, 1)
    zpad = jnp.zeros((2, 64, RW_W), F32)

    h = jnp.concatenate([ctx, x], axis=1)
    for l in range(DEPTH):
        mod = mods[l]
        i = l // 2
        lng = ln_g[l].reshape(2, 1, D)
        lnb = ln_b[l].reshape(2, 1, D)
        if l % 2 == 0:
            w_in = even_w_in[i].astype(MM)
            qkv, prw = _even_in_call(h, mod, w_in[:, :3 * NA_W], w_in[:, 3 * NA_W:], B)
            bias = jnp.where(ok[None], na_rpb[i][:, dr, dc], NEG).transpose(1, 0, 2, 3)
            ona = _na_call(qkv, bias, L)
            yf, yr, bg = _rw_call(
                prw, shift_w[i], rw_w0[i].reshape(2, 1, RW_W),
                jnp.concatenate([rw_w_up[i], zpad], axis=1).astype(MM), rw_a0[i].reshape(2, 1, RW_W),
                jnp.concatenate([zpad, rw_a_up[i]], axis=1).astype(MM), rw_g_up[i].astype(MM),
                rw_k_k[i].reshape(1, RW_W), rw_k_a[i].reshape(1, RW_W), rw_r_k[i].reshape(1, RW_W), L)
            h = _even_out_call(h, mod, ona, yf, yr, bg, rw_gn_g[i].reshape(1, RW_W), rw_gn_b[i].reshape(1, RW_W),
                               even_w_out[i].astype(MM), lng[0], lnb[0], B)
        else:
            lam_init = 0.8 - 0.6 * math.exp(-0.3 * l)
            qkv = _odd_in_call(h, mod, odd_w_in[i].astype(MM), cos_t, sin_t, B)
            oda = _da_call(qkv, da_lq1[i].reshape(1, 64), da_lk1[i].reshape(1, 64), da_lq2[i].reshape(1, 64),
                           da_lk2[i].reshape(1, 64), da_subln_g[i].reshape(1, 128), L, lam_init)
            h = _odd_out_call(h, mod, oda, odd_w_out[i].astype(MM), lng[0], lnb[0], B)
        h = _moe_call(h, mod, rwt, rb, exp_w_gate[l].astype(MM), exp_w_up[l].astype(MM), exp_w_down[l].astype(MM),
                      lng[1], lnb[1], B, l == DEPTH - 1, L)
    return h
```

```python
import functools
import math

import numpy as np
import jax
import jax.numpy as jnp
from jax import lax
from jax.experimental import pallas as pl
from jax.experimental.pallas import tpu as pltpu

F32 = jnp.float32
MM = jnp.bfloat16
HIGHEST = lax.Precision.HIGHEST

D = 1024
DEPTH = 4
GRID_W = 64
NA_HEADS = 8
NA_W = 512
NA_WIN_R = 8
NA_WIN_C = 16
NA_QB = 128
NA_KROWS = 9
NA_NLOC = NA_KROWS * GRID_W
RW_W = 512
RW_HEAD = 64
RW_IN = 1792
RW_GN_EPS = 64e-5
RW_CHUNK = 64
RW_GRP = 256
DA_HEADS = 8
DA_W = 1024
ROPE_THETA = 10000.0
N_EXPERTS = 16
N_GROUPS = 4
EPG = 4
D_EXPERT = 512
MOE_RT_ROWS = 32
MOE_TE = 512
MOE_BLK = 128
ALPHA = (2 * DEPTH) ** 0.25
LN_EPS = 1e-5
TM = 256
DA_TQ = 256
DA_TK = 512
NEG = -1e30
VMEM_LIMIT = 56 * 1024 * 1024


def _dot(a, b):
    return jnp.dot(a.astype(MM), b.astype(MM), preferred_element_type=F32)


def _dot_nt(a, b):
    return lax.dot_general(a.astype(MM), b.astype(MM), (((1,), (1,)), ((), ())), preferred_element_type=F32)


def _split(x):
    hi = x.astype(MM)
    lo = (x - hi.astype(F32)).astype(MM)
    return hi, lo


def _dot_split_lhs(x, w):
    hi, lo = _split(x)
    return jnp.dot(hi, w, preferred_element_type=F32) + jnp.dot(lo, w, preferred_element_type=F32)


def _dot_split_rhs(w, x):
    hi, lo = _split(x)
    return jnp.dot(w, hi, preferred_element_type=F32) + jnp.dot(w, lo, preferred_element_type=F32)


def _sigmoid(x):
    return 1.0 / (1.0 + jnp.exp(-x))


def _params(sem):
    return pltpu.CompilerParams(dimension_semantics=sem, vmem_limit_bytes=VMEM_LIMIT)


def _mod_row(mod_ref, row, j):
    return mod_ref[pl.ds(row, 1), j * D:(j + 1) * D]


def _layer_norm(z, g, b):
    mu = jnp.mean(z, axis=-1, keepdims=True)
    zc = z - mu
    var = jnp.mean(zc * zc, axis=-1, keepdims=True)
    return zc * lax.rsqrt(var + LN_EPS) * g + b


def _ada_kernel(cv_ref, w_ref, b_ref, o_ref):
    x = cv_ref[...]
    x = x * _sigmoid(x)
    o_ref[...] = jnp.dot(x, w_ref[...], precision=HIGHEST, preferred_element_type=F32) + b_ref[...]


def _ada_call(cv, ada_w, ada_b):
    R = cv.shape[0]
    tn = 1536
    return pl.pallas_call(
        _ada_kernel,
        grid=(DEPTH, 6 * D // tn),
        in_specs=[pl.BlockSpec((R, D), lambda l, j: (0, 0)),
                  pl.BlockSpec((None, D, tn), lambda l, j: (l, 0, j)),
                  pl.BlockSpec((None, 1, tn), lambda l, j: (l, 0, j))],
        out_specs=pl.BlockSpec((None, R, tn), lambda l, j: (l, 0, j)),
        out_shape=jax.ShapeDtypeStruct((DEPTH, R, 6 * D), F32),
        compiler_params=_params(("arbitrary", "arbitrary")),
        name="ada_mod",
    )(cv, ada_w, ada_b.reshape(DEPTH, 1, 6 * D))


def _even_in_kernel(h_ref, mod_ref, wna_ref, wrw_ref, qkv_ref, prw_ref, *, nB):
    b = pl.program_id(0)
    i = pl.program_id(1)
    row = jnp.where(i == 0, nB, b)
    u = (h_ref[...] * (1.0 + _mod_row(mod_ref, row, 1)) + _mod_row(mod_ref, row, 0)).astype(MM)
    res = jnp.dot(u, wna_ref[...], preferred_element_type=F32)
    qkv_ref[:, 0:NA_W] = (res[:, 0:NA_W] * 0.125).astype(qkv_ref.dtype)
    qkv_ref[:, NA_W:] = res[:, NA_W:].astype(qkv_ref.dtype)
    prw_ref[...] = jnp.dot(u, wrw_ref[...], preferred_element_type=F32)


def _even_in_call(h, mod, w_na, w_rw, nB):
    B, S, _ = h.shape
    R = mod.shape[0]
    return pl.pallas_call(
        functools.partial(_even_in_kernel, nB=nB),
        grid=(B, S // TM),
        in_specs=[pl.BlockSpec((None, TM, D), lambda b, i: (b, i, 0)),
                  pl.BlockSpec((R, 6 * D), lambda b, i: (0, 0)),
                  pl.BlockSpec((D, 3 * NA_W), lambda b, i: (0, 0)),
                  pl.BlockSpec((D, RW_IN), lambda b, i: (0, 0))],
        out_specs=[pl.BlockSpec((None, TM, 3 * NA_W), lambda b, i: (b, i, 0)),
                   pl.BlockSpec((None, TM, RW_IN), lambda b, i: (b, i, 0))],
        out_shape=[jax.ShapeDtypeStruct((B, S, 3 * NA_W), MM),
                   jax.ShapeDtypeStruct((B, S, RW_IN), F32)],
        compiler_params=_params(("parallel", "arbitrary")),
        name="even_in_proj",
    )(h, mod, w_na, w_rw)


def _na_bias(rpb, rows):
    cq = np.arange(GRID_W)[:, None]
    ck = np.arange(GRID_W)[None, :]
    cs = np.clip(cq - NA_WIN_C // 2, 0, GRID_W - NA_WIN_C)
    ok_c = (ck >= cs) & (ck < cs + NA_WIN_C)
    dc = np.clip(ck - cq + NA_WIN_C - 1, 0, 2 * NA_WIN_C - 2)
    onehot = np.zeros((2 * NA_WIN_C - 1, GRID_W * GRID_W), np.float32)
    onehot[dc.ravel(), np.arange(GRID_W * GRID_W)] = 1.0
    rc = jnp.einsum('hrd,dn->hrn', rpb, jnp.asarray(onehot), precision=HIGHEST)
    rc = rc.reshape(rpb.shape[0], 2 * NA_WIN_R - 1, GRID_W, GRID_W)
    cases = []
    for r0 in (0, 2, 4, rows - 4, rows - 2):
        ws = int(np.clip(r0 - NA_WIN_R // 2, 0, rows - NA_KROWS))
        qrows = []
        for qi in range(NA_QB // GRID_W):
            rq = r0 + qi
            rs = int(np.clip(rq - NA_WIN_R // 2, 0, rows - NA_WIN_R))
            pieces = []
            for j in range(NA_KROWS):
                rk = ws + j
                if rs <= rk < rs + NA_WIN_R:
                    pieces.append(jnp.where(ok_c[None], rc[:, rk - rq + NA_WIN_R - 1], NEG))
                else:
                    pieces.append(jnp.full((rpb.shape[0], GRID_W, GRID_W), NEG, F32))
            qrows.append(jnp.concatenate(pieces, axis=-1))
        cases.append(jnp.concatenate(qrows, axis=-2))
    return jnp.stack(cases, axis=0)


def _na_kernel(q_ref, k_ref, v_ref, bias_ref, o_ref, *, L, rows):
    i = pl.program_id(2)
    nq = L // NA_QB
    lane = lax.broadcasted_iota(jnp.int32, (NA_QB, 128), 1)
    lo = lane < 64
    q = q_ref[...]
    zero = jnp.zeros_like(q)
    kc = k_ref[0:L, :]
    vc = v_ref[0:L, :]

    @pl.when(i < nq)
    def _():
        outs = []
        for hh in range(2):
            qm = jnp.where(lo if hh == 0 else jnp.logical_not(lo), q, zero)
            s = _dot_nt(qm, kc)
            m = jnp.max(s, axis=-1, keepdims=True)
            p = jnp.exp(s - m)
            l = jnp.sum(p, axis=-1, keepdims=True)
            outs.append(_dot(p, vc) * (1.0 / l))
        o_ref[...] = jnp.where(lo, outs[0], outs[1]).astype(o_ref.dtype)

    @pl.when(i >= nq)
    def _():
        r0 = 2 * (i - nq)
        ws = jnp.clip(r0 - NA_WIN_R // 2, 0, rows - NA_KROWS)
        start = pl.multiple_of(L + GRID_W * ws, GRID_W)
        kl = k_ref[pl.ds(start, NA_NLOC), :]
        vl = v_ref[pl.ds(start, NA_NLOC), :]
        outs = []
        for hh in range(2):
            qm = jnp.where(lo if hh == 0 else jnp.logical_not(lo), q, zero)
            s_loc = _dot_nt(qm, kl) + bias_ref[hh]
            s_ctx = _dot_nt(qm, kc)
            m = jnp.maximum(jnp.max(s_loc, axis=-1, keepdims=True), jnp.max(s_ctx, axis=-1, keepdims=True))
            p_loc = jnp.exp(s_loc - m)
            p_ctx = jnp.exp(s_ctx - m)
            l = jnp.sum(p_loc, axis=-1, keepdims=True) + jnp.sum(p_ctx, axis=-1, keepdims=True)
            outs.append((_dot(p_loc, vl) + _dot(p_ctx, vc)) * (1.0 / l))
        o_ref[...] = jnp.where(lo, outs[0], outs[1]).astype(o_ref.dtype)


def _na_call(qkv, bias, L):
    B, S, _ = qkv.shape
    rows = (S - L) // GRID_W
    nq = L // NA_QB
    nhp = NA_HEADS // 2

    def bias_idx(b, hp, i):
        r0 = 2 * (i - nq)
        c = jnp.where(r0 == 0, 0, jnp.where(r0 == 2, 1, jnp.where(r0 == rows - 4, 3, jnp.where(r0 == rows - 2, 4, 2))))
        return (jnp.where(i < nq, 2, c), hp, 0, 0)

    return pl.pallas_call(
        functools.partial(_na_kernel, L=L, rows=rows),
        grid=(B, nhp, S // NA_QB),
        in_specs=[pl.BlockSpec((None, NA_QB, 128), lambda b, hp, i: (b, i, hp)),
                  pl.BlockSpec((None, S, 128), lambda b, hp, i: (b, 0, nhp + hp)),
                  pl.BlockSpec((None, S, 128), lambda b, hp, i: (b, 0, 2 * nhp + hp)),
                  pl.BlockSpec((None, 2, NA_QB, NA_NLOC), bias_idx)],
        out_specs=pl.BlockSpec((None, NA_QB, 128), lambda b, hp, i: (b, i, hp)),
        out_shape=jax.ShapeDtypeStruct((B, S, NA_W), MM),
        compiler_params=_params(("parallel", "parallel", "arbitrary")),
        name="na_attention",
    )(qkv, qkv, qkv, bias)


def _rw_conv(p_ref, prev_ref, next_ref, c, conv_ref, nL, NC):
    p = p_ref[...]
    at_start = jnp.logical_or(c == 0, c == nL)
    at_end = jnp.logical_or(c == nL - 1, c == NC - 1)
    prow = jnp.where(at_start, 0.0, prev_ref[7:8, :])
    nrow = jnp.where(at_end, 0.0, next_ref[0:1, :])
    rid = lax.broadcasted_iota(jnp.int32, p.shape, 0)
    pm = jnp.where(rid == 0, prow, pltpu.roll(p, 1, axis=0))
    pp = jnp.where(rid == RW_CHUNK - 1, nrow, pltpu.roll(p, RW_CHUNK - 1, axis=0))
    return pm * conv_ref[0:1, :] + p * conv_ref[1:2, :] + pp * conv_ref[2:3, :]


def _rw_expand(z, bd):
    return jnp.where(bd, jnp.concatenate([z, z, z, z], axis=0), 0.0)


def _rw_direction(pc, d, first_row, w0_ref, wup_ref, a0_ref, aup_ref, kk_ref, ka_ref, ones_ref, cum_ref,
                  mask_ref, g_scr):
    r = pc[:, 0:RW_W]
    k = pc[:, RW_W:2 * RW_W]
    v = pc[:, 2 * RW_W:3 * RW_W]
    xwa = pc[:, 3 * RW_W:3 * RW_W + 128]
    z = w0_ref[d] + _dot(jnp.tanh(xwa), wup_ref[d])
    lw = -math.exp(-0.5) * _sigmoid(z)
    a = _sigmoid(a0_ref[d] + _dot(xwa, aup_ref[d]))
    kk = k * kk_ref[...]
    n2 = _dot_split_lhs(kk * kk, ones_ref[...])
    kk = kk * (1.0 / jnp.maximum(jnp.sqrt(n2), 1e-12))
    keff = k * (1.0 + (a - 1.0) * ka_ref[...])
    av = -kk
    bv = kk * a
    lc = _dot_split_rhs(cum_ref[d], lw)
    ltot = lc[RW_CHUNK - 1:RW_CHUNK, :] if first_row == 0 else lc[0:1, :]
    at = av * jnp.exp(lc - lw)
    rt = r * jnp.exp(lc)
    ginv = jnp.exp(-lc)
    bt = bv * ginv
    kt = keff * ginv
    grest = jnp.exp(ltot - lc)
    bh = bv * grest
    kh = keff * grest
    gc = jnp.exp(ltot)

    bd = mask_ref[0] > 0.5
    eye = mask_ref[1]
    ms = mask_ref[2 + 2 * d] > 0.5
    mi = mask_ref[3 + 2 * d] > 0.5
    ys = []
    for g in range(RW_W // RW_GRP):
        sl = slice(g * RW_GRP, (g + 1) * RW_GRP)
        e_at = _rw_expand(at[:, sl], bd).astype(MM)
        e_rt = _rw_expand(rt[:, sl], bd).astype(MM)
        e_bt = _rw_expand(bt[:, sl], bd).astype(MM)
        e_kt = _rw_expand(kt[:, sl], bd).astype(MM)
        e_bh = _rw_expand(bh[:, sl], bd).astype(MM)
        e_kh = _rw_expand(kh[:, sl], bd).astype(MM)
        e_v = _rw_expand(v[:, sl], bd)
        e_vt = e_v.T.astype(MM)
        ar = jnp.concatenate([e_at, e_rt], axis=0)
        xb = _dot_nt(e_bt, ar)
        xk = _dot_nt(e_kt, ar)
        xab = jnp.where(ms, xb[:, 0:RW_GRP], 0.0)
        xrb = jnp.where(mi, xb[:, RW_GRP:], 0.0)
        xak = jnp.where(ms, xk[:, 0:RW_GRP], 0.0)
        xrk = jnp.where(mi, xk[:, RW_GRP:], 0.0)
        tt = eye + xab
        xp = xab
        for _ in range(5):
            xp = _dot(xp, xp)
            tt = tt + _dot(tt, xp)
        gs = g_scr[d, g]
        gsb = gs.astype(MM)
        w1 = _dot_nt(gsb, e_at) + _dot(e_vt, xak)
        ut = _dot(w1, tt).astype(MM)
        yt = _dot_nt(gsb, e_rt) + _dot(ut, xrb) + _dot(e_vt, xrk)
        g_scr[d, g] = gs * gc[:, sl] + _dot(ut, e_bh) + _dot(e_vt, e_kh)
        ybd = yt.T
        ys.append(ybd[0:64] + ybd[64:128] + ybd[128:192] + ybd[192:256])
    return jnp.concatenate(ys, axis=1), keff, r, v


def _rw_kernel(pf_ref, pfp_ref, pfn_ref, pr_ref, prp_ref, prn_ref, conv_ref, w0_ref, wup_ref, a0_ref, aup_ref,
               gup_ref, kk_ref, ka_ref, rk_ref, ones_ref, cum_ref, mask_ref, yf_ref, yr_ref, bg_ref, g_scr,
               *, nL, NC):
    i = pl.program_id(1)

    @pl.when(i == 0)
    def _():
        g_scr[...] = jnp.zeros_like(g_scr)

    cf = i
    cr = jnp.where(i < nL, nL - 1 - i, NC - 1 - i + nL)
    args = (w0_ref, wup_ref, a0_ref, aup_ref, kk_ref, ka_ref, ones_ref, cum_ref, mask_ref, g_scr)

    pcf = _rw_conv(pf_ref, pfp_ref, pfn_ref, cf, conv_ref, nL, NC)
    y_f, keff_f, r_f, v_f = _rw_direction(pcf, 0, 0, *args)
    yf_ref[...] = y_f
    xwa = pcf[:, 3 * RW_W:3 * RW_W + 128]
    a_r = _sigmoid(a0_ref[1] + _dot(xwa, aup_ref[1]))
    keff_r = pcf[:, RW_W:2 * RW_W] * (1.0 + (a_r - 1.0) * ka_ref[...])
    bsum = _dot_split_lhs(r_f * rk_ref[...] * (keff_f + keff_r), ones_ref[...]) * v_f
    bg_ref[:, 0:RW_W] = bsum
    bg_ref[:, RW_W:] = _dot(_sigmoid(pcf[:, 3 * RW_W + 128:]), gup_ref[...])

    pcr = _rw_conv(pr_ref, prp_ref, prn_ref, cr, conv_ref, nL, NC)
    y_r, _, _, _ = _rw_direction(pcr, 1, RW_CHUNK - 1, *args)
    yr_ref[...] = y_r


def _rw_masks():
    n = 4 * RW_CHUNK
    idx = np.arange(n)
    hd = idx // RW_CHUNK
    t = idx % RW_CHUNK
    bd = hd[:, None] == hd[None, :]
    m = np.zeros((6, n, n), np.float32)
    m[0] = bd
    m[1] = np.eye(n)
    m[2] = bd & (t[:, None] < t[None, :])
    m[3] = bd & (t[:, None] <= t[None, :])
    m[4] = bd & (t[:, None] > t[None, :])
    m[5] = bd & (t[:, None] >= t[None, :])
    tt = np.arange(RW_CHUNK)
    cum = np.stack([tt[None, :] <= tt[:, None], tt[None, :] >= tt[:, None]]).astype(np.float32)
    ch = np.arange(RW_W) // RW_HEAD
    ones = (ch[:, None] == ch[None, :]).astype(np.float32)
    return m, cum, ones


def _rw_call(prw, conv_w, w0, wup_pad, a0, aup_pad, gup, kk, ka, rk, L):
    B, S, _ = prw.shape
    NC = S // RW_CHUNK
    nL = L // RW_CHUNK
    n8 = S // 8
    m, cum, ones = _rw_masks()

    def cr_of(i):
        return jnp.where(i < nL, nL - 1 - i, NC - 1 - i + nL)

    full = lambda shape: pl.BlockSpec(shape, lambda b, i: (0,) * len(shape))
    in_specs = [
        pl.BlockSpec((None, RW_CHUNK, RW_IN), lambda b, i: (b, i, 0)),
        pl.BlockSpec((None, 8, RW_IN), lambda b, i: (b, jnp.maximum(i * 8 - 1, 0), 0)),
        pl.BlockSpec((None, 8, RW_IN), lambda b, i: (b, jnp.minimum(i * 8 + 8, n8 - 1), 0)),
        pl.BlockSpec((None, RW_CHUNK, RW_IN), lambda b, i: (b, cr_of(i), 0)),
        pl.BlockSpec((None, 8, RW_IN), lambda b, i: (b, jnp.maximum(cr_of(i) * 8 - 1, 0), 0)),
        pl.BlockSpec((None, 8, RW_IN), lambda b, i: (b, jnp.minimum(cr_of(i) * 8 + 8, n8 - 1), 0)),
        full((3, RW_IN)), full((2, 1, RW_W)), full((2, 128, RW_W)), full((2, 1, RW_W)), full((2, 128, RW_W)),
        full((128, RW_W)), full((1, RW_W)), full((1, RW_W)), full((1, RW_W)),
        full((RW_W, RW_W)), full((2, RW_CHUNK, RW_CHUNK)), full((6, 4 * RW_CHUNK, 4 * RW_CHUNK)),
    ]
    out_specs = [
        pl.BlockSpec((None, RW_CHUNK, RW_W), lambda b, i: (b, i, 0)),
        pl.BlockSpec((None, RW_CHUNK, RW_W), lambda b, i: (b, cr_of(i), 0)),
        pl.BlockSpec((None, RW_CHUNK, 2 * RW_W), lambda b, i: (b, i, 0)),
    ]
    return pl.pallas_call(
        functools.partial(_rw_kernel, nL=nL, NC=NC),
        grid=(B, NC),
        in_specs=in_specs,
        out_specs=out_specs,
        out_shape=[jax.ShapeDtypeStruct((B, S, RW_W), F32),
                   jax.ShapeDtypeStruct((B, S, RW_W), F32),
                   jax.ShapeDtypeStruct((B, S, 2 * RW_W), F32)],
        scratch_shapes=[pltpu.VMEM((2, RW_W // RW_GRP, RW_GRP, RW_GRP), F32)],
        compiler_params=_params(("parallel", "arbitrary")),
        name="rwkv7_chunked",
    )(prw, prw, prw, prw, prw, prw, conv_w, w0, wup_pad, a0, aup_pad, gup, kk, ka, rk,
      jnp.asarray(ones, MM), jnp.asarray(cum, MM), jnp.asarray(m))


def _res_ln(h_ref, mod_ref, row, j, y, lng_ref, lnb_ref, o_ref):
    z = ALPHA * h_ref[...] + _mod_row(mod_ref, row, j) * y
    o_ref[...] = _layer_norm(z, lng_ref[...], lnb_ref[...])


def _even_out_kernel(h_ref, mod_ref, ona_ref, yf_ref, yr_ref, bg_ref, ones_ref, gng_ref, gnb_ref, wo_ref,
                     lng_ref, lnb_ref, o_ref, *, nB):
    b = pl.program_id(0)
    i = pl.program_id(1)
    row = jnp.where(i == 0, nB, b)
    y = yf_ref[...] + yr_ref[...]
    mu = _dot_split_lhs(y, ones_ref[...]) * (1.0 / RW_HEAD)
    yc = y - mu
    var = _dot_split_lhs(yc * yc, ones_ref[...]) * (1.0 / RW_HEAD)
    yn = yc * lax.rsqrt(var + RW_GN_EPS) * gng_ref[...] + gnb_ref[...]
    orw = (yn + bg_ref[:, 0:RW_W]) * bg_ref[:, RW_W:]
    yy = (jnp.dot(ona_ref[...], wo_ref[0:NA_W, :], preferred_element_type=F32)
          + jnp.dot(orw.astype(MM), wo_ref[NA_W:, :], preferred_element_type=F32))
    _res_ln(h_ref, mod_ref, row, 2, yy, lng_ref, lnb_ref, o_ref)


def _even_out_call(h, mod, ona, yf, yr, bg, gng, gnb, wo, lng, lnb, nB):
    B, S, _ = h.shape
    R = mod.shape[0]
    _, _, ones = _rw_masks()
    tile = lambda w: pl.BlockSpec((None, TM, w), lambda b, i: (b, i, 0))
    full = lambda shape: pl.BlockSpec(shape, lambda b, i: (0,) * len(shape))
    return pl.pallas_call(
        functools.partial(_even_out_kernel, nB=nB),
        grid=(B, S // TM),
        in_specs=[tile(D), full((R, 6 * D)), tile(NA_W), tile(RW_W), tile(RW_W), tile(2 * RW_W),
                  full((RW_W, RW_W)), full((1, RW_W)), full((1, RW_W)), full((D, D)), full((1, D)), full((1, D))],
        out_specs=tile(D),
        out_shape=jax.ShapeDtypeStruct((B, S, D), F32),
        compiler_params=_params(("parallel", "arbitrary")),
        name="even_out_proj_ln",
    )(h, mod, ona, yf, yr, bg, jnp.asarray(ones, MM), gng, gnb, wo, lng, lnb)


def _odd_out_kernel(h_ref, mod_ref, oda_ref, wo_ref, lng_ref, lnb_ref, o_ref, *, nB):
    b = pl.program_id(0)
    i = pl.program_id(1)
    row = jnp.where(i == 0, nB, b)
    yy = jnp.dot(oda_ref[...], wo_ref[...], preferred_element_type=F32)
    _res_ln(h_ref, mod_ref, row, 2, yy, lng_ref, lnb_ref, o_ref)


def _odd_out_call(h, mod, oda, wo, lng, lnb, nB):
    B, S, _ = h.shape
    R = mod.shape[0]
    tile = lambda w: pl.BlockSpec((None, TM, w), lambda b, i: (b, i, 0))
    full = lambda shape: pl.BlockSpec(shape, lambda b, i: (0,) * len(shape))
    return pl.pallas_call(
        functools.partial(_odd_out_kernel, nB=nB),
        grid=(B, S // TM),
        in_specs=[tile(D), full((R, 6 * D)), tile(DA_W), full((DA_W, D)), full((1, D)), full((1, D))],
        out_specs=tile(D),
        out_shape=jax.ShapeDtypeStruct((B, S, D), F32),
        compiler_params=_params(("parallel", "arbitrary")),
        name="odd_out_proj_ln",
    )(h, mod, oda, wo, lng, lnb)


def _odd_in_kernel(h_ref, mod_ref, w_ref, cos_ref, sin_ref, o_ref, *, nB):
    b = pl.program_id(0)
    i = pl.program_id(1)
    row = jnp.where(i == 0, nB, b)
    u = (h_ref[...] * (1.0 + _mod_row(mod_ref, row, 1)) + _mod_row(mod_ref, row, 0)).astype(MM)
    res = jnp.dot(u, w_ref[...], preferred_element_type=F32)
    cs = cos_ref[...]
    sn = sin_ref[...]
    lane = lax.broadcasted_iota(jnp.int32, cs.shape, 1)
    first = (lane % 32) < 16
    for j in range(2 * DA_W // 128):
        zj = res[:, j * 128:(j + 1) * 128]
        sw = jnp.where(first, pltpu.roll(zj, 112, axis=1), pltpu.roll(zj, 16, axis=1))
        rot = zj * cs + sw * sn
        if j < DA_W // 128:
            rot = rot * 0.125
        o_ref[:, j * 128:(j + 1) * 128] = rot.astype(o_ref.dtype)
    o_ref[:, 2 * DA_W:] = res[:, 2 * DA_W:].astype(o_ref.dtype)


def _odd_in_call(h, mod, w, cos_t, sin_t, nB):
    B, S, _ = h.shape
    R = mod.shape[0]
    return pl.pallas_call(
        functools.partial(_odd_in_kernel, nB=nB),
        grid=(B, S // TM),
        in_specs=[pl.BlockSpec((None, TM, D), lambda b, i: (b, i, 0)),
                  pl.BlockSpec((R, 6 * D), lambda b, i: (0, 0)),
                  pl.BlockSpec((D, 3 * DA_W), lambda b, i: (0, 0)),
                  pl.BlockSpec((TM, 128), lambda b, i: (i, 0)),
                  pl.BlockSpec((TM, 128), lambda b, i: (i, 0))],
        out_specs=pl.BlockSpec((None, TM, 3 * DA_W), lambda b, i: (b, i, 0)),
        out_shape=jax.ShapeDtypeStruct((B, S, 3 * DA_W), MM),
        compiler_params=_params(("parallel", "arbitrary")),
        name="odd_in_proj_rope",
    )(h, mod, w, cos_t, sin_t)


def _rope_tables(L, T):
    nf = 16
    inv = ROPE_THETA ** (-jnp.arange(nf, dtype=F32) / nf)
    t = jnp.arange(T)
    ang_r = (t // GRID_W).astype(F32)[:, None] * inv
    ang_c = (t % GRID_W).astype(F32)[:, None] * inv
    cos64 = jnp.concatenate([jnp.cos(ang_r), jnp.cos(ang_r), jnp.cos(ang_c), jnp.cos(ang_c)], -1)
    sin64 = jnp.concatenate([-jnp.sin(ang_r), jnp.sin(ang_r), -jnp.sin(ang_c), jnp.sin(ang_c)], -1)
    cos_t = jnp.concatenate([jnp.ones((L, 128), F32), jnp.tile(cos64, (1, 2))], 0)
    sin_t = jnp.concatenate([jnp.zeros((L, 128), F32), jnp.tile(sin64, (1, 2))], 0)
    return cos_t, sin_t


def _da_kernel(q_ref, k_ref, v_ref, lq1_ref, lk1_ref, lq2_ref, lk2_ref, sg_ref, o_ref, m_scr, l_scr, acc_scr, s_scr,
               *, L, n_lat, lam_init):
    i = pl.program_id(2)
    q = q_ref[...]
    lane = lax.broadcasted_iota(jnp.int32, q.shape, 1)
    lo = lane < 64
    zero = jnp.zeros_like(q)
    qq = jnp.concatenate([jnp.where(lo, q, zero), jnp.where(lo, zero, q)], axis=0)
    align = math.gcd(L, DA_TK)

    def scores(j):
        start = pl.multiple_of(L + j * DA_TK, align)
        return _dot_nt(qq, k_ref[pl.ds(start, DA_TK), :])

    def softmax_pv(s, vb, first):
        m_cur = jnp.max(s, axis=-1, keepdims=True)
        if first:
            m_new = jnp.broadcast_to(m_cur, m_scr.shape)
            p = jnp.exp(s - m_cur)
            l_scr[...] = jnp.broadcast_to(jnp.sum(p, axis=-1, keepdims=True), l_scr.shape)
            acc_scr[...] = _dot(p, vb)
        else:
            m_old = m_scr[...]
            m_new = jnp.maximum(m_old, m_cur)
            alpha = jnp.exp(m_old - m_new)
            p = jnp.exp(s - pltpu.repeat(m_new, s.shape[1] // 128, axis=1))
            l_scr[...] = alpha * l_scr[...] + jnp.sum(p, axis=-1, keepdims=True)
            acc_scr[...] = alpha * acc_scr[...] + _dot(p, vb)
        m_scr[...] = m_new

    s_ctx = _dot_nt(qq, k_ref[0:L, :])

    @pl.when(i > 0)
    def _():
        s_scr[0] = scores(0)

    softmax_pv(s_ctx, v_ref[0:L, :], True)

    @pl.when(i > 0)
    def _():
        def body(jj, carry):
            for si in range(2):
                j = 2 * jj + si
                s = s_scr[si]
                s_scr[1 - si] = scores(jnp.minimum(j + 1, n_lat - 1))
                start = pl.multiple_of(L + j * DA_TK, align)
                softmax_pv(s, v_ref[pl.ds(start, DA_TK), :], False)
            return carry
        lax.fori_loop(0, n_lat // 2, body, 0)

    lam = (jnp.exp(jnp.sum(lq1_ref[...] * lk1_ref[...], axis=-1, keepdims=True))
           - jnp.exp(jnp.sum(lq2_ref[...] * lk2_ref[...], axis=-1, keepdims=True)) + lam_init)
    o_all = acc_scr[...] * (1.0 / l_scr[...])
    o = o_all[0:DA_TQ] - lam * o_all[DA_TQ:]
    o = o * lax.rsqrt(jnp.mean(o * o, axis=-1, keepdims=True) + 1e-5) * sg_ref[...] * (1.0 - lam_init)
    o_ref[...] = o.astype(o_ref.dtype)


def _da_call(qkv, lq1, lk1, lq2, lk2, sg, L, lam_init):
    B, S, _ = qkv.shape
    n_lat = (S - L) // DA_TK
    assert n_lat % 2 == 0
    full = lambda shape: pl.BlockSpec(shape, lambda b, h, i: (0,) * len(shape))
    return pl.pallas_call(
        functools.partial(_da_kernel, L=L, n_lat=n_lat, lam_init=lam_init),
        grid=(B, DA_HEADS, S // DA_TQ),
        in_specs=[pl.BlockSpec((None, DA_TQ, 128), lambda b, h, i: (b, i, h)),
                  pl.BlockSpec((None, S, 128), lambda b, h, i: (b, 0, DA_HEADS + h)),
                  pl.BlockSpec((None, S, 128), lambda b, h, i: (b, 0, 2 * DA_HEADS + h)),
                  full((1, 64)), full((1, 64)), full((1, 64)), full((1, 64)), full((1, 128))],
        out_specs=pl.BlockSpec((None, DA_TQ, 128), lambda b, h, i: (b, i, h)),
        out_shape=jax.ShapeDtypeStruct((B, S, DA_W), MM),
        scratch_shapes=[pltpu.VMEM((2 * DA_TQ, 128), F32), pltpu.VMEM((2 * DA_TQ, 128), F32),
                        pltpu.VMEM((2 * DA_TQ, 128), F32), pltpu.VMEM((2, 2 * DA_TQ, DA_TK), F32)],
        compiler_params=_params(("parallel", "parallel", "arbitrary")),
        name="diff_attention",
    )(qkv, qkv, qkv, lq1, lk1, lq2, lk2, sg)


def _route(logits_t, rb):
    s = _sigmoid(logits_t)
    sel = s + rb
    x = [sel[e:e + 1, :] for e in range(N_EXPERTS)]
    tg = []
    for g in range(N_GROUPS):
        best = None
        for a in range(EPG):
            for b in range(a + 1, EPG):
                pair = x[EPG * g + a] + x[EPG * g + b]
                best = pair if best is None else jnp.maximum(best, pair)
        tg.append(best)
    rows = []
    gsel = []
    for g in range(N_GROUPS):
        chosen = None
        for g2 in range(N_GROUPS):
            if g2 == g:
                continue
            c = (tg[g] > tg[g2]) if g2 < g else (tg[g] >= tg[g2])
            chosen = c if chosen is None else jnp.logical_and(chosen, c)
        for a in range(EPG):
            rank = jnp.zeros_like(x[0])
            for b in range(EPG):
                if b == a:
                    continue
                ahead = (x[EPG * g + b] >= x[EPG * g + a]) if b < a else (x[EPG * g + b] > x[EPG * g + a])
                rank = rank + jnp.where(ahead, 1.0, 0.0)
            picked = jnp.logical_and(chosen, rank < 1.5)
            rows.append(jnp.where(picked, s[EPG * g + a:EPG * g + a + 1, :], 0.0))
        gsel.append(jnp.where(chosen, 1.0, 0.0))
    comb = jnp.concatenate(rows, axis=0)
    return comb * (1.0 / jnp.sum(comb, axis=0, keepdims=True)), jnp.concatenate(gsel, axis=0)


def _router_kernel(h_ref, mod_ref, rwt_ref, rb_ref, u_ref, rt_ref, *, nB, tile0):
    b = pl.program_id(0)
    i = pl.program_id(1)
    row = jnp.where(i + tile0 == 0, nB, b)
    u = h_ref[...] * (1.0 + _mod_row(mod_ref, row, 4)) + _mod_row(mod_ref, row, 3)
    u_ref[...] = u.astype(MM)
    logits_t = lax.dot_general(rwt_ref[...], u, (((1,), (1,)), ((), ())), precision=HIGHEST,
                               preferred_element_type=F32)
    comb_t, gsel_t = _route(logits_t, rb_ref[...])
    pad = jnp.zeros((MOE_RT_ROWS - N_EXPERTS - N_GROUPS, TM), F32)
    rt_ref[...] = jnp.concatenate([comb_t, gsel_t, pad], axis=0)


def _router_call(h, mod, rwt, rb, nB, tile0):
    B, S, _ = h.shape
    R = mod.shape[0]
    nt = S // TM - tile0
    full = lambda shape: pl.BlockSpec(shape, lambda b, i: (0,) * len(shape))
    return pl.pallas_call(
        functools.partial(_router_kernel, nB=nB, tile0=tile0),
        grid=(B, nt),
        in_specs=[pl.BlockSpec((None, TM, D), lambda b, i: (b, i + tile0, 0)),
                  full((R, 6 * D)), full((N_EXPERTS, D)), full((N_EXPERTS, 1))],
        out_specs=[pl.BlockSpec((None, TM, D), lambda b, i: (b, i, 0)),
                   pl.BlockSpec((MOE_RT_ROWS, TM), lambda b, i: (0, b * nt + i))],
        out_shape=[jax.ShapeDtypeStruct((B, nt * TM, D), MM),
                   jax.ShapeDtypeStruct((MOE_RT_ROWS, B * nt * TM), F32)],
        compiler_params=_params(("parallel", "arbitrary")),
        name="moe_router",
    )(h, mod, rwt, rb)


def _expert_kernel(cnt_ref, u_ref, rt_ref, wg_ref, wu_ref, wd_ref, f_ref, xs_scr, cs_scr, fs_scr, pos_scr,
                   *, TE, nt):
    t = pl.program_id(0)
    g = pl.program_id(1)
    n = [cnt_ref[gg * nt + t] for gg in range(N_GROUPS)]
    off = [jnp.int32(0)]
    for gg in range(N_GROUPS - 1):
        off.append(off[-1] + n[gg])

    @pl.when(g == 0)
    def _():
        rt = rt_ref[...]
        gs = rt[N_EXPERTS:N_EXPERTS + 8, :]
        r_i = lax.broadcasted_iota(jnp.int32, (TE, TE), 0)
        c_i = lax.broadcasted_iota(jnp.int32, (TE, TE), 1)
        before = jnp.where(r_i < c_i, 1.0, 0.0).astype(MM)
        cnt_before = jnp.dot(gs.astype(MM), before, preferred_element_type=F32)
        pos = jnp.zeros((1, TE), F32)
        for gg in range(N_GROUPS):
            pos = pos + gs[gg:gg + 1, :] * (off[gg].astype(F32) + cnt_before[gg:gg + 1, :])
        perm = jnp.where(r_i.astype(F32) == pos, 1.0, 0.0).astype(MM)
        xs_scr[...] = jnp.dot(perm, u_ref[...], preferred_element_type=F32).astype(MM)
        rt_pad = jnp.concatenate([rt, jnp.zeros((128 - MOE_RT_ROWS, TE), F32)], axis=0)
        hi, lo = _split(rt_pad)
        cs_scr[...] = _dot_nt(perm, hi) + _dot_nt(perm, lo)
        pos_scr[...] = jnp.broadcast_to(pos, (128, TE)).T
        fs_scr[...] = jnp.zeros_like(fs_scr)

    lo_g = jnp.int32(0)
    n_g = jnp.int32(0)
    for gg in range(N_GROUPS):
        lo_g = jnp.where(g == gg, off[gg], lo_g)
        n_g = jnp.where(g == gg, n[gg], n_g)
    hi_g = lo_g + n_g
    lane = lax.broadcasted_iota(jnp.int32, (MOE_BLK, 128), 1)
    for rb in range(TE // MOE_BLK):
        @pl.when(jnp.logical_and(lo_g < MOE_BLK * (rb + 1), hi_g > MOE_BLK * rb))
        def _(rb=rb):
            rows = slice(rb * MOE_BLK, (rb + 1) * MOE_BLK)
            xb = xs_scr[rows, :]
            hg = jnp.dot(xb, wg_ref[...], preferred_element_type=F32)
            hu = jnp.dot(xb, wu_ref[...], preferred_element_type=F32)
            act = hg * _sigmoid(hg) * hu
            c = cs_scr[rows, :]
            parts = []
            for e in range(EPG):
                col = jnp.sum(jnp.where(lane == EPG * g + e, c, 0.0), axis=-1, keepdims=True)
                parts.append((act[:, e * D_EXPERT:(e + 1) * D_EXPERT] * col).astype(MM))
            he = jnp.concatenate(parts, axis=1)
            fs_scr[rows, :] += jnp.dot(he, wd_ref[...], preferred_element_type=F32)

    @pl.when(g == N_GROUPS - 1)
    def _():
        c_i = lax.broadcasted_iota(jnp.int32, (TE, TE), 1)
        unperm = jnp.where(c_i.astype(F32) == pltpu.repeat(pos_scr[...], TE // 128, axis=1), 1.0, 0.0).astype(MM)
        f_ref[...] = jnp.dot(unperm, fs_scr[...].astype(MM), preferred_element_type=F32).astype(f_ref.dtype)


def _expert_call(u, rt, wg, wu, wd):
    N = u.shape[0]
    TE = MOE_TE if N % MOE_TE == 0 else TM
    nt = N // TE
    cnt = jnp.sum(rt[N_EXPERTS:N_EXPERTS + N_GROUPS].reshape(N_GROUPS, nt, TE), axis=-1)
    cnt = cnt.astype(jnp.int32).reshape(N_GROUPS * nt)
    gw = EPG * D_EXPERT
    grid_spec = pltpu.PrefetchScalarGridSpec(
        num_scalar_prefetch=1,
        grid=(nt, N_GROUPS),
        in_specs=[pl.BlockSpec((TE, D), lambda t, g, c: (t, 0)),
                  pl.BlockSpec((MOE_RT_ROWS, TE), lambda t, g, c: (0, t)),
                  pl.BlockSpec((None, D, gw), lambda t, g, c: (g, 0, 0)),
                  pl.BlockSpec((None, D, gw), lambda t, g, c: (g, 0, 0)),
                  pl.BlockSpec((None, gw, D), lambda t, g, c: (g, 0, 0))],
        out_specs=pl.BlockSpec((TE, D), lambda t, g, c: (t, 0)),
        scratch_shapes=[pltpu.VMEM((TE, D), MM), pltpu.VMEM((TE, 128), F32), pltpu.VMEM((TE, D), F32),
                        pltpu.VMEM((TE, 128), F32)])
    return pl.pallas_call(
        functools.partial(_expert_kernel, TE=TE, nt=nt),
        grid_spec=grid_spec,
        out_shape=jax.ShapeDtypeStruct((N, D), MM),
        compiler_params=_params(("parallel", "arbitrary")),
        name="moe_experts",
    )(cnt, u, rt, wg, wu, wd)


def _moe_ln_kernel(h_ref, mod_ref, f_ref, lng_ref, lnb_ref, o_ref, *, nB, tile0):
    b = pl.program_id(0)
    i = pl.program_id(1)
    row = jnp.where(i + tile0 == 0, nB, b)
    _res_ln(h_ref, mod_ref, row, 5, f_ref[...].astype(F32), lng_ref, lnb_ref, o_ref)


def _moe_ln_call(h, mod, f, lng, lnb, nB, tile0):
    B, S, _ = h.shape
    R = mod.shape[0]
    nt = S // TM - tile0
    full = lambda shape: pl.BlockSpec(shape, lambda b, i: (0,) * len(shape))
    return pl.pallas_call(
        functools.partial(_moe_ln_kernel, nB=nB, tile0=tile0),
        grid=(B, nt),
        in_specs=[pl.BlockSpec((None, TM, D), lambda b, i: (b, i + tile0, 0)),
                  full((R, 6 * D)),
                  pl.BlockSpec((None, TM, D), lambda b, i: (b, i, 0)),
                  full((1, D)), full((1, D))],
        out_specs=pl.BlockSpec((None, TM, D), lambda b, i: (b, i, 0)),
        out_shape=jax.ShapeDtypeStruct((B, nt * TM, D), F32),
        compiler_params=_params(("parallel", "arbitrary")),
        name="moe_res_ln",
    )(h, mod, f, lng, lnb)


def _group_weights(w, axis):
    w = w.astype(MM).reshape(N_GROUPS, EPG, w.shape[1], w.shape[2])
    if axis == 2:
        return jnp.transpose(w, (0, 2, 1, 3)).reshape(N_GROUPS, w.shape[2], EPG * w.shape[3])
    return w.reshape(N_GROUPS, EPG * w.shape[2], w.shape[3])


def _moe_call(h, mod, rwt, rb, wg, wu, wd, lng, lnb, nB, latent_only, L):
    B = h.shape[0]
    tile0 = L // TM if latent_only else 0
    u, rt = _router_call(h, mod, rwt, rb, nB, tile0)
    f = _expert_call(u.reshape(-1, D), rt, _group_weights(wg, 2), _group_weights(wu, 2), _group_weights(wd, 1))
    return _moe_ln_call(h, mod, f.reshape(B, -1, D), lng, lnb, nB, tile0)


def kernel(x, c, ctx, c_ctx, ada_w, ada_b, ln_g, ln_b, even_w_in, even_w_out, shift_w, na_rpb, rw_w0, rw_w_up, rw_a0, rw_a_up, rw_g_up, rw_k_k, rw_k_a, rw_r_k, rw_gn_g, rw_gn_b, odd_w_in, odd_w_out, da_lq1, da_lk1, da_lq2, da_lk2, da_subln_g, router_w, router_b, exp_w_gate, exp_w_up, exp_w_down):
    B, T, _ = x.shape
    L = ctx.shape[1]
    assert L == TM and T % DA_TK == 0 and (T // GRID_W) >= 12
    rows = T // GRID_W
    R = -(-(B + 1) // 8) * 8

    cv = jnp.zeros((R, D), F32).at[:B].set(c).at[B].set(c_ctx)
    mods = _ada_call(cv, ada_w, ada_b)

    cos_t, sin_t = _rope_tables(L, T)
    rwt = router_w.T
    rb = router_b.reshape(N_EXPERTS, 1)
    zpad = jnp.zeros((2, 64, RW_W), F32)

    h = jnp.concatenate([ctx, x], axis=1)
    for l in range(DEPTH):
        mod = mods[l]
        i = l // 2
        lng = ln_g[l].reshape(2, 1, D)
        lnb = ln_b[l].reshape(2, 1, D)
        if l % 2 == 0:
            w_in = even_w_in[i].astype(MM)
            qkv, prw = _even_in_call(h, mod, w_in[:, :3 * NA_W], w_in[:, 3 * NA_W:], B)
            ona = _na_call(qkv, _na_bias(na_rpb[i], rows), L)
            yf, yr, bg = _rw_call(
                prw, shift_w[i], rw_w0[i].reshape(2, 1, RW_W),
                jnp.concatenate([rw_w_up[i], zpad], axis=1).astype(MM), rw_a0[i].reshape(2, 1, RW_W),
                jnp.concatenate([zpad, rw_a_up[i]], axis=1).astype(MM), rw_g_up[i].astype(MM),
                rw_k_k[i].reshape(1, RW_W), rw_k_a[i].reshape(1, RW_W), rw_r_k[i].reshape(1, RW_W), L)
            h = _even_out_call(h, mod, ona, yf, yr, bg, rw_gn_g[i].reshape(1, RW_W), rw_gn_b[i].reshape(1, RW_W),
                               even_w_out[i].astype(MM), lng[0], lnb[0], B)
        else:
            lam_init = 0.8 - 0.6 * math.exp(-0.3 * l)
            qkv = _odd_in_call(h, mod, odd_w_in[i].astype(MM), cos_t, sin_t, B)
            oda = _da_call(qkv, da_lq1[i].reshape(1, 64), da_lk1[i].reshape(1, 64), da_lq2[i].reshape(1, 64),
                           da_lk2[i].reshape(1, 64), da_subln_g[i].reshape(1, 128), L, lam_init)
            h = _odd_out_call(h, mod, oda, odd_w_out[i].astype(MM), lng[0], lnb[0], B)
        h = _moe_call(h, mod, rwt, rb, exp_w_gate[l].astype(MM), exp_w_up[l].astype(MM), exp_w_down[l].astype(MM),
                      lng[1], lnb[1], B, l == DEPTH - 1, L)
    return h
```

```python
import functools
import math

import numpy as np
import jax
import jax.numpy as jnp
from jax import lax
from jax.experimental import pallas as pl
from jax.experimental.pallas import tpu as pltpu

F32 = jnp.float32
MM = jnp.bfloat16
HIGHEST = lax.Precision.HIGHEST

D = 1024
DEPTH = 4
GRID_W = 64
NA_HEADS = 8
NA_W = 512
NA_WIN_R = 8
NA_WIN_C = 16
NA_QB = 128
NA_KROWS = 9
NA_NLOC = NA_KROWS * GRID_W
RW_W = 512
RW_HEAD = 64
RW_IN = 1792
RW_GN_EPS = 64e-5
RW_CHUNK = 64
RW_GRP = 256
DA_HEADS = 8
DA_W = 1024
ROPE_THETA = 10000.0
N_EXPERTS = 16
N_GROUPS = 4
EPG = 4
D_EXPERT = 512
MOE_RT_ROWS = 32
MOE_TE = 512
MOE_BLK = 128
ALPHA = (2 * DEPTH) ** 0.25
LN_EPS = 1e-5
TM = 256
DA_TQ = 256
DA_TK = (768, 512, 256)
DA_SUB = 10
DA_QSCALE = 0.125 * math.log2(math.e)
NEG = -1e30
VMEM_LIMIT = 56 * 1024 * 1024


def _dot(a, b):
    return jnp.dot(a.astype(MM), b.astype(MM), preferred_element_type=F32)


def _dot_nt(a, b):
    return lax.dot_general(a.astype(MM), b.astype(MM), (((1,), (1,)), ((), ())), preferred_element_type=F32)


def _split(x):
    hi = x.astype(MM)
    lo = (x - hi.astype(F32)).astype(MM)
    return hi, lo


def _dot_split_lhs(x, w):
    hi, lo = _split(x)
    return jnp.dot(hi, w, preferred_element_type=F32) + jnp.dot(lo, w, preferred_element_type=F32)


def _dot_split_rhs(w, x):
    hi, lo = _split(x)
    return jnp.dot(w, hi, preferred_element_type=F32) + jnp.dot(w, lo, preferred_element_type=F32)


def _sigmoid(x):
    return 1.0 / (1.0 + jnp.exp(-x))


def _params(sem):
    return pltpu.CompilerParams(dimension_semantics=sem, vmem_limit_bytes=VMEM_LIMIT)


def _mod_row(mod_ref, row, j):
    return mod_ref[pl.ds(row, 1), j * D:(j + 1) * D]


def _layer_norm(z, g, b):
    mu = jnp.mean(z, axis=-1, keepdims=True)
    zc = z - mu
    var = jnp.mean(zc * zc, axis=-1, keepdims=True)
    return zc * lax.rsqrt(var + LN_EPS) * g + b


def _ada_kernel(cv_ref, w_ref, b_ref, o_ref):
    x = cv_ref[...]
    x = x * _sigmoid(x)
    o_ref[...] = jnp.dot(x, w_ref[...], precision=HIGHEST, preferred_element_type=F32) + b_ref[...]


def _ada_call(cv, ada_w, ada_b):
    R = cv.shape[0]
    tn = 1536
    return pl.pallas_call(
        _ada_kernel,
        grid=(DEPTH, 6 * D // tn),
        in_specs=[pl.BlockSpec((R, D), lambda l, j: (0, 0)),
                  pl.BlockSpec((None, D, tn), lambda l, j: (l, 0, j)),
                  pl.BlockSpec((None, 1, tn), lambda l, j: (l, 0, j))],
        out_specs=pl.BlockSpec((None, R, tn), lambda l, j: (l, 0, j)),
        out_shape=jax.ShapeDtypeStruct((DEPTH, R, 6 * D), F32),
        compiler_params=_params(("arbitrary", "arbitrary")),
        name="ada_mod",
    )(cv, ada_w, ada_b.reshape(DEPTH, 1, 6 * D))


def _even_in_kernel(h_ref, mod_ref, wna_ref, wrw_ref, qkv_ref, prw_ref, *, nB):
    b = pl.program_id(0)
    i = pl.program_id(1)
    row = jnp.where(i == 0, nB, b)
    u = (h_ref[...] * (1.0 + _mod_row(mod_ref, row, 1)) + _mod_row(mod_ref, row, 0)).astype(MM)
    res = jnp.dot(u, wna_ref[...], preferred_element_type=F32)
    qkv_ref[:, 0:NA_W] = (res[:, 0:NA_W] * 0.125).astype(qkv_ref.dtype)
    qkv_ref[:, NA_W:] = res[:, NA_W:].astype(qkv_ref.dtype)
    prw_ref[...] = jnp.dot(u, wrw_ref[...], preferred_element_type=F32)


def _even_in_call(h, mod, w_na, w_rw, nB):
    B, S, _ = h.shape
    R = mod.shape[0]
    return pl.pallas_call(
        functools.partial(_even_in_kernel, nB=nB),
        grid=(B, S // TM),
        in_specs=[pl.BlockSpec((None, TM, D), lambda b, i: (b, i, 0)),
                  pl.BlockSpec((R, 6 * D), lambda b, i: (0, 0)),
                  pl.BlockSpec((D, 3 * NA_W), lambda b, i: (0, 0)),
                  pl.BlockSpec((D, RW_IN), lambda b, i: (0, 0))],
        out_specs=[pl.BlockSpec((None, TM, 3 * NA_W), lambda b, i: (b, i, 0)),
                   pl.BlockSpec((None, TM, RW_IN), lambda b, i: (b, i, 0))],
        out_shape=[jax.ShapeDtypeStruct((B, S, 3 * NA_W), MM),
                   jax.ShapeDtypeStruct((B, S, RW_IN), F32)],
        compiler_params=_params(("parallel", "arbitrary")),
        name="even_in_proj",
    )(h, mod, w_na, w_rw)


def _na_bias(rpb, rows):
    cq = np.arange(GRID_W)[:, None]
    ck = np.arange(GRID_W)[None, :]
    cs = np.clip(cq - NA_WIN_C // 2, 0, GRID_W - NA_WIN_C)
    ok_c = (ck >= cs) & (ck < cs + NA_WIN_C)
    dc = np.clip(ck - cq + NA_WIN_C - 1, 0, 2 * NA_WIN_C - 2)
    onehot = np.zeros((2 * NA_WIN_C - 1, GRID_W * GRID_W), np.float32)
    onehot[dc.ravel(), np.arange(GRID_W * GRID_W)] = 1.0
    rc = jnp.einsum('hrd,dn->hrn', rpb, jnp.asarray(onehot), precision=HIGHEST)
    rc = rc.reshape(rpb.shape[0], 2 * NA_WIN_R - 1, GRID_W, GRID_W)
    cases = []
    for r0 in (0, 2, 4, rows - 4, rows - 2):
        ws = int(np.clip(r0 - NA_WIN_R // 2, 0, rows - NA_KROWS))
        qrows = []
        for qi in range(NA_QB // GRID_W):
            rq = r0 + qi
            rs = int(np.clip(rq - NA_WIN_R // 2, 0, rows - NA_WIN_R))
            pieces = []
            for j in range(NA_KROWS):
                rk = ws + j
                if rs <= rk < rs + NA_WIN_R:
                    pieces.append(jnp.where(ok_c[None], rc[:, rk - rq + NA_WIN_R - 1], NEG))
                else:
                    pieces.append(jnp.full((rpb.shape[0], GRID_W, GRID_W), NEG, F32))
            qrows.append(jnp.concatenate(pieces, axis=-1))
        cases.append(jnp.concatenate(qrows, axis=-2))
    return jnp.stack(cases, axis=0)


def _na_kernel(q_ref, k_ref, v_ref, bias_ref, o_ref, *, L, rows):
    i = pl.program_id(2)
    nq = L // NA_QB
    lane = lax.broadcasted_iota(jnp.int32, (NA_QB, 128), 1)
    lo = lane < 64
    q = q_ref[...]
    zero = jnp.zeros_like(q)
    kc = k_ref[0:L, :]
    vc = v_ref[0:L, :]

    @pl.when(i < nq)
    def _():
        outs = []
        for hh in range(2):
            qm = jnp.where(lo if hh == 0 else jnp.logical_not(lo), q, zero)
            s = _dot_nt(qm, kc)
            m = jnp.max(s, axis=-1, keepdims=True)
            p = jnp.exp(s - m)
            l = jnp.sum(p, axis=-1, keepdims=True)
            outs.append(_dot(p, vc) * (1.0 / l))
        o_ref[...] = jnp.where(lo, outs[0], outs[1]).astype(o_ref.dtype)

    @pl.when(i >= nq)
    def _():
        r0 = 2 * (i - nq)
        ws = jnp.clip(r0 - NA_WIN_R // 2, 0, rows - NA_KROWS)
        start = pl.multiple_of(L + GRID_W * ws, GRID_W)
        kl = k_ref[pl.ds(start, NA_NLOC), :]
        vl = v_ref[pl.ds(start, NA_NLOC), :]
        outs = []
        for hh in range(2):
            qm = jnp.where(lo if hh == 0 else jnp.logical_not(lo), q, zero)
            s_loc = _dot_nt(qm, kl) + bias_ref[hh]
            s_ctx = _dot_nt(qm, kc)
            m = jnp.maximum(jnp.max(s_loc, axis=-1, keepdims=True), jnp.max(s_ctx, axis=-1, keepdims=True))
            p_loc = jnp.exp(s_loc - m)
            p_ctx = jnp.exp(s_ctx - m)
            l = jnp.sum(p_loc, axis=-1, keepdims=True) + jnp.sum(p_ctx, axis=-1, keepdims=True)
            outs.append((_dot(p_loc, vl) + _dot(p_ctx, vc)) * (1.0 / l))
        o_ref[...] = jnp.where(lo, outs[0], outs[1]).astype(o_ref.dtype)


def _na_call(qkv, bias, L):
    B, S, _ = qkv.shape
    rows = (S - L) // GRID_W
    nq = L // NA_QB
    nhp = NA_HEADS // 2

    def bias_idx(b, hp, i):
        r0 = 2 * (i - nq)
        c = jnp.where(r0 == 0, 0, jnp.where(r0 == 2, 1, jnp.where(r0 == rows - 4, 3, jnp.where(r0 == rows - 2, 4, 2))))
        return (jnp.where(i < nq, 2, c), hp, 0, 0)

    return pl.pallas_call(
        functools.partial(_na_kernel, L=L, rows=rows),
        grid=(B, nhp, S // NA_QB),
        in_specs=[pl.BlockSpec((None, NA_QB, 128), lambda b, hp, i: (b, i, hp)),
                  pl.BlockSpec((None, S, 128), lambda b, hp, i: (b, 0, nhp + hp)),
                  pl.BlockSpec((None, S, 128), lambda b, hp, i: (b, 0, 2 * nhp + hp)),
                  pl.BlockSpec((None, 2, NA_QB, NA_NLOC), bias_idx)],
        out_specs=pl.BlockSpec((None, NA_QB, 128), lambda b, hp, i: (b, i, hp)),
        out_shape=jax.ShapeDtypeStruct((B, S, NA_W), MM),
        compiler_params=_params(("parallel", "parallel", "arbitrary")),
        name="na_attention",
    )(qkv, qkv, qkv, bias)


def _rw_conv(p_ref, prev_ref, next_ref, c, conv_ref, nL, NC):
    p = p_ref[...]
    at_start = jnp.logical_or(c == 0, c == nL)
    at_end = jnp.logical_or(c == nL - 1, c == NC - 1)
    prow = jnp.where(at_start, 0.0, prev_ref[7:8, :])
    nrow = jnp.where(at_end, 0.0, next_ref[0:1, :])
    rid = lax.broadcasted_iota(jnp.int32, p.shape, 0)
    pm = jnp.where(rid == 0, prow, pltpu.roll(p, 1, axis=0))
    pp = jnp.where(rid == RW_CHUNK - 1, nrow, pltpu.roll(p, RW_CHUNK - 1, axis=0))
    return pm * conv_ref[0:1, :] + p * conv_ref[1:2, :] + pp * conv_ref[2:3, :]


def _rw_expand(z, bd):
    return jnp.where(bd, jnp.concatenate([z, z, z, z], axis=0), 0.0)


def _rw_direction(pc, d, first_row, w0_ref, wup_ref, a0_ref, aup_ref, kk_ref, ka_ref, ones_ref, cum_ref,
                  mask_ref, g_scr):
    r = pc[:, 0:RW_W]
    k = pc[:, RW_W:2 * RW_W]
    v = pc[:, 2 * RW_W:3 * RW_W]
    xwa = pc[:, 3 * RW_W:3 * RW_W + 128]
    z = w0_ref[d] + _dot(jnp.tanh(xwa), wup_ref[d])
    lw = -math.exp(-0.5) * _sigmoid(z)
    a = _sigmoid(a0_ref[d] + _dot(xwa, aup_ref[d]))
    kk = k * kk_ref[...]
    n2 = _dot_split_lhs(kk * kk, ones_ref[...])
    kk = kk * (1.0 / jnp.maximum(jnp.sqrt(n2), 1e-12))
    keff = k * (1.0 + (a - 1.0) * ka_ref[...])
    av = -kk
    bv = kk * a
    lc = _dot_split_rhs(cum_ref[d], lw)
    ltot = lc[RW_CHUNK - 1:RW_CHUNK, :] if first_row == 0 else lc[0:1, :]
    at = av * jnp.exp(lc - lw)
    rt = r * jnp.exp(lc)
    ginv = jnp.exp(-lc)
    bt = bv * ginv
    kt = keff * ginv
    grest = jnp.exp(ltot - lc)
    bh = bv * grest
    kh = keff * grest
    gc = jnp.exp(ltot)

    bd = mask_ref[0] > 0.5
    eye = mask_ref[1]
    ms = mask_ref[2 + 2 * d] > 0.5
    mi = mask_ref[3 + 2 * d] > 0.5
    ys = []
    for g in range(RW_W // RW_GRP):
        sl = slice(g * RW_GRP, (g + 1) * RW_GRP)
        e_at = _rw_expand(at[:, sl], bd).astype(MM)
        e_rt = _rw_expand(rt[:, sl], bd).astype(MM)
        e_bt = _rw_expand(bt[:, sl], bd).astype(MM)
        e_kt = _rw_expand(kt[:, sl], bd).astype(MM)
        e_bh = _rw_expand(bh[:, sl], bd).astype(MM)
        e_kh = _rw_expand(kh[:, sl], bd).astype(MM)
        e_v = _rw_expand(v[:, sl], bd)
        e_vt = e_v.T.astype(MM)
        ar = jnp.concatenate([e_at, e_rt], axis=0)
        xb = _dot_nt(e_bt, ar)
        xk = _dot_nt(e_kt, ar)
        xab = jnp.where(ms, xb[:, 0:RW_GRP], 0.0)
        xrb = jnp.where(mi, xb[:, RW_GRP:], 0.0)
        xak = jnp.where(ms, xk[:, 0:RW_GRP], 0.0)
        xrk = jnp.where(mi, xk[:, RW_GRP:], 0.0)
        tt = eye + xab
        xp = xab
        for _ in range(5):
            xp = _dot(xp, xp)
            tt = tt + _dot(tt, xp)
        gs = g_scr[d, g]
        gsb = gs.astype(MM)
        w1 = _dot_nt(gsb, e_at) + _dot(e_vt, xak)
        ut = _dot(w1, tt).astype(MM)
        yt = _dot_nt(gsb, e_rt) + _dot(ut, xrb) + _dot(e_vt, xrk)
        g_scr[d, g] = gs * gc[:, sl] + _dot(ut, e_bh) + _dot(e_vt, e_kh)
        ybd = yt.T
        ys.append(ybd[0:64] + ybd[64:128] + ybd[128:192] + ybd[192:256])
    return jnp.concatenate(ys, axis=1), keff, r, v


def _rw_kernel(pf_ref, pfp_ref, pfn_ref, pr_ref, prp_ref, prn_ref, conv_ref, w0_ref, wup_ref, a0_ref, aup_ref,
               gup_ref, kk_ref, ka_ref, rk_ref, ones_ref, cum_ref, mask_ref, yf_ref, yr_ref, bg_ref, g_scr,
               *, nL, NC):
    i = pl.program_id(1)

    @pl.when(i == 0)
    def _():
        g_scr[...] = jnp.zeros_like(g_scr)

    cf = i
    cr = jnp.where(i < nL, nL - 1 - i, NC - 1 - i + nL)
    args = (w0_ref, wup_ref, a0_ref, aup_ref, kk_ref, ka_ref, ones_ref, cum_ref, mask_ref, g_scr)

    pcf = _rw_conv(pf_ref, pfp_ref, pfn_ref, cf, conv_ref, nL, NC)
    y_f, keff_f, r_f, v_f = _rw_direction(pcf, 0, 0, *args)
    yf_ref[...] = y_f
    xwa = pcf[:, 3 * RW_W:3 * RW_W + 128]
    a_r = _sigmoid(a0_ref[1] + _dot(xwa, aup_ref[1]))
    keff_r = pcf[:, RW_W:2 * RW_W] * (1.0 + (a_r - 1.0) * ka_ref[...])
    bsum = _dot_split_lhs(r_f * rk_ref[...] * (keff_f + keff_r), ones_ref[...]) * v_f
    bg_ref[:, 0:RW_W] = bsum
    bg_ref[:, RW_W:] = _dot(_sigmoid(pcf[:, 3 * RW_W + 128:]), gup_ref[...])

    pcr = _rw_conv(pr_ref, prp_ref, prn_ref, cr, conv_ref, nL, NC)
    y_r, _, _, _ = _rw_direction(pcr, 1, RW_CHUNK - 1, *args)
    yr_ref[...] = y_r


def _rw_masks():
    n = 4 * RW_CHUNK
    idx = np.arange(n)
    hd = idx // RW_CHUNK
    t = idx % RW_CHUNK
    bd = hd[:, None] == hd[None, :]
    m = np.zeros((6, n, n), np.float32)
    m[0] = bd
    m[1] = np.eye(n)
    m[2] = bd & (t[:, None] < t[None, :])
    m[3] = bd & (t[:, None] <= t[None, :])
    m[4] = bd & (t[:, None] > t[None, :])
    m[5] = bd & (t[:, None] >= t[None, :])
    tt = np.arange(RW_CHUNK)
    cum = np.stack([tt[None, :] <= tt[:, None], tt[None, :] >= tt[:, None]]).astype(np.float32)
    ch = np.arange(RW_W) // RW_HEAD
    ones = (ch[:, None] == ch[None, :]).astype(np.float32)
    return m, cum, ones


def _rw_call(prw, conv_w, w0, wup_pad, a0, aup_pad, gup, kk, ka, rk, L):
    B, S, _ = prw.shape
    NC = S // RW_CHUNK
    nL = L // RW_CHUNK
    n8 = S // 8
    m, cum, ones = _rw_masks()

    def cr_of(i):
        return jnp.where(i < nL, nL - 1 - i, NC - 1 - i + nL)

    full = lambda shape: pl.BlockSpec(shape, lambda b, i: (0,) * len(shape))
    in_specs = [
        pl.BlockSpec((None, RW_CHUNK, RW_IN), lambda b, i: (b, i, 0)),
        pl.BlockSpec((None, 8, RW_IN), lambda b, i: (b, jnp.maximum(i * 8 - 1, 0), 0)),
        pl.BlockSpec((None, 8, RW_IN), lambda b, i: (b, jnp.minimum(i * 8 + 8, n8 - 1), 0)),
        pl.BlockSpec((None, RW_CHUNK, RW_IN), lambda b, i: (b, cr_of(i), 0)),
        pl.BlockSpec((None, 8, RW_IN), lambda b, i: (b, jnp.maximum(cr_of(i) * 8 - 1, 0), 0)),
        pl.BlockSpec((None, 8, RW_IN), lambda b, i: (b, jnp.minimum(cr_of(i) * 8 + 8, n8 - 1), 0)),
        full((3, RW_IN)), full((2, 1, RW_W)), full((2, 128, RW_W)), full((2, 1, RW_W)), full((2, 128, RW_W)),
        full((128, RW_W)), full((1, RW_W)), full((1, RW_W)), full((1, RW_W)),
        full((RW_W, RW_W)), full((2, RW_CHUNK, RW_CHUNK)), full((6, 4 * RW_CHUNK, 4 * RW_CHUNK)),
    ]
    out_specs = [
        pl.BlockSpec((None, RW_CHUNK, RW_W), lambda b, i: (b, i, 0)),
        pl.BlockSpec((None, RW_CHUNK, RW_W), lambda b, i: (b, cr_of(i), 0)),
        pl.BlockSpec((None, RW_CHUNK, 2 * RW_W), lambda b, i: (b, i, 0)),
    ]
    return pl.pallas_call(
        functools.partial(_rw_kernel, nL=nL, NC=NC),
        grid=(B, NC),
        in_specs=in_specs,
        out_specs=out_specs,
        out_shape=[jax.ShapeDtypeStruct((B, S, RW_W), F32),
                   jax.ShapeDtypeStruct((B, S, RW_W), F32),
                   jax.ShapeDtypeStruct((B, S, 2 * RW_W), F32)],
        scratch_shapes=[pltpu.VMEM((2, RW_W // RW_GRP, RW_GRP, RW_GRP), F32)],
        compiler_params=_params(("parallel", "arbitrary")),
        name="rwkv7_chunked",
    )(prw, prw, prw, prw, prw, prw, conv_w, w0, wup_pad, a0, aup_pad, gup, kk, ka, rk,
      jnp.asarray(ones, MM), jnp.asarray(cum, MM), jnp.asarray(m))


def _res_ln(h_ref, mod_ref, row, j, y, lng_ref, lnb_ref, o_ref):
    z = ALPHA * h_ref[...] + _mod_row(mod_ref, row, j) * y
    o_ref[...] = _layer_norm(z, lng_ref[...], lnb_ref[...])


def _even_out_kernel(h_ref, mod_ref, ona_ref, yf_ref, yr_ref, bg_ref, ones_ref, gng_ref, gnb_ref, wo_ref,
                     lng_ref, lnb_ref, o_ref, *, nB):
    b = pl.program_id(0)
    i = pl.program_id(1)
    row = jnp.where(i == 0, nB, b)
    y = yf_ref[...] + yr_ref[...]
    mu = _dot_split_lhs(y, ones_ref[...]) * (1.0 / RW_HEAD)
    yc = y - mu
    var = _dot_split_lhs(yc * yc, ones_ref[...]) * (1.0 / RW_HEAD)
    yn = yc * lax.rsqrt(var + RW_GN_EPS) * gng_ref[...] + gnb_ref[...]
    orw = (yn + bg_ref[:, 0:RW_W]) * bg_ref[:, RW_W:]
    yy = (jnp.dot(ona_ref[...], wo_ref[0:NA_W, :], preferred_element_type=F32)
          + jnp.dot(orw.astype(MM), wo_ref[NA_W:, :], preferred_element_type=F32))
    _res_ln(h_ref, mod_ref, row, 2, yy, lng_ref, lnb_ref, o_ref)


def _even_out_call(h, mod, ona, yf, yr, bg, gng, gnb, wo, lng, lnb, nB):
    B, S, _ = h.shape
    R = mod.shape[0]
    _, _, ones = _rw_masks()
    tile = lambda w: pl.BlockSpec((None, TM, w), lambda b, i: (b, i, 0))
    full = lambda shape: pl.BlockSpec(shape, lambda b, i: (0,) * len(shape))
    return pl.pallas_call(
        functools.partial(_even_out_kernel, nB=nB),
        grid=(B, S // TM),
        in_specs=[tile(D), full((R, 6 * D)), tile(NA_W), tile(RW_W), tile(RW_W), tile(2 * RW_W),
                  full((RW_W, RW_W)), full((1, RW_W)), full((1, RW_W)), full((D, D)), full((1, D)), full((1, D))],
        out_specs=tile(D),
        out_shape=jax.ShapeDtypeStruct((B, S, D), F32),
        compiler_params=_params(("parallel", "arbitrary")),
        name="even_out_proj_ln",
    )(h, mod, ona, yf, yr, bg, jnp.asarray(ones, MM), gng, gnb, wo, lng, lnb)


def _odd_out_kernel(h_ref, mod_ref, oda_ref, wo_ref, lng_ref, lnb_ref, o_ref, *, nB):
    b = pl.program_id(0)
    i = pl.program_id(1)
    row = jnp.where(i == 0, nB, b)
    yy = jnp.dot(oda_ref[...], wo_ref[...], preferred_element_type=F32)
    _res_ln(h_ref, mod_ref, row, 2, yy, lng_ref, lnb_ref, o_ref)


def _odd_out_call(h, mod, oda, wo, lng, lnb, nB):
    B, S, _ = h.shape
    R = mod.shape[0]
    tile = lambda w: pl.BlockSpec((None, TM, w), lambda b, i: (b, i, 0))
    full = lambda shape: pl.BlockSpec(shape, lambda b, i: (0,) * len(shape))
    return pl.pallas_call(
        functools.partial(_odd_out_kernel, nB=nB),
        grid=(B, S // TM),
        in_specs=[tile(D), full((R, 6 * D)), tile(DA_W), full((DA_W, D)), full((1, D)), full((1, D))],
        out_specs=tile(D),
        out_shape=jax.ShapeDtypeStruct((B, S, D), F32),
        compiler_params=_params(("parallel", "arbitrary")),
        name="odd_out_proj_ln",
    )(h, mod, oda, wo, lng, lnb)


def _odd_in_kernel(h_ref, mod_ref, w_ref, cos_ref, sin_ref, o_ref, *, nB):
    b = pl.program_id(0)
    i = pl.program_id(1)
    row = jnp.where(i == 0, nB, b)
    u = (h_ref[...] * (1.0 + _mod_row(mod_ref, row, 1)) + _mod_row(mod_ref, row, 0)).astype(MM)
    res = jnp.dot(u, w_ref[...], preferred_element_type=F32)
    cs = cos_ref[...]
    sn = sin_ref[...]
    lane = lax.broadcasted_iota(jnp.int32, cs.shape, 1)
    first = (lane % 32) < 16
    for j in range(2 * DA_W // 128):
        zj = res[:, j * 128:(j + 1) * 128]
        sw = jnp.where(first, pltpu.roll(zj, 112, axis=1), pltpu.roll(zj, 16, axis=1))
        rot = zj * cs + sw * sn
        if j < DA_W // 128:
            rot = rot * DA_QSCALE
        o_ref[:, j * 128:(j + 1) * 128] = rot.astype(o_ref.dtype)
    o_ref[:, 2 * DA_W:] = res[:, 2 * DA_W:].astype(o_ref.dtype)


def _odd_in_call(h, mod, w, cos_t, sin_t, nB):
    B, S, _ = h.shape
    R = mod.shape[0]
    return pl.pallas_call(
        functools.partial(_odd_in_kernel, nB=nB),
        grid=(B, S // TM),
        in_specs=[pl.BlockSpec((None, TM, D), lambda b, i: (b, i, 0)),
                  pl.BlockSpec((R, 6 * D), lambda b, i: (0, 0)),
                  pl.BlockSpec((D, 3 * DA_W), lambda b, i: (0, 0)),
                  pl.BlockSpec((TM, 128), lambda b, i: (i, 0)),
                  pl.BlockSpec((TM, 128), lambda b, i: (i, 0))],
        out_specs=pl.BlockSpec((None, TM, 3 * DA_W), lambda b, i: (b, i, 0)),
        out_shape=jax.ShapeDtypeStruct((B, S, 3 * DA_W), MM),
        compiler_params=_params(("parallel", "arbitrary")),
        name="odd_in_proj_rope",
    )(h, mod, w, cos_t, sin_t)


def _rope_tables(L, T):
    nf = 16
    inv = ROPE_THETA ** (-jnp.arange(nf, dtype=F32) / nf)
    t = jnp.arange(T)
    ang_r = (t // GRID_W).astype(F32)[:, None] * inv
    ang_c = (t % GRID_W).astype(F32)[:, None] * inv
    cos64 = jnp.concatenate([jnp.cos(ang_r), jnp.cos(ang_r), jnp.cos(ang_c), jnp.cos(ang_c)], -1)
    sin64 = jnp.concatenate([-jnp.sin(ang_r), jnp.sin(ang_r), -jnp.sin(ang_c), jnp.sin(ang_c)], -1)
    cos_t = jnp.concatenate([jnp.ones((L, 128), F32), jnp.tile(cos64, (1, 2))], 0)
    sin_t = jnp.concatenate([jnp.zeros((L, 128), F32), jnp.tile(sin64, (1, 2))], 0)
    return cos_t, sin_t


def _da_kernel(q_ref, k_ref, v_ref, lq1_ref, lk1_ref, lq2_ref, lk2_ref, sg_ref, o_ref, m_scr, acc_scr, s_scr, va_scr,
               *, L, tk, n_chunks, sub, lam_init):
    i = pl.program_id(2)
    n_rest = n_chunks - 1

    @pl.when(i == 0)
    def _():
        va_scr[:, 0:128] = v_ref[...]
        va_scr[:, 128:256] = jnp.ones((va_scr.shape[0], 128), va_scr.dtype)

    q = q_ref[...]
    lane = lax.broadcasted_iota(jnp.int32, q.shape, 1)
    lo = lane < 64
    zero = jnp.zeros_like(q)
    qq = jnp.concatenate([jnp.where(lo, q, zero), jnp.where(lo, zero, q)], axis=0)

    def chunk(ref, j):
        if isinstance(j, int):
            return ref[j * tk:(j + 1) * tk, :]
        return ref[pl.ds(pl.multiple_of(j * tk, tk), tk), :]

    def scores(j):
        return _dot_nt(qq, chunk(k_ref, j))

    def softmax_pv(s, vb, first):
        m_cur = jnp.max(s, axis=-1, keepdims=True)
        if first:
            m_new = jnp.broadcast_to(m_cur, m_scr.shape)
            acc_scr[...] = _dot(jnp.exp2(s - m_cur), vb)
        else:
            m_old = m_scr[...]
            m_new = jnp.maximum(m_old, m_cur)
            alpha = jnp.exp2(m_old - m_new)
            p = jnp.exp2(s - jnp.tile(m_new, (1, s.shape[1] // 128)))
            acc_scr[...] = jnp.tile(alpha, (1, 2)) * acc_scr[...] + _dot(p, vb)
        m_scr[...] = m_new

    @pl.when(i == 0)
    def _():
        softmax_pv(_dot_nt(qq, k_ref[0:L, :]), va_scr[0:L, :], True)

    @pl.when(i > 0)
    def _():
        s_first = scores(0)
        s_scr[0] = scores(1)
        softmax_pv(s_first, chunk(va_scr, 0), True)

        def body(jj, carry):
            for si in range(sub):
                j = 1 + si if sub == n_rest else 1 + sub * jj + si
                s = s_scr[si % 2]
                if sub < n_rest:
                    s_scr[1 - si % 2] = scores(jnp.minimum(j + 1, n_chunks - 1))
                elif j + 1 < n_chunks:
                    s_scr[1 - si % 2] = scores(j + 1)
                softmax_pv(s, chunk(va_scr, j), False)
            return carry
        if sub < n_rest:
            lax.fori_loop(0, n_rest // sub, body, 0)
        else:
            body(0, 0)

    lam = (jnp.exp(jnp.sum(lq1_ref[...] * lk1_ref[...], axis=-1, keepdims=True))
           - jnp.exp(jnp.sum(lq2_ref[...] * lk2_ref[...], axis=-1, keepdims=True)) + lam_init)
    o_all = acc_scr[:, 0:128] * (1.0 / acc_scr[:, 128:256])
    o = o_all[0:DA_TQ] - lam * o_all[DA_TQ:]
    o = o * lax.rsqrt(jnp.mean(o * o, axis=-1, keepdims=True) + 1e-5) * sg_ref[...] * (1.0 - lam_init)
    o_ref[...] = o.astype(o_ref.dtype)


def _da_call(qkv, lq1, lk1, lq2, lk2, sg, L, lam_init):
    B, S, _ = qkv.shape
    tk = next(t for t in DA_TK if S % t == 0 and t >= L and (S // t) % 2 == 1)
    n_chunks = S // tk
    sub = DA_SUB if (n_chunks - 1) % DA_SUB == 0 else 2
    full = lambda shape: pl.BlockSpec(shape, lambda b, h, i: (0,) * len(shape))
    return pl.pallas_call(
        functools.partial(_da_kernel, L=L, tk=tk, n_chunks=n_chunks, sub=sub, lam_init=lam_init),
        grid=(B, DA_HEADS, S // DA_TQ),
        in_specs=[pl.BlockSpec((None, DA_TQ, 128), lambda b, h, i: (b, i, h)),
                  pl.BlockSpec((None, S, 128), lambda b, h, i: (b, 0, DA_HEADS + h)),
                  pl.BlockSpec((None, S, 128), lambda b, h, i: (b, 0, 2 * DA_HEADS + h)),
                  full((1, 64)), full((1, 64)), full((1, 64)), full((1, 64)), full((1, 128))],
        out_specs=pl.BlockSpec((None, DA_TQ, 128), lambda b, h, i: (b, i, h)),
        out_shape=jax.ShapeDtypeStruct((B, S, DA_W), MM),
        scratch_shapes=[pltpu.VMEM((2 * DA_TQ, 128), F32), pltpu.VMEM((2 * DA_TQ, 256), F32),
                        pltpu.VMEM((2, 2 * DA_TQ, tk), F32), pltpu.VMEM((S, 256), MM)],
        compiler_params=_params(("arbitrary", "arbitrary", "arbitrary")),
        name="diff_attention",
    )(qkv, qkv, qkv, lq1, lk1, lq2, lk2, sg)


def _route(logits_t, rb):
    s = _sigmoid(logits_t)
    sel = s + rb
    x = [sel[e:e + 1, :] for e in range(N_EXPERTS)]
    tg = []
    for g in range(N_GROUPS):
        best = None
        for a in range(EPG):
            for b in range(a + 1, EPG):
                pair = x[EPG * g + a] + x[EPG * g + b]
                best = pair if best is None else jnp.maximum(best, pair)
        tg.append(best)
    rows = []
    gsel = []
    for g in range(N_GROUPS):
        chosen = None
        for g2 in range(N_GROUPS):
            if g2 == g:
                continue
            c = (tg[g] > tg[g2]) if g2 < g else (tg[g] >= tg[g2])
            chosen = c if chosen is None else jnp.logical_and(chosen, c)
        for a in range(EPG):
            rank = jnp.zeros_like(x[0])
            for b in range(EPG):
                if b == a:
                    continue
                ahead = (x[EPG * g + b] >= x[EPG * g + a]) if b < a else (x[EPG * g + b] > x[EPG * g + a])
                rank = rank + jnp.where(ahead, 1.0, 0.0)
            picked = jnp.logical_and(chosen, rank < 1.5)
            rows.append(jnp.where(picked, s[EPG * g + a:EPG * g + a + 1, :], 0.0))
        gsel.append(jnp.where(chosen, 1.0, 0.0))
    comb = jnp.concatenate(rows, axis=0)
    return comb * (1.0 / jnp.sum(comb, axis=0, keepdims=True)), jnp.concatenate(gsel, axis=0)


def _router_kernel(h_ref, mod_ref, rwt_ref, rb_ref, u_ref, rt_ref, *, nB, tile0):
    b = pl.program_id(0)
    i = pl.program_id(1)
    row = jnp.where(i + tile0 == 0, nB, b)
    u = h_ref[...] * (1.0 + _mod_row(mod_ref, row, 4)) + _mod_row(mod_ref, row, 3)
    u_ref[...] = u.astype(MM)
    logits_t = lax.dot_general(rwt_ref[...], u, (((1,), (1,)), ((), ())), precision=HIGHEST,
                               preferred_element_type=F32)
    comb_t, gsel_t = _route(logits_t, rb_ref[...])
    pad = jnp.zeros((MOE_RT_ROWS - N_EXPERTS - N_GROUPS, TM), F32)
    rt_ref[...] = jnp.concatenate([comb_t, gsel_t, pad], axis=0)


def _router_call(h, mod, rwt, rb, nB, tile0):
    B, S, _ = h.shape
    R = mod.shape[0]
    nt = S // TM - tile0
    full = lambda shape: pl.BlockSpec(shape, lambda b, i: (0,) * len(shape))
    return pl.pallas_call(
        functools.partial(_router_kernel, nB=nB, tile0=tile0),
        grid=(B, nt),
        in_specs=[pl.BlockSpec((None, TM, D), lambda b, i: (b, i + tile0, 0)),
                  full((R, 6 * D)), full((N_EXPERTS, D)), full((N_EXPERTS, 1))],
        out_specs=[pl.BlockSpec((None, TM, D), lambda b, i: (b, i, 0)),
                   pl.BlockSpec((MOE_RT_ROWS, TM), lambda b, i: (0, b * nt + i))],
        out_shape=[jax.ShapeDtypeStruct((B, nt * TM, D), MM),
                   jax.ShapeDtypeStruct((MOE_RT_ROWS, B * nt * TM), F32)],
        compiler_params=_params(("parallel", "arbitrary")),
        name="moe_router",
    )(h, mod, rwt, rb)


def _expert_kernel(cnt_ref, u_ref, rt_ref, wg_ref, wu_ref, wd_ref, f_ref, xs_scr, cs_scr, fs_scr, pos_scr,
                   *, TE, nt):
    t = pl.program_id(0)
    g = pl.program_id(1)
    n = [cnt_ref[gg * nt + t] for gg in range(N_GROUPS)]
    off = [jnp.int32(0)]
    for gg in range(N_GROUPS - 1):
        off.append(off[-1] + n[gg])

    @pl.when(g == 0)
    def _():
        rt = rt_ref[...]
        gs = rt[N_EXPERTS:N_EXPERTS + 8, :]
        r_i = lax.broadcasted_iota(jnp.int32, (TE, TE), 0)
        c_i = lax.broadcasted_iota(jnp.int32, (TE, TE), 1)
        before = jnp.where(r_i < c_i, 1.0, 0.0).astype(MM)
        cnt_before = jnp.dot(gs.astype(MM), before, preferred_element_type=F32)
        pos = jnp.zeros((1, TE), F32)
        for gg in range(N_GROUPS):
            pos = pos + gs[gg:gg + 1, :] * (off[gg].astype(F32) + cnt_before[gg:gg + 1, :])
        perm = jnp.where(r_i.astype(F32) == pos, 1.0, 0.0).astype(MM)
        xs_scr[...] = jnp.dot(perm, u_ref[...], preferred_element_type=F32).astype(MM)
        rt_pad = jnp.concatenate([rt, jnp.zeros((128 - MOE_RT_ROWS, TE), F32)], axis=0)
        hi, lo = _split(rt_pad)
        cs_scr[...] = _dot_nt(perm, hi) + _dot_nt(perm, lo)
        pos_scr[...] = jnp.broadcast_to(pos, (128, TE)).T
        fs_scr[...] = jnp.zeros_like(fs_scr)

    lo_g = jnp.int32(0)
    n_g = jnp.int32(0)
    for gg in range(N_GROUPS):
        lo_g = jnp.where(g == gg, off[gg], lo_g)
        n_g = jnp.where(g == gg, n[gg], n_g)
    hi_g = lo_g + n_g
    lane = lax.broadcasted_iota(jnp.int32, (MOE_BLK, 128), 1)
    for rb in range(TE // MOE_BLK):
        @pl.when(jnp.logical_and(lo_g < MOE_BLK * (rb + 1), hi_g > MOE_BLK * rb))
        def _(rb=rb):
            rows = slice(rb * MOE_BLK, (rb + 1) * MOE_BLK)
            xb = xs_scr[rows, :]
            c = cs_scr[rows, :]
            parts = []
            for e in range(EPG):
                hg = jnp.dot(xb, wg_ref[e], preferred_element_type=F32)
                hu = jnp.dot(xb, wu_ref[e], preferred_element_type=F32)
                col = jnp.sum(jnp.where(lane == EPG * g + e, c, 0.0), axis=-1, keepdims=True)
                parts.append((hg * _sigmoid(hg) * hu * col).astype(MM))
            he = jnp.concatenate(parts, axis=1)
            fs_scr[rows, :] += jnp.dot(he, wd_ref[...], preferred_element_type=F32)

    @pl.when(g == N_GROUPS - 1)
    def _():
        c_i = lax.broadcasted_iota(jnp.int32, (TE, TE), 1)
        unperm = jnp.where(c_i.astype(F32) == jnp.tile(pos_scr[...], (1, TE // 128)), 1.0, 0.0).astype(MM)
        f_ref[...] = jnp.dot(unperm, fs_scr[...].astype(MM), preferred_element_type=F32).astype(f_ref.dtype)


def _expert_call(u, rt, wg, wu, wd):
    N = u.shape[0]
    TE = MOE_TE if N % MOE_TE == 0 else TM
    nt = N // TE
    cnt = jnp.sum(rt[N_EXPERTS:N_EXPERTS + N_GROUPS].reshape(N_GROUPS, nt, TE), axis=-1)
    cnt = cnt.astype(jnp.int32).reshape(N_GROUPS * nt)
    gw = EPG * D_EXPERT
    grid_spec = pltpu.PrefetchScalarGridSpec(
        num_scalar_prefetch=1,
        grid=(nt, N_GROUPS),
        in_specs=[pl.BlockSpec((TE, D), lambda t, g, c: (t, 0)),
                  pl.BlockSpec((MOE_RT_ROWS, TE), lambda t, g, c: (0, t)),
                  pl.BlockSpec((None, EPG, D, D_EXPERT), lambda t, g, c: (g, 0, 0, 0)),
                  pl.BlockSpec((None, EPG, D, D_EXPERT), lambda t, g, c: (g, 0, 0, 0)),
                  pl.BlockSpec((None, gw, D), lambda t, g, c: (g, 0, 0))],
        out_specs=pl.BlockSpec((TE, D), lambda t, g, c: (t, 0)),
        scratch_shapes=[pltpu.VMEM((TE, D), MM), pltpu.VMEM((TE, 128), F32), pltpu.VMEM((TE, D), F32),
                        pltpu.VMEM((TE, 128), F32)])
    return pl.pallas_call(
        functools.partial(_expert_kernel, TE=TE, nt=nt),
        grid_spec=grid_spec,
        out_shape=jax.ShapeDtypeStruct((N, D), MM),
        compiler_params=_params(("parallel", "arbitrary")),
        name="moe_experts",
    )(cnt, u, rt, wg, wu, wd)


def _moe_ln_kernel(h_ref, mod_ref, f_ref, lng_ref, lnb_ref, o_ref, *, nB, tile0):
    b = pl.program_id(0)
    i = pl.program_id(1)
    row = jnp.where(i + tile0 == 0, nB, b)
    _res_ln(h_ref, mod_ref, row, 5, f_ref[...].astype(F32), lng_ref, lnb_ref, o_ref)


def _moe_ln_call(h, mod, f, lng, lnb, nB, tile0):
    B, S, _ = h.shape
    R = mod.shape[0]
    nt = S // TM - tile0
    full = lambda shape: pl.BlockSpec(shape, lambda b, i: (0,) * len(shape))
    return pl.pallas_call(
        functools.partial(_moe_ln_kernel, nB=nB, tile0=tile0),
        grid=(B, nt),
        in_specs=[pl.BlockSpec((None, TM, D), lambda b, i: (b, i + tile0, 0)),
                  full((R, 6 * D)),
                  pl.BlockSpec((None, TM, D), lambda b, i: (b, i, 0)),
                  full((1, D)), full((1, D))],
        out_specs=pl.BlockSpec((None, TM, D), lambda b, i: (b, i, 0)),
        out_shape=jax.ShapeDtypeStruct((B, nt * TM, D), F32),
        compiler_params=_params(("parallel", "arbitrary")),
        name="moe_res_ln",
    )(h, mod, f, lng, lnb)


def _moe_call(h, mod, rwt, rb, wg, wu, wd, lng, lnb, nB, latent_only, L):
    B = h.shape[0]
    tile0 = L // TM if latent_only else 0
    u, rt = _router_call(h, mod, rwt, rb, nB, tile0)
    f = _expert_call(u.reshape(-1, D), rt, wg.reshape(N_GROUPS, EPG, D, D_EXPERT),
                     wu.reshape(N_GROUPS, EPG, D, D_EXPERT), wd.reshape(N_GROUPS, EPG * D_EXPERT, D))
    return _moe_ln_call(h, mod, f.reshape(B, -1, D), lng, lnb, nB, tile0)


def kernel(x, c, ctx, c_ctx, ada_w, ada_b, ln_g, ln_b, even_w_in, even_w_out, shift_w, na_rpb, rw_w0, rw_w_up, rw_a0, rw_a_up, rw_g_up, rw_k_k, rw_k_a, rw_r_k, rw_gn_g, rw_gn_b, odd_w_in, odd_w_out, da_lq1, da_lk1, da_lq2, da_lk2, da_subln_g, router_w, router_b, exp_w_gate, exp_w_up, exp_w_down):
    B, T, _ = x.shape
    L = ctx.shape[1]
    assert L == TM and T % TM == 0 and (T // GRID_W) >= 12
    rows = T // GRID_W
    R = -(-(B + 1) // 8) * 8

    cv = jnp.zeros((R, D), F32).at[:B].set(c).at[B].set(c_ctx)
    mods = _ada_call(cv, ada_w, ada_b)

    cos_t, sin_t = _rope_tables(L, T)
    rwt = router_w.T
    rb = router_b.reshape(N_EXPERTS, 1)
    zpad = jnp.zeros((2, 64, RW_W), F32)

    h = jnp.concatenate([ctx, x], axis=1)
    for l in range(DEPTH):
        mod = mods[l]
        i = l // 2
        lng = ln_g[l].reshape(2, 1, D)
        lnb = ln_b[l].reshape(2, 1, D)
        if l % 2 == 0:
            w_in = even_w_in[i].astype(MM)
            qkv, prw = _even_in_call(h, mod, w_in[:, :3 * NA_W], w_in[:, 3 * NA_W:], B)
            ona = _na_call(qkv, _na_bias(na_rpb[i], rows), L)
            yf, yr, bg = _rw_call(
                prw, shift_w[i], rw_w0[i].reshape(2, 1, RW_W),
                jnp.concatenate([rw_w_up[i], zpad], axis=1).astype(MM), rw_a0[i].reshape(2, 1, RW_W),
                jnp.concatenate([zpad, rw_a_up[i]], axis=1).astype(MM), rw_g_up[i].astype(MM),
                rw_k_k[i].reshape(1, RW_W), rw_k_a[i].reshape(1, RW_W), rw_r_k[i].reshape(1, RW_W), L)
            h = _even_out_call(h, mod, ona, yf, yr, bg, rw_gn_g[i].reshape(1, RW_W), rw_gn_b[i].reshape(1, RW_W),
                               even_w_out[i].astype(MM), lng[0], lnb[0], B)
        else:
            lam_init = 0.8 - 0.6 * math.exp(-0.3 * l)
            qkv = _odd_in_call(h, mod, odd_w_in[i].astype(MM), cos_t, sin_t, B)
            oda = _da_call(qkv, da_lq1[i].reshape(1, 64), da_lk1[i].reshape(1, 64), da_lq2[i].reshape(1, 64),
                           da_lk2[i].reshape(1, 64), da_subln_g[i].reshape(1, 128), L, lam_init)
            h = _odd_out_call(h, mod, oda, odd_w_out[i].astype(MM), lng[0], lnb[0], B)
        h = _moe_call(h, mod, rwt, rb, exp_w_gate[l].astype(MM), exp_w_up[l].astype(MM), exp_w_down[l].astype(MM),
                      lng[1], lnb[1], B, l == DEPTH - 1, L)
    return h
```

```python
import functools
import math

import numpy as np
import jax
import jax.numpy as jnp
from jax import lax
from jax.experimental import pallas as pl
from jax.experimental.pallas import tpu as pltpu

F32 = jnp.float32
MM = jnp.bfloat16
HIGHEST = lax.Precision.HIGHEST

D = 1024
DEPTH = 4
GRID_W = 64
NA_HEADS = 8
NA_W = 512
NA_WIN_R = 8
NA_WIN_C = 16
NA_QB = 128
NA_NB = 2
NA_KROWS = 9
NA_NLOC = NA_KROWS * GRID_W
RW_W = 512
RW_HEAD = 64
RW_IN = 1792
RW_GN_EPS = 64e-5
RW_CHUNK = 64
RW_GRP = 256
DA_HEADS = 8
DA_W = 1024
ROPE_THETA = 10000.0
N_EXPERTS = 16
N_GROUPS = 4
EPG = 4
D_EXPERT = 512
MOE_RT_ROWS = 32
MOE_TE = 512
MOE_NP = 2
MOE_BLK = 128
ALPHA = (2 * DEPTH) ** 0.25
LN_EPS = 1e-5
TM = 256
DA_TQ = 256
DA_TK = (768, 512, 256)
DA_SUB = 10
DA_QSCALE = 0.125 * math.log2(math.e)
NEG = -1e30
VMEM_LIMIT = 56 * 1024 * 1024


def _dot(a, b):
    return jnp.dot(a.astype(MM), b.astype(MM), preferred_element_type=F32)


def _dot_nt(a, b):
    return lax.dot_general(a.astype(MM), b.astype(MM), (((1,), (1,)), ((), ())), preferred_element_type=F32)


def _split(x):
    hi = x.astype(MM)
    lo = (x - hi.astype(F32)).astype(MM)
    return hi, lo


def _dot_split_lhs(x, w):
    hi, lo = _split(x)
    return jnp.dot(hi, w, preferred_element_type=F32) + jnp.dot(lo, w, preferred_element_type=F32)


def _dot_split_rhs(w, x):
    hi, lo = _split(x)
    return jnp.dot(w, hi, preferred_element_type=F32) + jnp.dot(w, lo, preferred_element_type=F32)


def _sigmoid(x):
    return 1.0 / (1.0 + jnp.exp(-x))


def _params(sem):
    return pltpu.CompilerParams(dimension_semantics=sem, vmem_limit_bytes=VMEM_LIMIT)


def _mod_row(mod_ref, row, j):
    return mod_ref[pl.ds(row, 1), j * D:(j + 1) * D]


def _layer_norm(z, g, b):
    mu = jnp.mean(z, axis=-1, keepdims=True)
    zc = z - mu
    var = jnp.mean(zc * zc, axis=-1, keepdims=True)
    return zc * lax.rsqrt(var + LN_EPS) * g + b


def _ada_kernel(cv_ref, w_ref, b_ref, o_ref):
    x = cv_ref[...]
    x = x * _sigmoid(x)
    o_ref[...] = jnp.dot(x, w_ref[...], precision=HIGHEST, preferred_element_type=F32) + b_ref[...]


def _ada_call(cv, ada_w, ada_b):
    R = cv.shape[0]
    tn = 1536
    return pl.pallas_call(
        _ada_kernel,
        grid=(DEPTH, 6 * D // tn),
        in_specs=[pl.BlockSpec((R, D), lambda l, j: (0, 0)),
                  pl.BlockSpec((None, D, tn), lambda l, j: (l, 0, j)),
                  pl.BlockSpec((None, 1, tn), lambda l, j: (l, 0, j))],
        out_specs=pl.BlockSpec((None, R, tn), lambda l, j: (l, 0, j)),
        out_shape=jax.ShapeDtypeStruct((DEPTH, R, 6 * D), F32),
        compiler_params=_params(("arbitrary", "arbitrary")),
        name="ada_mod",
    )(cv, ada_w, ada_b.reshape(DEPTH, 1, 6 * D))


def _even_in_kernel(h_ref, mod_ref, wna_ref, wrw_ref, qkv_ref, prw_ref, *, nB):
    b = pl.program_id(0)
    i = pl.program_id(1)
    row = jnp.where(i == 0, nB, b)
    u = (h_ref[...] * (1.0 + _mod_row(mod_ref, row, 1)) + _mod_row(mod_ref, row, 0)).astype(MM)
    res = jnp.dot(u, wna_ref[...], preferred_element_type=F32)
    qkv_ref[:, 0:NA_W] = (res[:, 0:NA_W] * 0.125).astype(qkv_ref.dtype)
    qkv_ref[:, NA_W:] = res[:, NA_W:].astype(qkv_ref.dtype)
    prw_ref[...] = jnp.dot(u, wrw_ref[...], preferred_element_type=F32)


def _even_in_call(h, mod, w_na, w_rw, nB):
    B, S, _ = h.shape
    R = mod.shape[0]
    return pl.pallas_call(
        functools.partial(_even_in_kernel, nB=nB),
        grid=(B, S // TM),
        in_specs=[pl.BlockSpec((None, TM, D), lambda b, i: (b, i, 0)),
                  pl.BlockSpec((R, 6 * D), lambda b, i: (0, 0)),
                  pl.BlockSpec((D, 3 * NA_W), lambda b, i: (0, 0)),
                  pl.BlockSpec((D, RW_IN), lambda b, i: (0, 0))],
        out_specs=[pl.BlockSpec((None, TM, 3 * NA_W), lambda b, i: (b, i, 0)),
                   pl.BlockSpec((None, TM, RW_IN), lambda b, i: (b, i, 0))],
        out_shape=[jax.ShapeDtypeStruct((B, S, 3 * NA_W), MM),
                   jax.ShapeDtypeStruct((B, S, RW_IN), F32)],
        compiler_params=_params(("parallel", "arbitrary")),
        name="even_in_proj",
    )(h, mod, w_na, w_rw)


def _na_bias(rpb, rows):
    cq = np.arange(GRID_W)[:, None]
    ck = np.arange(GRID_W)[None, :]
    cs = np.clip(cq - NA_WIN_C // 2, 0, GRID_W - NA_WIN_C)
    ok_c = (ck >= cs) & (ck < cs + NA_WIN_C)
    dc = np.clip(ck - cq + NA_WIN_C - 1, 0, 2 * NA_WIN_C - 2)
    onehot = np.zeros((2 * NA_WIN_C - 1, GRID_W * GRID_W), np.float32)
    onehot[dc.ravel(), np.arange(GRID_W * GRID_W)] = 1.0
    rc = jnp.einsum('hrd,dn->hrn', rpb, jnp.asarray(onehot), precision=HIGHEST)
    rc = rc.reshape(rpb.shape[0], 2 * NA_WIN_R - 1, GRID_W, GRID_W)
    cases = []
    for r0 in (0, 2, 4, rows - 4, rows - 2):
        ws = int(np.clip(r0 - NA_WIN_R // 2, 0, rows - NA_KROWS))
        qrows = []
        for qi in range(NA_QB // GRID_W):
            rq = r0 + qi
            rs = int(np.clip(rq - NA_WIN_R // 2, 0, rows - NA_WIN_R))
            pieces = []
            for j in range(NA_KROWS):
                rk = ws + j
                if rs <= rk < rs + NA_WIN_R:
                    pieces.append(jnp.where(ok_c[None], rc[:, rk - rq + NA_WIN_R - 1], NEG))
                else:
                    pieces.append(jnp.full((rpb.shape[0], GRID_W, GRID_W), NEG, F32))
            qrows.append(jnp.concatenate(pieces, axis=-1))
        cases.append(jnp.concatenate(qrows, axis=-2))
    return jnp.stack(cases, axis=0)


def _na_kernel(q_ref, k_ref, v_ref, *rest, L, rows):
    bias_refs, o_ref = rest[:NA_NB], rest[NA_NB]
    i = pl.program_id(2)
    nq = L // (NA_NB * NA_QB)
    lane = lax.broadcasted_iota(jnp.int32, (NA_QB, 128), 1)
    lo = lane < 64
    kc = k_ref[0:L, :]
    vc = v_ref[0:L, :]
    ch = [(n, hh) for n in range(NA_NB) for hh in range(2)]
    qs = [q_ref[n * NA_QB:(n + 1) * NA_QB, :] for n in range(NA_NB)]
    qm = {(n, hh): jnp.where(lo if hh == 0 else jnp.logical_not(lo), qs[n], jnp.zeros_like(qs[n])) for n, hh in ch}

    def store(outs):
        for n in range(NA_NB):
            o_ref[n * NA_QB:(n + 1) * NA_QB, :] = jnp.where(lo, outs[(n, 0)], outs[(n, 1)]).astype(o_ref.dtype)

    @pl.when(i < nq)
    def _():
        s = {c: _dot_nt(qm[c], kc) for c in ch}
        m = {c: jnp.max(s[c], axis=-1, keepdims=True) for c in ch}
        p = {c: jnp.exp(s[c] - m[c]) for c in ch}
        l = {c: jnp.sum(p[c], axis=-1, keepdims=True) for c in ch}
        store({c: _dot(p[c], vc) * (1.0 / l[c]) for c in ch})

    @pl.when(i >= nq)
    def _():
        kl, vl = [], []
        for n in range(NA_NB):
            r0 = 2 * ((i - nq) * NA_NB + n)
            ws = jnp.clip(r0 - NA_WIN_R // 2, 0, rows - NA_KROWS)
            start = pl.multiple_of(L + GRID_W * ws, GRID_W)
            kl.append(k_ref[pl.ds(start, NA_NLOC), :])
            vl.append(v_ref[pl.ds(start, NA_NLOC), :])
        s_loc = {c: _dot_nt(qm[c], kl[c[0]]) + bias_refs[c[0]][c[1]] for c in ch}
        s_ctx = {c: _dot_nt(qm[c], kc) for c in ch}
        m = {c: jnp.maximum(jnp.max(s_loc[c], axis=-1, keepdims=True), jnp.max(s_ctx[c], axis=-1, keepdims=True))
             for c in ch}
        p_loc = {c: jnp.exp(s_loc[c] - m[c]) for c in ch}
        p_ctx = {c: jnp.exp(s_ctx[c] - m[c]) for c in ch}
        l = {c: jnp.sum(p_loc[c], axis=-1, keepdims=True) + jnp.sum(p_ctx[c], axis=-1, keepdims=True) for c in ch}
        store({c: (_dot(p_loc[c], vl[c[0]]) + _dot(p_ctx[c], vc)) * (1.0 / l[c]) for c in ch})


def _na_call(qkv, bias, L):
    B, S, _ = qkv.shape
    rows = (S - L) // GRID_W
    tq = NA_NB * NA_QB
    assert L % tq == 0 and (S - L) % tq == 0
    nq = L // tq
    nhp = NA_HEADS // 2

    def bias_idx(n):
        def idx(b, hp, i):
            r0 = 2 * ((i - nq) * NA_NB + n)
            c = jnp.where(r0 == 0, 0,
                          jnp.where(r0 == 2, 1, jnp.where(r0 == rows - 4, 3, jnp.where(r0 == rows - 2, 4, 2))))
            return (jnp.where(i < nq, 2, c), hp, 0, 0)
        return idx

    return pl.pallas_call(
        functools.partial(_na_kernel, L=L, rows=rows),
        grid=(B, nhp, S // tq),
        in_specs=[pl.BlockSpec((None, tq, 128), lambda b, hp, i: (b, i, hp)),
                  pl.BlockSpec((None, S, 128), lambda b, hp, i: (b, 0, nhp + hp)),
                  pl.BlockSpec((None, S, 128), lambda b, hp, i: (b, 0, 2 * nhp + hp))]
                 + [pl.BlockSpec((None, 2, NA_QB, NA_NLOC), bias_idx(n)) for n in range(NA_NB)],
        out_specs=pl.BlockSpec((None, tq, 128), lambda b, hp, i: (b, i, hp)),
        out_shape=jax.ShapeDtypeStruct((B, S, NA_W), MM),
        compiler_params=_params(("parallel", "parallel", "arbitrary")),
        name="na_attention",
    )(qkv, qkv, qkv, *([bias] * NA_NB))


def _rw_conv(p_ref, prev_ref, next_ref, c, conv_ref, nL, NC):
    p = p_ref[...]
    at_start = jnp.logical_or(c == 0, c == nL)
    at_end = jnp.logical_or(c == nL - 1, c == NC - 1)
    prow = jnp.where(at_start, 0.0, prev_ref[7:8, :])
    nrow = jnp.where(at_end, 0.0, next_ref[0:1, :])
    rid = lax.broadcasted_iota(jnp.int32, p.shape, 0)
    pm = jnp.where(rid == 0, prow, pltpu.roll(p, 1, axis=0))
    pp = jnp.where(rid == RW_CHUNK - 1, nrow, pltpu.roll(p, RW_CHUNK - 1, axis=0))
    return pm * conv_ref[0:1, :] + p * conv_ref[1:2, :] + pp * conv_ref[2:3, :]


def _rw_expand(z, bd):
    return jnp.where(bd, jnp.concatenate([z, z, z, z], axis=0), 0.0)


def _rw_prep(pc, d, w0_ref, wup_ref, a0_ref, aup_ref, kk_ref, ka_ref, ones_ref, cum_ref):
    r = pc[:, 0:RW_W]
    k = pc[:, RW_W:2 * RW_W]
    v = pc[:, 2 * RW_W:3 * RW_W]
    xwa = pc[:, 3 * RW_W:3 * RW_W + 128]
    z = w0_ref[d] + _dot(jnp.tanh(xwa), wup_ref[d])
    lw = -math.exp(-0.5) * _sigmoid(z)
    a = _sigmoid(a0_ref[d] + _dot(xwa, aup_ref[d]))
    kk = k * kk_ref[...]
    n2 = _dot_split_lhs(kk * kk, ones_ref[...])
    kk = kk * (1.0 / jnp.maximum(jnp.sqrt(n2), 1e-12))
    keff = k * (1.0 + (a - 1.0) * ka_ref[...])
    av = -kk
    bv = kk * a
    lc = _dot_split_rhs(cum_ref[d], lw)
    ltot = lc[RW_CHUNK - 1:RW_CHUNK, :] if d == 0 else lc[0:1, :]
    at = av * jnp.exp(lc - lw)
    rt = r * jnp.exp(lc)
    ginv = jnp.exp(-lc)
    bt = bv * ginv
    kt = keff * ginv
    grest = jnp.exp(ltot - lc)
    bh = bv * grest
    kh = keff * grest
    gc = jnp.exp(ltot)

    return dict(at=at, rt=rt, bt=bt, kt=kt, bh=bh, kh=kh, v=v, gc=gc, keff=keff, r=r)


def _rw_chains(preps, mask_ref, g_scr):
    bd = mask_ref[0] > 0.5
    eye = mask_ref[1]
    chains = [(d, g) for d in range(2) for g in range(RW_W // RW_GRP)]
    ex = {}
    for c in chains:
        d, g = c
        sl = slice(g * RW_GRP, (g + 1) * RW_GRP)
        p = preps[d]
        e = {n: _rw_expand(p[n][:, sl], bd).astype(MM) for n in ("at", "rt", "bt", "kt", "bh", "kh")}
        e["vt"] = _rw_expand(p["v"][:, sl], bd).T.astype(MM)
        e["gc"] = p["gc"][:, sl]
        ex[c] = e
    xb = {c: _dot_nt(ex[c]["bt"], jnp.concatenate([ex[c]["at"], ex[c]["rt"]], axis=0)) for c in chains}
    xk = {c: _dot_nt(ex[c]["kt"], jnp.concatenate([ex[c]["at"], ex[c]["rt"]], axis=0)) for c in chains}
    xab, xrb, xak, xrk = {}, {}, {}, {}
    for c in chains:
        ms = mask_ref[2 + 2 * c[0]] > 0.5
        mi = mask_ref[3 + 2 * c[0]] > 0.5
        xab[c] = jnp.where(ms, xb[c][:, 0:RW_GRP], 0.0)
        xrb[c] = jnp.where(mi, xb[c][:, RW_GRP:], 0.0).astype(MM)
        xak[c] = jnp.where(ms, xk[c][:, 0:RW_GRP], 0.0).astype(MM)
        xrk[c] = jnp.where(mi, xk[c][:, RW_GRP:], 0.0).astype(MM)
    tt = {c: eye + xab[c] for c in chains}
    xp = dict(xab)
    for _ in range(5):
        xp = {c: _dot(xp[c], xp[c]) for c in chains}
        tt = {c: tt[c] + _dot(tt[c], xp[c]) for c in chains}
    gs = {c: g_scr[c[0], c[1]] for c in chains}
    gsb = {c: gs[c].astype(MM) for c in chains}
    w1 = {c: _dot_nt(gsb[c], ex[c]["at"]) + _dot(ex[c]["vt"], xak[c]) for c in chains}
    ut = {c: _dot(w1[c], tt[c]).astype(MM) for c in chains}
    yt = {c: _dot_nt(gsb[c], ex[c]["rt"]) + _dot(ut[c], xrb[c]) + _dot(ex[c]["vt"], xrk[c]) for c in chains}
    for c in chains:
        g_scr[c[0], c[1]] = gs[c] * ex[c]["gc"] + _dot(ut[c], ex[c]["bh"]) + _dot(ex[c]["vt"], ex[c]["kh"])
    ys = {}
    for c in chains:
        ybd = yt[c].T
        ys[c] = ybd[0:64] + ybd[64:128] + ybd[128:192] + ybd[192:256]
    return [jnp.concatenate([ys[(d, g)] for g in range(RW_W // RW_GRP)], axis=1) for d in range(2)]


def _rw_kernel(pf_ref, pfp_ref, pfn_ref, pr_ref, prp_ref, prn_ref, conv_ref, w0_ref, wup_ref, a0_ref, aup_ref,
               gup_ref, kk_ref, ka_ref, rk_ref, ones_ref, cum_ref, mask_ref, yf_ref, yr_ref, bg_ref, g_scr,
               *, nL, NC):
    i = pl.program_id(1)

    @pl.when(i == 0)
    def _():
        g_scr[...] = jnp.zeros_like(g_scr)

    cf = i
    cr = jnp.where(i < nL, nL - 1 - i, NC - 1 - i + nL)
    args = (w0_ref, wup_ref, a0_ref, aup_ref, kk_ref, ka_ref, ones_ref, cum_ref)

    pcf = _rw_conv(pf_ref, pfp_ref, pfn_ref, cf, conv_ref, nL, NC)
    pcr = _rw_conv(pr_ref, prp_ref, prn_ref, cr, conv_ref, nL, NC)
    pf = _rw_prep(pcf, 0, *args)
    pr = _rw_prep(pcr, 1, *args)
    xwa = pcf[:, 3 * RW_W:3 * RW_W + 128]
    a_r = _sigmoid(a0_ref[1] + _dot(xwa, aup_ref[1]))
    keff_r = pcf[:, RW_W:2 * RW_W] * (1.0 + (a_r - 1.0) * ka_ref[...])
    bsum = _dot_split_lhs(pf["r"] * rk_ref[...] * (pf["keff"] + keff_r), ones_ref[...]) * pf["v"]
    bg_ref[:, 0:RW_W] = bsum
    bg_ref[:, RW_W:] = _dot(_sigmoid(pcf[:, 3 * RW_W + 128:]), gup_ref[...])

    y_f, y_r = _rw_chains([pf, pr], mask_ref, g_scr)
    yf_ref[...] = y_f
    yr_ref[...] = y_r


def _rw_masks():
    n = 4 * RW_CHUNK
    idx = np.arange(n)
    hd = idx // RW_CHUNK
    t = idx % RW_CHUNK
    bd = hd[:, None] == hd[None, :]
    m = np.zeros((6, n, n), np.float32)
    m[0] = bd
    m[1] = np.eye(n)
    m[2] = bd & (t[:, None] < t[None, :])
    m[3] = bd & (t[:, None] <= t[None, :])
    m[4] = bd & (t[:, None] > t[None, :])
    m[5] = bd & (t[:, None] >= t[None, :])
    tt = np.arange(RW_CHUNK)
    cum = np.stack([tt[None, :] <= tt[:, None], tt[None, :] >= tt[:, None]]).astype(np.float32)
    ch = np.arange(RW_W) // RW_HEAD
    ones = (ch[:, None] == ch[None, :]).astype(np.float32)
    return m, cum, ones


def _rw_call(prw, conv_w, w0, wup_pad, a0, aup_pad, gup, kk, ka, rk, L):
    B, S, _ = prw.shape
    NC = S // RW_CHUNK
    nL = L // RW_CHUNK
    n8 = S // 8
    m, cum, ones = _rw_masks()

    def cr_of(i):
        return jnp.where(i < nL, nL - 1 - i, NC - 1 - i + nL)

    full = lambda shape: pl.BlockSpec(shape, lambda b, i: (0,) * len(shape))
    in_specs = [
        pl.BlockSpec((None, RW_CHUNK, RW_IN), lambda b, i: (b, i, 0)),
        pl.BlockSpec((None, 8, RW_IN), lambda b, i: (b, jnp.maximum(i * 8 - 1, 0), 0)),
        pl.BlockSpec((None, 8, RW_IN), lambda b, i: (b, jnp.minimum(i * 8 + 8, n8 - 1), 0)),
        pl.BlockSpec((None, RW_CHUNK, RW_IN), lambda b, i: (b, cr_of(i), 0)),
        pl.BlockSpec((None, 8, RW_IN), lambda b, i: (b, jnp.maximum(cr_of(i) * 8 - 1, 0), 0)),
        pl.BlockSpec((None, 8, RW_IN), lambda b, i: (b, jnp.minimum(cr_of(i) * 8 + 8, n8 - 1), 0)),
        full((3, RW_IN)), full((2, 1, RW_W)), full((2, 128, RW_W)), full((2, 1, RW_W)), full((2, 128, RW_W)),
        full((128, RW_W)), full((1, RW_W)), full((1, RW_W)), full((1, RW_W)),
        full((RW_W, RW_W)), full((2, RW_CHUNK, RW_CHUNK)), full((6, 4 * RW_CHUNK, 4 * RW_CHUNK)),
    ]
    out_specs = [
        pl.BlockSpec((None, RW_CHUNK, RW_W), lambda b, i: (b, i, 0)),
        pl.BlockSpec((None, RW_CHUNK, RW_W), lambda b, i: (b, cr_of(i), 0)),
        pl.BlockSpec((None, RW_CHUNK, 2 * RW_W), lambda b, i: (b, i, 0)),
    ]
    return pl.pallas_call(
        functools.partial(_rw_kernel, nL=nL, NC=NC),
        grid=(B, NC),
        in_specs=in_specs,
        out_specs=out_specs,
        out_shape=[jax.ShapeDtypeStruct((B, S, RW_W), F32),
                   jax.ShapeDtypeStruct((B, S, RW_W), F32),
                   jax.ShapeDtypeStruct((B, S, 2 * RW_W), F32)],
        scratch_shapes=[pltpu.VMEM((2, RW_W // RW_GRP, RW_GRP, RW_GRP), F32)],
        compiler_params=_params(("parallel", "arbitrary")),
        name="rwkv7_chunked",
    )(prw, prw, prw, prw, prw, prw, conv_w, w0, wup_pad, a0, aup_pad, gup, kk, ka, rk,
      jnp.asarray(ones, MM), jnp.asarray(cum, MM), jnp.asarray(m))


def _res_ln(h_ref, mod_ref, row, j, y, lng_ref, lnb_ref, o_ref):
    z = ALPHA * h_ref[...] + _mod_row(mod_ref, row, j) * y
    o_ref[...] = _layer_norm(z, lng_ref[...], lnb_ref[...])


def _even_out_kernel(h_ref, mod_ref, ona_ref, yf_ref, yr_ref, bg_ref, ones_ref, gng_ref, gnb_ref, wo_ref,
                     lng_ref, lnb_ref, o_ref, *, nB):
    b = pl.program_id(0)
    i = pl.program_id(1)
    row = jnp.where(i == 0, nB, b)
    y = yf_ref[...] + yr_ref[...]
    mu = _dot_split_lhs(y, ones_ref[...]) * (1.0 / RW_HEAD)
    yc = y - mu
    var = _dot_split_lhs(yc * yc, ones_ref[...]) * (1.0 / RW_HEAD)
    yn = yc * lax.rsqrt(var + RW_GN_EPS) * gng_ref[...] + gnb_ref[...]
    orw = (yn + bg_ref[:, 0:RW_W]) * bg_ref[:, RW_W:]
    yy = (jnp.dot(ona_ref[...], wo_ref[0:NA_W, :], preferred_element_type=F32)
          + jnp.dot(orw.astype(MM), wo_ref[NA_W:, :], preferred_element_type=F32))
    _res_ln(h_ref, mod_ref, row, 2, yy, lng_ref, lnb_ref, o_ref)


def _even_out_call(h, mod, ona, yf, yr, bg, gng, gnb, wo, lng, lnb, nB):
    B, S, _ = h.shape
    R = mod.shape[0]
    _, _, ones = _rw_masks()
    tile = lambda w: pl.BlockSpec((None, TM, w), lambda b, i: (b, i, 0))
    full = lambda shape: pl.BlockSpec(shape, lambda b, i: (0,) * len(shape))
    return pl.pallas_call(
        functools.partial(_even_out_kernel, nB=nB),
        grid=(B, S // TM),
        in_specs=[tile(D), full((R, 6 * D)), tile(NA_W), tile(RW_W), tile(RW_W), tile(2 * RW_W),
                  full((RW_W, RW_W)), full((1, RW_W)), full((1, RW_W)), full((D, D)), full((1, D)), full((1, D))],
        out_specs=tile(D),
        out_shape=jax.ShapeDtypeStruct((B, S, D), F32),
        compiler_params=_params(("parallel", "arbitrary")),
        name="even_out_proj_ln",
    )(h, mod, ona, yf, yr, bg, jnp.asarray(ones, MM), gng, gnb, wo, lng, lnb)


def _odd_out_kernel(h_ref, mod_ref, oda_ref, wo_ref, lng_ref, lnb_ref, o_ref, *, nB):
    b = pl.program_id(0)
    i = pl.program_id(1)
    row = jnp.where(i == 0, nB, b)
    yy = jnp.dot(oda_ref[...], wo_ref[...], preferred_element_type=F32)
    _res_ln(h_ref, mod_ref, row, 2, yy, lng_ref, lnb_ref, o_ref)


def _odd_out_call(h, mod, oda, wo, lng, lnb, nB):
    B, S, _ = h.shape
    R = mod.shape[0]
    tile = lambda w: pl.BlockSpec((None, TM, w), lambda b, i: (b, i, 0))
    full = lambda shape: pl.BlockSpec(shape, lambda b, i: (0,) * len(shape))
    return pl.pallas_call(
        functools.partial(_odd_out_kernel, nB=nB),
        grid=(B, S // TM),
        in_specs=[tile(D), full((R, 6 * D)), tile(DA_W), full((DA_W, D)), full((1, D)), full((1, D))],
        out_specs=tile(D),
        out_shape=jax.ShapeDtypeStruct((B, S, D), F32),
        compiler_params=_params(("parallel", "arbitrary")),
        name="odd_out_proj_ln",
    )(h, mod, oda, wo, lng, lnb)


def _odd_in_kernel(h_ref, mod_ref, w_ref, cos_ref, sin_ref, o_ref, *, nB):
    b = pl.program_id(0)
    i = pl.program_id(1)
    row = jnp.where(i == 0, nB, b)
    u = (h_ref[...] * (1.0 + _mod_row(mod_ref, row, 1)) + _mod_row(mod_ref, row, 0)).astype(MM)
    res = jnp.dot(u, w_ref[...], preferred_element_type=F32)
    cs = cos_ref[...]
    sn = sin_ref[...]
    lane = lax.broadcasted_iota(jnp.int32, cs.shape, 1)
    first = (lane % 32) < 16
    for j in range(2 * DA_W // 128):
        zj = res[:, j * 128:(j + 1) * 128]
        sw = jnp.where(first, pltpu.roll(zj, 112, axis=1), pltpu.roll(zj, 16, axis=1))
        rot = zj * cs + sw * sn
        if j < DA_W // 128:
            rot = rot * DA_QSCALE
        o_ref[:, j * 128:(j + 1) * 128] = rot.astype(o_ref.dtype)
    o_ref[:, 2 * DA_W:] = res[:, 2 * DA_W:].astype(o_ref.dtype)


def _odd_in_call(h, mod, w, cos_t, sin_t, nB):
    B, S, _ = h.shape
    R = mod.shape[0]
    return pl.pallas_call(
        functools.partial(_odd_in_kernel, nB=nB),
        grid=(B, S // TM),
        in_specs=[pl.BlockSpec((None, TM, D), lambda b, i: (b, i, 0)),
                  pl.BlockSpec((R, 6 * D), lambda b, i: (0, 0)),
                  pl.BlockSpec((D, 3 * DA_W), lambda b, i: (0, 0)),
                  pl.BlockSpec((TM, 128), lambda b, i: (i, 0)),
                  pl.BlockSpec((TM, 128), lambda b, i: (i, 0))],
        out_specs=pl.BlockSpec((None, TM, 3 * DA_W), lambda b, i: (b, i, 0)),
        out_shape=jax.ShapeDtypeStruct((B, S, 3 * DA_W), MM),
        compiler_params=_params(("parallel", "arbitrary")),
        name="odd_in_proj_rope",
    )(h, mod, w, cos_t, sin_t)


def _rope_tables(L, T):
    nf = 16
    inv = ROPE_THETA ** (-jnp.arange(nf, dtype=F32) / nf)
    t = jnp.arange(T)
    ang_r = (t // GRID_W).astype(F32)[:, None] * inv
    ang_c = (t % GRID_W).astype(F32)[:, None] * inv
    cos64 = jnp.concatenate([jnp.cos(ang_r), jnp.cos(ang_r), jnp.cos(ang_c), jnp.cos(ang_c)], -1)
    sin64 = jnp.concatenate([-jnp.sin(ang_r), jnp.sin(ang_r), -jnp.sin(ang_c), jnp.sin(ang_c)], -1)
    cos_t = jnp.concatenate([jnp.ones((L, 128), F32), jnp.tile(cos64, (1, 2))], 0)
    sin_t = jnp.concatenate([jnp.zeros((L, 128), F32), jnp.tile(sin64, (1, 2))], 0)
    return cos_t, sin_t


def _da_kernel(q_ref, k_ref, v_ref, lq1_ref, lk1_ref, lq2_ref, lk2_ref, sg_ref, o_ref, m_scr, acc_scr, s_scr, va_scr,
               *, L, tk, n_chunks, sub, lam_init):
    i = pl.program_id(2)
    n_rest = n_chunks - 1

    @pl.when(i == 0)
    def _():
        va_scr[:, 0:128] = v_ref[...]
        va_scr[:, 128:256] = jnp.ones((va_scr.shape[0], 128), va_scr.dtype)

    q = q_ref[...]
    lane = lax.broadcasted_iota(jnp.int32, q.shape, 1)
    lo = lane < 64
    zero = jnp.zeros_like(q)
    qq = jnp.concatenate([jnp.where(lo, q, zero), jnp.where(lo, zero, q)], axis=0)

    def chunk(ref, j):
        if isinstance(j, int):
            return ref[j * tk:(j + 1) * tk, :]
        return ref[pl.ds(pl.multiple_of(j * tk, tk), tk), :]

    def scores(j):
        return _dot_nt(qq, chunk(k_ref, j))

    def softmax_pv(s, vb, first):
        m_cur = jnp.max(s, axis=-1, keepdims=True)
        if first:
            m_new = jnp.broadcast_to(m_cur, m_scr.shape)
            acc_scr[...] = _dot(jnp.exp2(s - m_cur), vb)
        else:
            m_old = m_scr[...]
            m_new = jnp.maximum(m_old, m_cur)
            alpha = jnp.exp2(m_old - m_new)
            p = jnp.exp2(s - jnp.tile(m_new, (1, s.shape[1] // 128)))
            acc_scr[...] = jnp.tile(alpha, (1, 2)) * acc_scr[...] + _dot(p, vb)
        m_scr[...] = m_new

    @pl.when(i == 0)
    def _():
        softmax_pv(_dot_nt(qq, k_ref[0:L, :]), va_scr[0:L, :], True)

    @pl.when(i > 0)
    def _():
        s_first = scores(0)
        s_scr[0] = scores(1)
        softmax_pv(s_first, chunk(va_scr, 0), True)

        def body(jj, carry):
            for si in range(sub):
                j = 1 + si if sub == n_rest else 1 + sub * jj + si
                s = s_scr[si % 2]
                if sub < n_rest:
                    s_scr[1 - si % 2] = scores(jnp.minimum(j + 1, n_chunks - 1))
                elif j + 1 < n_chunks:
                    s_scr[1 - si % 2] = scores(j + 1)
                softmax_pv(s, chunk(va_scr, j), False)
            return carry
        if sub < n_rest:
            lax.fori_loop(0, n_rest // sub, body, 0)
        else:
            body(0, 0)

    lam = (jnp.exp(jnp.sum(lq1_ref[...] * lk1_ref[...], axis=-1, keepdims=True))
           - jnp.exp(jnp.sum(lq2_ref[...] * lk2_ref[...], axis=-1, keepdims=True)) + lam_init)
    o_all = acc_scr[:, 0:128] * (1.0 / acc_scr[:, 128:256])
    o = o_all[0:DA_TQ] - lam * o_all[DA_TQ:]
    o = o * lax.rsqrt(jnp.mean(o * o, axis=-1, keepdims=True) + 1e-5) * sg_ref[...] * (1.0 - lam_init)
    o_ref[...] = o.astype(o_ref.dtype)


def _da_call(qkv, lq1, lk1, lq2, lk2, sg, L, lam_init):
    B, S, _ = qkv.shape
    tk = next(t for t in DA_TK if S % t == 0 and t >= L)
    n_chunks = S // tk
    n_rest = n_chunks - 1
    sub = DA_SUB if n_rest % DA_SUB == 0 else (2 if n_rest % 2 == 0 else n_rest)
    full = lambda shape: pl.BlockSpec(shape, lambda b, h, i: (0,) * len(shape))
    return pl.pallas_call(
        functools.partial(_da_kernel, L=L, tk=tk, n_chunks=n_chunks, sub=sub, lam_init=lam_init),
        grid=(B, DA_HEADS, S // DA_TQ),
        in_specs=[pl.BlockSpec((None, DA_TQ, 128), lambda b, h, i: (b, i, h)),
                  pl.BlockSpec((None, S, 128), lambda b, h, i: (b, 0, DA_HEADS + h)),
                  pl.BlockSpec((None, S, 128), lambda b, h, i: (b, 0, 2 * DA_HEADS + h)),
                  full((1, 64)), full((1, 64)), full((1, 64)), full((1, 64)), full((1, 128))],
        out_specs=pl.BlockSpec((None, DA_TQ, 128), lambda b, h, i: (b, i, h)),
        out_shape=jax.ShapeDtypeStruct((B, S, DA_W), MM),
        scratch_shapes=[pltpu.VMEM((2 * DA_TQ, 128), F32), pltpu.VMEM((2 * DA_TQ, 256), F32),
                        pltpu.VMEM((2, 2 * DA_TQ, tk), F32), pltpu.VMEM((S, 256), MM)],
        compiler_params=_params(("arbitrary", "arbitrary", "arbitrary")),
        name="diff_attention",
    )(qkv, qkv, qkv, lq1, lk1, lq2, lk2, sg)


def _route(logits_t, rb):
    s = _sigmoid(logits_t)
    sel = s + rb
    x = [sel[e:e + 1, :] for e in range(N_EXPERTS)]
    tg = []
    for g in range(N_GROUPS):
        best = None
        for a in range(EPG):
            for b in range(a + 1, EPG):
                pair = x[EPG * g + a] + x[EPG * g + b]
                best = pair if best is None else jnp.maximum(best, pair)
        tg.append(best)
    rows = []
    gsel = []
    for g in range(N_GROUPS):
        chosen = None
        for g2 in range(N_GROUPS):
            if g2 == g:
                continue
            c = (tg[g] > tg[g2]) if g2 < g else (tg[g] >= tg[g2])
            chosen = c if chosen is None else jnp.logical_and(chosen, c)
        for a in range(EPG):
            rank = jnp.zeros_like(x[0])
            for b in range(EPG):
                if b == a:
                    continue
                ahead = (x[EPG * g + b] >= x[EPG * g + a]) if b < a else (x[EPG * g + b] > x[EPG * g + a])
                rank = rank + jnp.where(ahead, 1.0, 0.0)
            picked = jnp.logical_and(chosen, rank < 1.5)
            rows.append(jnp.where(picked, s[EPG * g + a:EPG * g + a + 1, :], 0.0))
        gsel.append(jnp.where(chosen, 1.0, 0.0))
    comb = jnp.concatenate(rows, axis=0)
    return comb * (1.0 / jnp.sum(comb, axis=0, keepdims=True)), jnp.concatenate(gsel, axis=0)


def _router_kernel(h_ref, mod_ref, rwt_ref, rb_ref, u_ref, rt_ref, *, nB, tile0):
    b = pl.program_id(0)
    i = pl.program_id(1)
    row = jnp.where(i + tile0 == 0, nB, b)
    u = h_ref[...] * (1.0 + _mod_row(mod_ref, row, 4)) + _mod_row(mod_ref, row, 3)
    u_ref[...] = u.astype(MM)
    logits_t = lax.dot_general(rwt_ref[...], u, (((1,), (1,)), ((), ())), precision=HIGHEST,
                               preferred_element_type=F32)
    comb_t, gsel_t = _route(logits_t, rb_ref[...])
    pad = jnp.zeros((MOE_RT_ROWS - N_EXPERTS - N_GROUPS, TM), F32)
    rt_ref[...] = jnp.concatenate([comb_t, gsel_t, pad], axis=0)


def _router_call(h, mod, rwt, rb, nB, tile0):
    B, S, _ = h.shape
    R = mod.shape[0]
    nt = S // TM - tile0
    full = lambda shape: pl.BlockSpec(shape, lambda b, i: (0,) * len(shape))
    return pl.pallas_call(
        functools.partial(_router_kernel, nB=nB, tile0=tile0),
        grid=(B, nt),
        in_specs=[pl.BlockSpec((None, TM, D), lambda b, i: (b, i + tile0, 0)),
                  full((R, 6 * D)), full((N_EXPERTS, D)), full((N_EXPERTS, 1))],
        out_specs=[pl.BlockSpec((None, TM, D), lambda b, i: (b, i, 0)),
                   pl.BlockSpec((MOE_RT_ROWS, TM), lambda b, i: (0, b * nt + i))],
        out_shape=[jax.ShapeDtypeStruct((B, nt * TM, D), MM),
                   jax.ShapeDtypeStruct((MOE_RT_ROWS, B * nt * TM), F32)],
        compiler_params=_params(("parallel", "arbitrary")),
        name="moe_router",
    )(h, mod, rwt, rb)


def _expert_kernel(cnt_ref, u_ref, rt_ref, wg_ref, wu_ref, wd_ref, f_ref, xs_scr, cs_scr, fs_scr, pos_scr,
                   *, TE, NP, nt):
    t = pl.program_id(0)
    g = pl.program_id(1)
    lane = lax.broadcasted_iota(jnp.int32, (MOE_BLK, 128), 1)
    for hf in range(NP):
        base = hf * TE
        n = [cnt_ref[gg * nt + t * NP + hf] for gg in range(N_GROUPS)]
        off = [jnp.int32(0)]
        for gg in range(N_GROUPS - 1):
            off.append(off[-1] + n[gg])

        @pl.when(g == 0)
        def _(base=base, off=off):
            rt = rt_ref[:, base:base + TE]
            gs = rt[N_EXPERTS:N_EXPERTS + 8, :]
            r_i = lax.broadcasted_iota(jnp.int32, (TE, TE), 0)
            c_i = lax.broadcasted_iota(jnp.int32, (TE, TE), 1)
            before = jnp.where(r_i < c_i, 1.0, 0.0).astype(MM)
            cnt_before = jnp.dot(gs.astype(MM), before, preferred_element_type=F32)
            pos = jnp.zeros((1, TE), F32)
            for gg in range(N_GROUPS):
                pos = pos + gs[gg:gg + 1, :] * (off[gg].astype(F32) + cnt_before[gg:gg + 1, :])
            perm = jnp.where(r_i.astype(F32) == pos, 1.0, 0.0).astype(MM)
            xs_scr[base:base + TE, :] = jnp.dot(perm, u_ref[base:base + TE, :],
                                                preferred_element_type=F32).astype(MM)
            rt_pad = jnp.concatenate([rt, jnp.zeros((128 - MOE_RT_ROWS, TE), F32)], axis=0)
            hi, lo = _split(rt_pad)
            cs_scr[base:base + TE, :] = _dot_nt(perm, hi) + _dot_nt(perm, lo)
            pos_scr[base:base + TE, :] = jnp.broadcast_to(pos, (128, TE)).T
            fs_scr[base:base + TE, :] = jnp.zeros((TE, D), F32)

        lo_g = jnp.int32(0)
        n_g = jnp.int32(0)
        for gg in range(N_GROUPS):
            lo_g = jnp.where(g == gg, off[gg], lo_g)
            n_g = jnp.where(g == gg, n[gg], n_g)
        hi_g = lo_g + n_g
        for rb in range(TE // MOE_BLK):
            @pl.when(jnp.logical_and(lo_g < MOE_BLK * (rb + 1), hi_g > MOE_BLK * rb))
            def _(r0=base + rb * MOE_BLK):
                rows = slice(r0, r0 + MOE_BLK)
                xb = xs_scr[rows, :]
                c = cs_scr[rows, :]
                hg = [jnp.dot(xb, wg_ref[e], preferred_element_type=F32) for e in range(EPG)]
                hu = [jnp.dot(xb, wu_ref[e], preferred_element_type=F32) for e in range(EPG)]
                col = [jnp.sum(jnp.where(lane == EPG * g + e, c, 0.0), axis=-1, keepdims=True) for e in range(EPG)]
                he = jnp.concatenate([(hg[e] * _sigmoid(hg[e]) * hu[e] * col[e]).astype(MM) for e in range(EPG)],
                                     axis=1)
                fs_scr[rows, :] += jnp.dot(he, wd_ref[...], preferred_element_type=F32)

        @pl.when(g == N_GROUPS - 1)
        def _(base=base):
            c_i = lax.broadcasted_iota(jnp.int32, (TE, TE), 1)
            pos_t = jnp.tile(pos_scr[base:base + TE, :], (1, TE // 128))
            unperm = jnp.where(c_i.astype(F32) == pos_t, 1.0, 0.0).astype(MM)
            f_ref[base:base + TE, :] = jnp.dot(unperm, fs_scr[base:base + TE, :].astype(MM),
                                               preferred_element_type=F32).astype(f_ref.dtype)


def _expert_call(u, rt, wg, wu, wd):
    N = u.shape[0]
    TE = MOE_TE if N % MOE_TE == 0 else TM
    nt = N // TE
    NP = MOE_NP if nt % MOE_NP == 0 else 1
    cnt = jnp.sum(rt[N_EXPERTS:N_EXPERTS + N_GROUPS].reshape(N_GROUPS, nt, TE), axis=-1)
    cnt = cnt.astype(jnp.int32).reshape(N_GROUPS * nt)
    gw = EPG * D_EXPERT
    grid_spec = pltpu.PrefetchScalarGridSpec(
        num_scalar_prefetch=1,
        grid=(nt // NP, N_GROUPS),
        in_specs=[pl.BlockSpec((NP * TE, D), lambda t, g, c: (t, 0)),
                  pl.BlockSpec((MOE_RT_ROWS, NP * TE), lambda t, g, c: (0, t)),
                  pl.BlockSpec((None, EPG, D, D_EXPERT), lambda t, g, c: (g, 0, 0, 0)),
                  pl.BlockSpec((None, EPG, D, D_EXPERT), lambda t, g, c: (g, 0, 0, 0)),
                  pl.BlockSpec((None, gw, D), lambda t, g, c: (g, 0, 0))],
        out_specs=pl.BlockSpec((NP * TE, D), lambda t, g, c: (t, 0)),
        scratch_shapes=[pltpu.VMEM((NP * TE, D), MM), pltpu.VMEM((NP * TE, 128), F32),
                        pltpu.VMEM((NP * TE, D), F32), pltpu.VMEM((NP * TE, 128), F32)])
    return pl.pallas_call(
        functools.partial(_expert_kernel, TE=TE, NP=NP, nt=nt),
        grid_spec=grid_spec,
        out_shape=jax.ShapeDtypeStruct((N, D), MM),
        compiler_params=_params(("parallel", "arbitrary")),
        name="moe_experts",
    )(cnt, u, rt, wg, wu, wd)


def _moe_ln_kernel(h_ref, mod_ref, f_ref, lng_ref, lnb_ref, o_ref, *, nB, tile0):
    b = pl.program_id(0)
    i = pl.program_id(1)
    row = jnp.where(i + tile0 == 0, nB, b)
    _res_ln(h_ref, mod_ref, row, 5, f_ref[...].astype(F32), lng_ref, lnb_ref, o_ref)


def _moe_ln_call(h, mod, f, lng, lnb, nB, tile0):
    B, S, _ = h.shape
    R = mod.shape[0]
    nt = S // TM - tile0
    full = lambda shape: pl.BlockSpec(shape, lambda b, i: (0,) * len(shape))
    return pl.pallas_call(
        functools.partial(_moe_ln_kernel, nB=nB, tile0=tile0),
        grid=(B, nt),
        in_specs=[pl.BlockSpec((None, TM, D), lambda b, i: (b, i + tile0, 0)),
                  full((R, 6 * D)),
                  pl.BlockSpec((None, TM, D), lambda b, i: (b, i, 0)),
                  full((1, D)), full((1, D))],
        out_specs=pl.BlockSpec((None, TM, D), lambda b, i: (b, i, 0)),
        out_shape=jax.ShapeDtypeStruct((B, nt * TM, D), F32),
        compiler_params=_params(("parallel", "arbitrary")),
        name="moe_res_ln",
    )(h, mod, f, lng, lnb)


def _moe_call(h, mod, rwt, rb, wg, wu, wd, lng, lnb, nB, latent_only, L):
    B = h.shape[0]
    tile0 = L // TM if latent_only else 0
    u, rt = _router_call(h, mod, rwt, rb, nB, tile0)
    f = _expert_call(u.reshape(-1, D), rt, wg.reshape(N_GROUPS, EPG, D, D_EXPERT),
                     wu.reshape(N_GROUPS, EPG, D, D_EXPERT), wd.reshape(N_GROUPS, EPG * D_EXPERT, D))
    return _moe_ln_call(h, mod, f.reshape(B, -1, D), lng, lnb, nB, tile0)


def kernel(x, c, ctx, c_ctx, ada_w, ada_b, ln_g, ln_b, even_w_in, even_w_out, shift_w, na_rpb, rw_w0, rw_w_up, rw_a0, rw_a_up, rw_g_up, rw_k_k, rw_k_a, rw_r_k, rw_gn_g, rw_gn_b, odd_w_in, odd_w_out, da_lq1, da_lk1, da_lq2, da_lk2, da_subln_g, router_w, router_b, exp_w_gate, exp_w_up, exp_w_down):
    B, T, _ = x.shape
    L = ctx.shape[1]
    assert L == TM and T % TM == 0 and (T // GRID_W) >= 12
    rows = T // GRID_W
    R = -(-(B + 1) // 8) * 8

    cv = jnp.zeros((R, D), F32).at[:B].set(c).at[B].set(c_ctx)
    mods = _ada_call(cv, ada_w, ada_b)

    cos_t, sin_t = _rope_tables(L, T)
    rwt = router_w.T
    rb = router_b.reshape(N_EXPERTS, 1)
    zpad = jnp.zeros((2, 64, RW_W), F32)

    h = jnp.concatenate([ctx, x], axis=1)
    for l in range(DEPTH):
        mod = mods[l]
        i = l // 2
        lng = ln_g[l].reshape(2, 1, D)
        lnb = ln_b[l].reshape(2, 1, D)
        if l % 2 == 0:
            w_in = even_w_in[i].astype(MM)
            qkv, prw = _even_in_call(h, mod, w_in[:, :3 * NA_W], w_in[:, 3 * NA_W:], B)
            ona = _na_call(qkv, _na_bias(na_rpb[i], rows), L)
            yf, yr, bg = _rw_call(
                prw, shift_w[i], rw_w0[i].reshape(2, 1, RW_W),
                jnp.concatenate([rw_w_up[i], zpad], axis=1).astype(MM), rw_a0[i].reshape(2, 1, RW_W),
                jnp.concatenate([zpad, rw_a_up[i]], axis=1).astype(MM), rw_g_up[i].astype(MM),
                rw_k_k[i].reshape(1, RW_W), rw_k_a[i].reshape(1, RW_W), rw_r_k[i].reshape(1, RW_W), L)
            h = _even_out_call(h, mod, ona, yf, yr, bg, rw_gn_g[i].reshape(1, RW_W), rw_gn_b[i].reshape(1, RW_W),
                               even_w_out[i].astype(MM), lng[0], lnb[0], B)
        else:
            lam_init = 0.8 - 0.6 * math.exp(-0.3 * l)
            qkv = _odd_in_call(h, mod, odd_w_in[i].astype(MM), cos_t, sin_t, B)
            oda = _da_call(qkv, da_lq1[i].reshape(1, 64), da_lk1[i].reshape(1, 64), da_lq2[i].reshape(1, 64),
                           da_lk2[i].reshape(1, 64), da_subln_g[i].reshape(1, 128), L, lam_init)
            h = _odd_out_call(h, mod, oda, odd_w_out[i].astype(MM), lng[0], lnb[0], B)
        h = _moe_call(h, mod, rwt, rb, exp_w_gate[l].astype(MM), exp_w_up[l].astype(MM), exp_w_down[l].astype(MM),
                      lng[1], lnb[1], B, l == DEPTH - 1, L)
    return h
```

```python
import functools
import math

import numpy as np
import jax
import jax.numpy as jnp
from jax import lax
from jax.experimental import pallas as pl
from jax.experimental.pallas import tpu as pltpu

F32 = jnp.float32
MM = jnp.bfloat16
HIGHEST = lax.Precision.HIGHEST

D = 1024
DEPTH = 4
GRID_W = 64
NA_HEADS = 8
NA_W = 512
NA_WIN_R = 8
NA_WIN_C = 16
NA_QB = 128
NA_NB = 2
NA_KROWS = 9
NA_NLOC = NA_KROWS * GRID_W
RW_W = 512
RW_HEAD = 64
RW_IN = 1792
RW_GN_EPS = 64e-5
RW_CHUNK = 64
RW_GRP = 256
RW_INV_BASE = 8
RW_INV_LEVELS = 3
RW_NMASK = 7 + RW_INV_LEVELS
DA_HEADS = 8
DA_W = 1024
ROPE_THETA = 10000.0
N_EXPERTS = 16
N_GROUPS = 4
EPG = 4
D_EXPERT = 512
MOE_RT_ROWS = 32
MOE_TE = 512
MOE_NP = 2
MOE_BLK = 128
ALPHA = (2 * DEPTH) ** 0.25
LN_EPS = 1e-5
TM = 256
DA_TQ = 256
DA_NH = 2
DA_TK = (768, 512, 256)
DA_SUB = 10
DA_QSCALE = 0.125 * math.log2(math.e)
NEG = -1e30
VMEM_LIMIT = 56 * 1024 * 1024


def _dot(a, b):
    return jnp.dot(a.astype(MM), b.astype(MM), preferred_element_type=F32)


def _dot_nt(a, b):
    return lax.dot_general(a.astype(MM), b.astype(MM), (((1,), (1,)), ((), ())), preferred_element_type=F32)


def _split(x):
    hi = x.astype(MM)
    lo = (x - hi.astype(F32)).astype(MM)
    return hi, lo


def _dot_split_lhs(x, w):
    hi, lo = _split(x)
    return jnp.dot(hi, w, preferred_element_type=F32) + jnp.dot(lo, w, preferred_element_type=F32)


def _dot_split_rhs(w, x):
    hi, lo = _split(x)
    return jnp.dot(w, hi, preferred_element_type=F32) + jnp.dot(w, lo, preferred_element_type=F32)


def _sigmoid(x):
    return 1.0 / (1.0 + jnp.exp(-x))


def _params(sem):
    return pltpu.CompilerParams(dimension_semantics=sem, vmem_limit_bytes=VMEM_LIMIT)


def _mod_row(mod_ref, row, j):
    return mod_ref[pl.ds(row, 1), j * D:(j + 1) * D]


def _layer_norm(z, g, b):
    mu = jnp.mean(z, axis=-1, keepdims=True)
    zc = z - mu
    var = jnp.mean(zc * zc, axis=-1, keepdims=True)
    return zc * lax.rsqrt(var + LN_EPS) * g + b


def _ada_kernel(cv_ref, w_ref, b_ref, o_ref):
    x = cv_ref[...]
    x = x * _sigmoid(x)
    o_ref[...] = jnp.dot(x, w_ref[...], precision=HIGHEST, preferred_element_type=F32) + b_ref[...]


def _ada_call(cv, ada_w, ada_b):
    R = cv.shape[0]
    tn = 1536
    return pl.pallas_call(
        _ada_kernel,
        grid=(DEPTH, 6 * D // tn),
        in_specs=[pl.BlockSpec((R, D), lambda l, j: (0, 0)),
                  pl.BlockSpec((None, D, tn), lambda l, j: (l, 0, j)),
                  pl.BlockSpec((None, 1, tn), lambda l, j: (l, 0, j))],
        out_specs=pl.BlockSpec((None, R, tn), lambda l, j: (l, 0, j)),
        out_shape=jax.ShapeDtypeStruct((DEPTH, R, 6 * D), F32),
        compiler_params=_params(("arbitrary", "arbitrary")),
        name="ada_mod",
    )(cv, ada_w, ada_b.reshape(DEPTH, 1, 6 * D))


def _even_in_kernel(h_ref, mod_ref, wna_ref, wrw_ref, qkv_ref, prw_ref, *, nB):
    b = pl.program_id(0)
    i = pl.program_id(1)
    row = jnp.where(i == 0, nB, b)
    u = (h_ref[...] * (1.0 + _mod_row(mod_ref, row, 1)) + _mod_row(mod_ref, row, 0)).astype(MM)
    res = jnp.dot(u, wna_ref[...], preferred_element_type=F32)
    qkv_ref[:, 0:NA_W] = (res[:, 0:NA_W] * 0.125).astype(qkv_ref.dtype)
    qkv_ref[:, NA_W:] = res[:, NA_W:].astype(qkv_ref.dtype)
    prw_ref[...] = jnp.dot(u, wrw_ref[...], preferred_element_type=F32)


def _even_in_call(h, mod, w_na, w_rw, nB):
    B, S, _ = h.shape
    R = mod.shape[0]
    return pl.pallas_call(
        functools.partial(_even_in_kernel, nB=nB),
        grid=(B, S // TM),
        in_specs=[pl.BlockSpec((None, TM, D), lambda b, i: (b, i, 0)),
                  pl.BlockSpec((R, 6 * D), lambda b, i: (0, 0)),
                  pl.BlockSpec((D, 3 * NA_W), lambda b, i: (0, 0)),
                  pl.BlockSpec((D, RW_IN), lambda b, i: (0, 0))],
        out_specs=[pl.BlockSpec((None, TM, 3 * NA_W), lambda b, i: (b, i, 0)),
                   pl.BlockSpec((None, TM, RW_IN), lambda b, i: (b, i, 0))],
        out_shape=[jax.ShapeDtypeStruct((B, S, 3 * NA_W), MM),
                   jax.ShapeDtypeStruct((B, S, RW_IN), F32)],
        compiler_params=_params(("parallel", "arbitrary")),
        name="even_in_proj",
    )(h, mod, w_na, w_rw)


def _na_bias(rpb, rows):
    cq = np.arange(GRID_W)[:, None]
    ck = np.arange(GRID_W)[None, :]
    cs = np.clip(cq - NA_WIN_C // 2, 0, GRID_W - NA_WIN_C)
    ok_c = (ck >= cs) & (ck < cs + NA_WIN_C)
    dc = np.clip(ck - cq + NA_WIN_C - 1, 0, 2 * NA_WIN_C - 2)
    onehot = np.zeros((2 * NA_WIN_C - 1, GRID_W * GRID_W), np.float32)
    onehot[dc.ravel(), np.arange(GRID_W * GRID_W)] = 1.0
    rc = jnp.einsum('hrd,dn->hrn', rpb, jnp.asarray(onehot), precision=HIGHEST)
    rc = rc.reshape(rpb.shape[0], 2 * NA_WIN_R - 1, GRID_W, GRID_W)
    cases = []
    for r0 in (0, 2, 4, rows - 4, rows - 2):
        ws = int(np.clip(r0 - NA_WIN_R // 2, 0, rows - NA_KROWS))
        qrows = []
        for qi in range(NA_QB // GRID_W):
            rq = r0 + qi
            rs = int(np.clip(rq - NA_WIN_R // 2, 0, rows - NA_WIN_R))
            pieces = []
            for j in range(NA_KROWS):
                rk = ws + j
                if rs <= rk < rs + NA_WIN_R:
                    pieces.append(jnp.where(ok_c[None], rc[:, rk - rq + NA_WIN_R - 1], NEG))
                else:
                    pieces.append(jnp.full((rpb.shape[0], GRID_W, GRID_W), NEG, F32))
            qrows.append(jnp.concatenate(pieces, axis=-1))
        cases.append(jnp.concatenate(qrows, axis=-2))
    return jnp.stack(cases, axis=0)


def _na_kernel(q_ref, k_ref, v_ref, *rest, L, rows):
    bias_refs, o_ref = rest[:NA_NB], rest[NA_NB]
    i = pl.program_id(2)
    nq = L // (NA_NB * NA_QB)
    lane = lax.broadcasted_iota(jnp.int32, (NA_QB, 128), 1)
    lo = lane < 64
    kc = k_ref[0:L, :]
    vc = v_ref[0:L, :]
    ch = [(n, hh) for n in range(NA_NB) for hh in range(2)]
    qs = [q_ref[n * NA_QB:(n + 1) * NA_QB, :] for n in range(NA_NB)]
    qm = {(n, hh): jnp.where(lo if hh == 0 else jnp.logical_not(lo), qs[n], jnp.zeros_like(qs[n])) for n, hh in ch}

    def store(outs):
        for n in range(NA_NB):
            o_ref[n * NA_QB:(n + 1) * NA_QB, :] = jnp.where(lo, outs[(n, 0)], outs[(n, 1)]).astype(o_ref.dtype)

    @pl.when(i < nq)
    def _():
        s = {c: _dot_nt(qm[c], kc) for c in ch}
        m = {c: jnp.max(s[c], axis=-1, keepdims=True) for c in ch}
        p = {c: jnp.exp(s[c] - m[c]) for c in ch}
        l = {c: jnp.sum(p[c], axis=-1, keepdims=True) for c in ch}
        store({c: _dot(p[c], vc) * (1.0 / l[c]) for c in ch})

    @pl.when(i >= nq)
    def _():
        kl, vl = [], []
        for n in range(NA_NB):
            r0 = 2 * ((i - nq) * NA_NB + n)
            ws = jnp.clip(r0 - NA_WIN_R // 2, 0, rows - NA_KROWS)
            start = pl.multiple_of(L + GRID_W * ws, GRID_W)
            kl.append(k_ref[pl.ds(start, NA_NLOC), :])
            vl.append(v_ref[pl.ds(start, NA_NLOC), :])
        s_loc = {c: _dot_nt(qm[c], kl[c[0]]) + bias_refs[c[0]][c[1]] for c in ch}
        s_ctx = {c: _dot_nt(qm[c], kc) for c in ch}
        m = {c: jnp.maximum(jnp.max(s_loc[c], axis=-1, keepdims=True), jnp.max(s_ctx[c], axis=-1, keepdims=True))
             for c in ch}
        p_loc = {c: jnp.exp(s_loc[c] - m[c]) for c in ch}
        p_ctx = {c: jnp.exp(s_ctx[c] - m[c]) for c in ch}
        l = {c: jnp.sum(p_loc[c], axis=-1, keepdims=True) + jnp.sum(p_ctx[c], axis=-1, keepdims=True) for c in ch}
        store({c: (_dot(p_loc[c], vl[c[0]]) + _dot(p_ctx[c], vc)) * (1.0 / l[c]) for c in ch})


def _na_call(qkv, bias, L):
    B, S, _ = qkv.shape
    rows = (S - L) // GRID_W
    tq = NA_NB * NA_QB
    assert L % tq == 0 and (S - L) % tq == 0
    nq = L // tq
    nhp = NA_HEADS // 2

    def bias_idx(n):
        def idx(b, hp, i):
            r0 = 2 * ((i - nq) * NA_NB + n)
            c = jnp.where(r0 == 0, 0,
                          jnp.where(r0 == 2, 1, jnp.where(r0 == rows - 4, 3, jnp.where(r0 == rows - 2, 4, 2))))
            return (jnp.where(i < nq, 2, c), hp, 0, 0)
        return idx

    return pl.pallas_call(
        functools.partial(_na_kernel, L=L, rows=rows),
        grid=(B, nhp, S // tq),
        in_specs=[pl.BlockSpec((None, tq, 128), lambda b, hp, i: (b, i, hp)),
                  pl.BlockSpec((None, S, 128), lambda b, hp, i: (b, 0, nhp + hp)),
                  pl.BlockSpec((None, S, 128), lambda b, hp, i: (b, 0, 2 * nhp + hp))]
                 + [pl.BlockSpec((None, 2, NA_QB, NA_NLOC), bias_idx(n)) for n in range(NA_NB)],
        out_specs=pl.BlockSpec((None, tq, 128), lambda b, hp, i: (b, i, hp)),
        out_shape=jax.ShapeDtypeStruct((B, S, NA_W), MM),
        compiler_params=_params(("parallel", "parallel", "arbitrary")),
        name="na_attention",
    )(qkv, qkv, qkv, *([bias] * NA_NB))


def _rw_conv(p_ref, prev_ref, next_ref, c, conv_ref, nL, NC):
    p = p_ref[...]
    at_start = jnp.logical_or(c == 0, c == nL)
    at_end = jnp.logical_or(c == nL - 1, c == NC - 1)
    prow = jnp.where(at_start, 0.0, prev_ref[7:8, :])
    nrow = jnp.where(at_end, 0.0, next_ref[0:1, :])
    rid = lax.broadcasted_iota(jnp.int32, p.shape, 0)
    pm = jnp.where(rid == 0, prow, pltpu.roll(p, 1, axis=0))
    pp = jnp.where(rid == RW_CHUNK - 1, nrow, pltpu.roll(p, RW_CHUNK - 1, axis=0))
    return pm * conv_ref[0:1, :] + p * conv_ref[1:2, :] + pp * conv_ref[2:3, :]


def _rw_expand(z, bd):
    return jnp.where(bd, jnp.concatenate([z, z, z, z], axis=0), 0.0)


def _rw_prep(pc, d, w0_ref, wup_ref, a0_ref, aup_ref, kk_ref, ka_ref, ones_ref, cum_ref):
    r = pc[:, 0:RW_W]
    k = pc[:, RW_W:2 * RW_W]
    v = pc[:, 2 * RW_W:3 * RW_W]
    xwa = pc[:, 3 * RW_W:3 * RW_W + 128]
    z = w0_ref[d] + _dot(jnp.tanh(xwa), wup_ref[d])
    lw = -math.exp(-0.5) * _sigmoid(z)
    a = _sigmoid(a0_ref[d] + _dot(xwa, aup_ref[d]))
    kk = k * kk_ref[...]
    n2 = _dot_split_lhs(kk * kk, ones_ref[...])
    kk = kk * (1.0 / jnp.maximum(jnp.sqrt(n2), 1e-12))
    keff = k * (1.0 + (a - 1.0) * ka_ref[...])
    av = -kk
    bv = kk * a
    lc = _dot_split_rhs(cum_ref[d], lw)
    ltot = lc[RW_CHUNK - 1:RW_CHUNK, :] if d == 0 else lc[0:1, :]
    at = av * jnp.exp(lc - lw)
    rt = r * jnp.exp(lc)
    ginv = jnp.exp(-lc)
    bt = bv * ginv
    kt = keff * ginv
    grest = jnp.exp(ltot - lc)
    bh = bv * grest
    kh = keff * grest
    gc = jnp.exp(ltot)

    return dict(at=at, rt=rt, bt=bt, kt=kt, bh=bh, kh=kh, v=v, gc=gc, keff=keff, r=r)


def _rw_chains(preps, mask_ref, g_scr):
    bd = mask_ref[0] > 0.5
    bdm = mask_ref[0].astype(MM)
    eye = mask_ref[1]
    chains = [(d, g) for d in range(2) for g in range(RW_W // RW_GRP)]
    ex = {}
    for c in chains:
        d, g = c
        sl = slice(g * RW_GRP, (g + 1) * RW_GRP)
        p = preps[d]
        e = {}
        for n in ("at", "rt", "bt", "kt", "bh", "kh"):
            z = p[n][:, sl].astype(MM)
            e[n] = jnp.concatenate([z, z, z, z], axis=0) * bdm
        e["vt"] = _rw_expand(p["v"][:, sl], bd).T.astype(MM)
        e["gc"] = p["gc"][:, sl]
        ex[c] = e
    xb = {c: _dot_nt(ex[c]["bt"], jnp.concatenate([ex[c]["at"], ex[c]["rt"]], axis=0)) for c in chains}
    xk = {c: _dot_nt(ex[c]["kt"], jnp.concatenate([ex[c]["at"], ex[c]["rt"]], axis=0)) for c in chains}
    xab, xrb, xak, xrk = {}, {}, {}, {}
    for c in chains:
        ms = mask_ref[2 + 2 * c[0]]
        mi = mask_ref[3 + 2 * c[0]]
        xab[c] = xb[c][:, 0:RW_GRP] * ms
        xrb[c] = (xb[c][:, RW_GRP:] * mi).astype(MM)
        xak[c] = (xk[c][:, 0:RW_GRP] * ms).astype(MM)
        xrk[c] = (xk[c][:, RW_GRP:] * mi).astype(MM)
    xp = {c: xab[c] * mask_ref[6] for c in chains}
    tt = {c: eye + xp[c] for c in chains}
    for _ in range(RW_INV_BASE.bit_length() - 2):
        xp = {c: _dot(xp[c], xp[c]) for c in chains}
        tt = {c: tt[c] + _dot(tt[c], xp[c]) for c in chains}
    for lvl in range(RW_INV_LEVELS):
        tc = {c: _dot(tt[c], xab[c] * mask_ref[7 + lvl]) for c in chains}
        tt = {c: tt[c] + _dot(tc[c], tt[c]) for c in chains}
    gs = {c: g_scr[c[0], c[1]] for c in chains}
    gsb = {c: gs[c].astype(MM) for c in chains}
    w1 = {c: _dot_nt(gsb[c], ex[c]["at"]) + _dot(ex[c]["vt"], xak[c]) for c in chains}
    ut = {c: _dot(w1[c], tt[c]).astype(MM) for c in chains}
    yt = {c: _dot_nt(gsb[c], ex[c]["rt"]) + _dot(ut[c], xrb[c]) + _dot(ex[c]["vt"], xrk[c]) for c in chains}
    for c in chains:
        g_scr[c[0], c[1]] = gs[c] * ex[c]["gc"] + _dot(ut[c], ex[c]["bh"]) + _dot(ex[c]["vt"], ex[c]["kh"])
    ys = {}
    for c in chains:
        ybd = yt[c].T
        ys[c] = ybd[0:64] + ybd[64:128] + ybd[128:192] + ybd[192:256]
    return [jnp.concatenate([ys[(d, g)] for g in range(RW_W // RW_GRP)], axis=1) for d in range(2)]


def _rw_kernel(pf_ref, pfp_ref, pfn_ref, pr_ref, prp_ref, prn_ref, conv_ref, w0_ref, wup_ref, a0_ref, aup_ref,
               gup_ref, kk_ref, ka_ref, rk_ref, ones_ref, cum_ref, mask_ref, yf_ref, yr_ref, bg_ref, g_scr,
               *, nL, NC):
    i = pl.program_id(1)

    @pl.when(i == 0)
    def _():
        g_scr[...] = jnp.zeros_like(g_scr)

    cf = i
    cr = jnp.where(i < nL, nL - 1 - i, NC - 1 - i + nL)
    args = (w0_ref, wup_ref, a0_ref, aup_ref, kk_ref, ka_ref, ones_ref, cum_ref)

    pcf = _rw_conv(pf_ref, pfp_ref, pfn_ref, cf, conv_ref, nL, NC)
    pcr = _rw_conv(pr_ref, prp_ref, prn_ref, cr, conv_ref, nL, NC)
    pf = _rw_prep(pcf, 0, *args)
    pr = _rw_prep(pcr, 1, *args)
    xwa = pcf[:, 3 * RW_W:3 * RW_W + 128]
    a_r = _sigmoid(a0_ref[1] + _dot(xwa, aup_ref[1]))
    keff_r = pcf[:, RW_W:2 * RW_W] * (1.0 + (a_r - 1.0) * ka_ref[...])
    bsum = _dot_split_lhs(pf["r"] * rk_ref[...] * (pf["keff"] + keff_r), ones_ref[...]) * pf["v"]
    bg_ref[:, 0:RW_W] = bsum
    bg_ref[:, RW_W:] = _dot(_sigmoid(pcf[:, 3 * RW_W + 128:]), gup_ref[...])

    y_f, y_r = _rw_chains([pf, pr], mask_ref, g_scr)
    yf_ref[...] = y_f
    yr_ref[...] = y_r


def _rw_masks():
    n = 4 * RW_CHUNK
    idx = np.arange(n)
    hd = idx // RW_CHUNK
    t = idx % RW_CHUNK
    bd = hd[:, None] == hd[None, :]
    m = np.zeros((RW_NMASK, n, n), np.float32)
    m[0] = bd
    m[1] = np.eye(n)
    m[2] = bd & (t[:, None] < t[None, :])
    m[3] = bd & (t[:, None] <= t[None, :])
    m[4] = bd & (t[:, None] > t[None, :])
    m[5] = bd & (t[:, None] >= t[None, :])
    same = lambda w: bd & ((t[:, None] // w) == (t[None, :] // w))
    m[6] = same(RW_INV_BASE)
    for lvl in range(RW_INV_LEVELS):
        w = RW_INV_BASE << lvl
        m[7 + lvl] = same(2 * w) & ~same(w)
    tt = np.arange(RW_CHUNK)
    cum = np.stack([tt[None, :] <= tt[:, None], tt[None, :] >= tt[:, None]]).astype(np.float32)
    ch = np.arange(RW_W) // RW_HEAD
    ones = (ch[:, None] == ch[None, :]).astype(np.float32)
    return m, cum, ones


def _rw_call(prw, conv_w, w0, wup_pad, a0, aup_pad, gup, kk, ka, rk, L):
    B, S, _ = prw.shape
    NC = S // RW_CHUNK
    nL = L // RW_CHUNK
    n8 = S // 8
    m, cum, ones = _rw_masks()

    def cr_of(i):
        return jnp.where(i < nL, nL - 1 - i, NC - 1 - i + nL)

    full = lambda shape: pl.BlockSpec(shape, lambda b, i: (0,) * len(shape))
    in_specs = [
        pl.BlockSpec((None, RW_CHUNK, RW_IN), lambda b, i: (b, i, 0)),
        pl.BlockSpec((None, 8, RW_IN), lambda b, i: (b, jnp.maximum(i * 8 - 1, 0), 0)),
        pl.BlockSpec((None, 8, RW_IN), lambda b, i: (b, jnp.minimum(i * 8 + 8, n8 - 1), 0)),
        pl.BlockSpec((None, RW_CHUNK, RW_IN), lambda b, i: (b, cr_of(i), 0)),
        pl.BlockSpec((None, 8, RW_IN), lambda b, i: (b, jnp.maximum(cr_of(i) * 8 - 1, 0), 0)),
        pl.BlockSpec((None, 8, RW_IN), lambda b, i: (b, jnp.minimum(cr_of(i) * 8 + 8, n8 - 1), 0)),
        full((3, RW_IN)), full((2, 1, RW_W)), full((2, 128, RW_W)), full((2, 1, RW_W)), full((2, 128, RW_W)),
        full((128, RW_W)), full((1, RW_W)), full((1, RW_W)), full((1, RW_W)),
        full((RW_W, RW_W)), full((2, RW_CHUNK, RW_CHUNK)), full((RW_NMASK, 4 * RW_CHUNK, 4 * RW_CHUNK)),
    ]
    out_specs = [
        pl.BlockSpec((None, RW_CHUNK, RW_W), lambda b, i: (b, i, 0)),
        pl.BlockSpec((None, RW_CHUNK, RW_W), lambda b, i: (b, cr_of(i), 0)),
        pl.BlockSpec((None, RW_CHUNK, 2 * RW_W), lambda b, i: (b, i, 0)),
    ]
    return pl.pallas_call(
        functools.partial(_rw_kernel, nL=nL, NC=NC),
        grid=(B, NC),
        in_specs=in_specs,
        out_specs=out_specs,
        out_shape=[jax.ShapeDtypeStruct((B, S, RW_W), F32),
                   jax.ShapeDtypeStruct((B, S, RW_W), F32),
                   jax.ShapeDtypeStruct((B, S, 2 * RW_W), F32)],
        scratch_shapes=[pltpu.VMEM((2, RW_W // RW_GRP, RW_GRP, RW_GRP), F32)],
        compiler_params=_params(("parallel", "arbitrary")),
        name="rwkv7_chunked",
    )(prw, prw, prw, prw, prw, prw, conv_w, w0, wup_pad, a0, aup_pad, gup, kk, ka, rk,
      jnp.asarray(ones, MM), jnp.asarray(cum, MM), jnp.asarray(m))


def _res_ln(h_ref, mod_ref, row, j, y, lng_ref, lnb_ref, o_ref):
    z = ALPHA * h_ref[...] + _mod_row(mod_ref, row, j) * y
    o_ref[...] = _layer_norm(z, lng_ref[...], lnb_ref[...])


def _even_out_kernel(h_ref, mod_ref, ona_ref, yf_ref, yr_ref, bg_ref, ones_ref, gng_ref, gnb_ref, wo_ref,
                     lng_ref, lnb_ref, o_ref, *, nB):
    b = pl.program_id(0)
    i = pl.program_id(1)
    row = jnp.where(i == 0, nB, b)
    y = yf_ref[...] + yr_ref[...]
    mu = _dot_split_lhs(y, ones_ref[...]) * (1.0 / RW_HEAD)
    yc = y - mu
    var = _dot_split_lhs(yc * yc, ones_ref[...]) * (1.0 / RW_HEAD)
    yn = yc * lax.rsqrt(var + RW_GN_EPS) * gng_ref[...] + gnb_ref[...]
    orw = (yn + bg_ref[:, 0:RW_W]) * bg_ref[:, RW_W:]
    yy = (jnp.dot(ona_ref[...], wo_ref[0:NA_W, :], preferred_element_type=F32)
          + jnp.dot(orw.astype(MM), wo_ref[NA_W:, :], preferred_element_type=F32))
    _res_ln(h_ref, mod_ref, row, 2, yy, lng_ref, lnb_ref, o_ref)


def _even_out_call(h, mod, ona, yf, yr, bg, gng, gnb, wo, lng, lnb, nB):
    B, S, _ = h.shape
    R = mod.shape[0]
    _, _, ones = _rw_masks()
    tile = lambda w: pl.BlockSpec((None, TM, w), lambda b, i: (b, i, 0))
    full = lambda shape: pl.BlockSpec(shape, lambda b, i: (0,) * len(shape))
    return pl.pallas_call(
        functools.partial(_even_out_kernel, nB=nB),
        grid=(B, S // TM),
        in_specs=[tile(D), full((R, 6 * D)), tile(NA_W), tile(RW_W), tile(RW_W), tile(2 * RW_W),
                  full((RW_W, RW_W)), full((1, RW_W)), full((1, RW_W)), full((D, D)), full((1, D)), full((1, D))],
        out_specs=tile(D),
        out_shape=jax.ShapeDtypeStruct((B, S, D), F32),
        compiler_params=_params(("parallel", "arbitrary")),
        name="even_out_proj_ln",
    )(h, mod, ona, yf, yr, bg, jnp.asarray(ones, MM), gng, gnb, wo, lng, lnb)


def _odd_out_kernel(h_ref, mod_ref, oda_ref, wo_ref, lng_ref, lnb_ref, o_ref, *, nB):
    b = pl.program_id(0)
    i = pl.program_id(1)
    row = jnp.where(i == 0, nB, b)
    yy = jnp.dot(oda_ref[...], wo_ref[...], preferred_element_type=F32)
    _res_ln(h_ref, mod_ref, row, 2, yy, lng_ref, lnb_ref, o_ref)


def _odd_out_call(h, mod, oda, wo, lng, lnb, nB):
    B, S, _ = h.shape
    R = mod.shape[0]
    tile = lambda w: pl.BlockSpec((None, TM, w), lambda b, i: (b, i, 0))
    full = lambda shape: pl.BlockSpec(shape, lambda b, i: (0,) * len(shape))
    return pl.pallas_call(
        functools.partial(_odd_out_kernel, nB=nB),
        grid=(B, S // TM),
        in_specs=[tile(D), full((R, 6 * D)), tile(DA_W), full((DA_W, D)), full((1, D)), full((1, D))],
        out_specs=tile(D),
        out_shape=jax.ShapeDtypeStruct((B, S, D), F32),
        compiler_params=_params(("parallel", "arbitrary")),
        name="odd_out_proj_ln",
    )(h, mod, oda, wo, lng, lnb)


def _odd_in_kernel(h_ref, mod_ref, w_ref, cos_ref, sin_ref, o_ref, *, nB):
    b = pl.program_id(0)
    i = pl.program_id(1)
    row = jnp.where(i == 0, nB, b)
    u = (h_ref[...] * (1.0 + _mod_row(mod_ref, row, 1)) + _mod_row(mod_ref, row, 0)).astype(MM)
    res = jnp.dot(u, w_ref[...], preferred_element_type=F32)
    cs = cos_ref[...]
    sn = sin_ref[...]
    lane = lax.broadcasted_iota(jnp.int32, cs.shape, 1)
    first = (lane % 32) < 16
    for j in range(2 * DA_W // 128):
        zj = res[:, j * 128:(j + 1) * 128]
        sw = jnp.where(first, pltpu.roll(zj, 112, axis=1), pltpu.roll(zj, 16, axis=1))
        rot = zj * cs + sw * sn
        if j < DA_W // 128:
            rot = rot * DA_QSCALE
        o_ref[:, j * 128:(j + 1) * 128] = rot.astype(o_ref.dtype)
    o_ref[:, 2 * DA_W:] = res[:, 2 * DA_W:].astype(o_ref.dtype)


def _odd_in_call(h, mod, w, cos_t, sin_t, nB):
    B, S, _ = h.shape
    R = mod.shape[0]
    return pl.pallas_call(
        functools.partial(_odd_in_kernel, nB=nB),
        grid=(B, S // TM),
        in_specs=[pl.BlockSpec((None, TM, D), lambda b, i: (b, i, 0)),
                  pl.BlockSpec((R, 6 * D), lambda b, i: (0, 0)),
                  pl.BlockSpec((D, 3 * DA_W), lambda b, i: (0, 0)),
                  pl.BlockSpec((TM, 128), lambda b, i: (i, 0)),
                  pl.BlockSpec((TM, 128), lambda b, i: (i, 0))],
        out_specs=pl.BlockSpec((None, TM, 3 * DA_W), lambda b, i: (b, i, 0)),
        out_shape=jax.ShapeDtypeStruct((B, S, 3 * DA_W), MM),
        compiler_params=_params(("parallel", "arbitrary")),
        name="odd_in_proj_rope",
    )(h, mod, w, cos_t, sin_t)


def _rope_tables(L, T):
    nf = 16
    inv = ROPE_THETA ** (-jnp.arange(nf, dtype=F32) / nf)
    t = jnp.arange(T)
    ang_r = (t // GRID_W).astype(F32)[:, None] * inv
    ang_c = (t % GRID_W).astype(F32)[:, None] * inv
    cos64 = jnp.concatenate([jnp.cos(ang_r), jnp.cos(ang_r), jnp.cos(ang_c), jnp.cos(ang_c)], -1)
    sin64 = jnp.concatenate([-jnp.sin(ang_r), jnp.sin(ang_r), -jnp.sin(ang_c), jnp.sin(ang_c)], -1)
    cos_t = jnp.concatenate([jnp.ones((L, 128), F32), jnp.tile(cos64, (1, 2))], 0)
    sin_t = jnp.concatenate([jnp.zeros((L, 128), F32), jnp.tile(sin64, (1, 2))], 0)
    return cos_t, sin_t


def _da_kernel(q_ref, k_ref, v_ref, lq1_ref, lk1_ref, lq2_ref, lk2_ref, sg_ref, o_ref, m_scr, acc_scr, s_scr, va_scr,
               *, L, tk, n_chunks, sub, lam_init):
    i = pl.program_id(2)
    n_rest = n_chunks - 1
    hs = range(DA_NH)
    hl = [slice(128 * hh, 128 * (hh + 1)) for hh in hs]

    @pl.when(i == 0)
    def _():
        for hh in hs:
            va_scr[hh, :, 0:128] = v_ref[:, hl[hh]]
            va_scr[hh, :, 128:256] = jnp.ones((va_scr.shape[1], 128), va_scr.dtype)

    lane = lax.broadcasted_iota(jnp.int32, (DA_TQ, 128), 1)
    lo = lane < 64
    qq = []
    for hh in hs:
        q = q_ref[:, hl[hh]]
        zero = jnp.zeros_like(q)
        qq.append(jnp.concatenate([jnp.where(lo, q, zero), jnp.where(lo, zero, q)], axis=0))

    def rows(j):
        if isinstance(j, int):
            return slice(j * tk, (j + 1) * tk)
        return pl.ds(pl.multiple_of(j * tk, tk), tk)

    def scores(hh, j):
        return _dot_nt(qq[hh], k_ref[rows(j), hl[hh]])

    def softmax_pv(hh, s, vb, first):
        m_cur = jnp.max(s, axis=-1, keepdims=True)
        if first:
            m_new = jnp.broadcast_to(m_cur, m_scr.shape[1:])
            acc_scr[hh] = _dot(jnp.exp2(s - m_cur), vb)
        else:
            m_old = m_scr[hh]
            m_new = jnp.maximum(m_old, m_cur)
            alpha = jnp.exp2(m_old - m_new)
            p = jnp.exp2(s - jnp.tile(m_new, (1, s.shape[1] // 128)))
            acc_scr[hh] = jnp.tile(alpha, (1, 2)) * acc_scr[hh] + _dot(p, vb)
        m_scr[hh] = m_new

    @pl.when(i == 0)
    def _():
        s = [_dot_nt(qq[hh], k_ref[0:L, hl[hh]]) for hh in hs]
        for hh in hs:
            softmax_pv(hh, s[hh], va_scr[hh, 0:L, :], True)

    @pl.when(i > 0)
    def _():
        s_first = [scores(hh, 0) for hh in hs]
        for hh in hs:
            s_scr[hh, 0] = scores(hh, 1)
        for hh in hs:
            softmax_pv(hh, s_first[hh], va_scr[hh, rows(0), :], True)

        def body(jj, carry):
            for si in range(sub):
                j = 1 + si if sub == n_rest else 1 + sub * jj + si
                for hh in hs:
                    if sub < n_rest:
                        s_scr[hh, 1 - si % 2] = scores(hh, jnp.minimum(j + 1, n_chunks - 1))
                    elif j + 1 < n_chunks:
                        s_scr[hh, 1 - si % 2] = scores(hh, j + 1)
                    softmax_pv(hh, s_scr[hh, si % 2], va_scr[hh, rows(j), :], False)
            return carry
        if sub < n_rest:
            lax.fori_loop(0, n_rest // sub, body, 0)
        else:
            body(0, 0)

    lam = (jnp.exp(jnp.sum(lq1_ref[...] * lk1_ref[...], axis=-1, keepdims=True))
           - jnp.exp(jnp.sum(lq2_ref[...] * lk2_ref[...], axis=-1, keepdims=True)) + lam_init)
    for hh in hs:
        o_all = acc_scr[hh, :, 0:128] * (1.0 / acc_scr[hh, :, 128:256])
        o = o_all[0:DA_TQ] - lam * o_all[DA_TQ:]
        o = o * lax.rsqrt(jnp.mean(o * o, axis=-1, keepdims=True) + 1e-5) * sg_ref[...] * (1.0 - lam_init)
        o_ref[:, hl[hh]] = o.astype(o_ref.dtype)


def _da_call(qkv, lq1, lk1, lq2, lk2, sg, L, lam_init):
    B, S, _ = qkv.shape
    tk = next(t for t in DA_TK if S % t == 0 and t >= L)
    n_chunks = S // tk
    n_rest = n_chunks - 1
    sub = DA_SUB if n_rest % DA_SUB == 0 else (2 if n_rest % 2 == 0 else n_rest)
    nh = DA_HEADS // DA_NH
    w = 128 * DA_NH
    full = lambda shape: pl.BlockSpec(shape, lambda b, h, i: (0,) * len(shape))
    return pl.pallas_call(
        functools.partial(_da_kernel, L=L, tk=tk, n_chunks=n_chunks, sub=sub, lam_init=lam_init),
        grid=(B, nh, S // DA_TQ),
        in_specs=[pl.BlockSpec((None, DA_TQ, w), lambda b, h, i: (b, i, h)),
                  pl.BlockSpec((None, S, w), lambda b, h, i: (b, 0, nh + h)),
                  pl.BlockSpec((None, S, w), lambda b, h, i: (b, 0, 2 * nh + h)),
                  full((1, 64)), full((1, 64)), full((1, 64)), full((1, 64)), full((1, 128))],
        out_specs=pl.BlockSpec((None, DA_TQ, w), lambda b, h, i: (b, i, h)),
        out_shape=jax.ShapeDtypeStruct((B, S, DA_W), MM),
        scratch_shapes=[pltpu.VMEM((DA_NH, 2 * DA_TQ, 128), F32), pltpu.VMEM((DA_NH, 2 * DA_TQ, 256), F32),
                        pltpu.VMEM((DA_NH, 2, 2 * DA_TQ, tk), F32), pltpu.VMEM((DA_NH, S, 256), MM)],
        compiler_params=_params(("arbitrary", "arbitrary", "arbitrary")),
        name="diff_attention",
    )(qkv, qkv, qkv, lq1, lk1, lq2, lk2, sg)


def _route(logits_t, rb):
    s = _sigmoid(logits_t)
    sel = s + rb
    x = [sel[e:e + 1, :] for e in range(N_EXPERTS)]
    tg = []
    for g in range(N_GROUPS):
        best = None
        for a in range(EPG):
            for b in range(a + 1, EPG):
                pair = x[EPG * g + a] + x[EPG * g + b]
                best = pair if best is None else jnp.maximum(best, pair)
        tg.append(best)
    rows = []
    gsel = []
    for g in range(N_GROUPS):
        chosen = None
        for g2 in range(N_GROUPS):
            if g2 == g:
                continue
            c = (tg[g] > tg[g2]) if g2 < g else (tg[g] >= tg[g2])
            chosen = c if chosen is None else jnp.logical_and(chosen, c)
        for a in range(EPG):
            rank = jnp.zeros_like(x[0])
            for b in range(EPG):
                if b == a:
                    continue
                ahead = (x[EPG * g + b] >= x[EPG * g + a]) if b < a else (x[EPG * g + b] > x[EPG * g + a])
                rank = rank + jnp.where(ahead, 1.0, 0.0)
            picked = jnp.logical_and(chosen, rank < 1.5)
            rows.append(jnp.where(picked, s[EPG * g + a:EPG * g + a + 1, :], 0.0))
        gsel.append(jnp.where(chosen, 1.0, 0.0))
    comb = jnp.concatenate(rows, axis=0)
    return comb * (1.0 / jnp.sum(comb, axis=0, keepdims=True)), jnp.concatenate(gsel, axis=0)


def _router_kernel(h_ref, mod_ref, rwt_ref, rb_ref, u_ref, rt_ref, *, nB, tile0):
    b = pl.program_id(0)
    i = pl.program_id(1)
    row = jnp.where(i + tile0 == 0, nB, b)
    u = h_ref[...] * (1.0 + _mod_row(mod_ref, row, 4)) + _mod_row(mod_ref, row, 3)
    u_ref[...] = u.astype(MM)
    logits_t = lax.dot_general(rwt_ref[...], u, (((1,), (1,)), ((), ())), precision=HIGHEST,
                               preferred_element_type=F32)
    comb_t, gsel_t = _route(logits_t, rb_ref[...])
    pad = jnp.zeros((MOE_RT_ROWS - N_EXPERTS - N_GROUPS, TM), F32)
    rt_ref[...] = jnp.concatenate([comb_t, gsel_t, pad], axis=0)


def _router_call(h, mod, rwt, rb, nB, tile0):
    B, S, _ = h.shape
    R = mod.shape[0]
    nt = S // TM - tile0
    full = lambda shape: pl.BlockSpec(shape, lambda b, i: (0,) * len(shape))
    return pl.pallas_call(
        functools.partial(_router_kernel, nB=nB, tile0=tile0),
        grid=(B, nt),
        in_specs=[pl.BlockSpec((None, TM, D), lambda b, i: (b, i + tile0, 0)),
                  full((R, 6 * D)), full((N_EXPERTS, D)), full((N_EXPERTS, 1))],
        out_specs=[pl.BlockSpec((None, TM, D), lambda b, i: (b, i, 0)),
                   pl.BlockSpec((MOE_RT_ROWS, TM), lambda b, i: (0, b * nt + i))],
        out_shape=[jax.ShapeDtypeStruct((B, nt * TM, D), MM),
                   jax.ShapeDtypeStruct((MOE_RT_ROWS, B * nt * TM), F32)],
        compiler_params=_params(("parallel", "arbitrary")),
        name="moe_router",
    )(h, mod, rwt, rb)


def _expert_kernel(cnt_ref, u_ref, rt_ref, wg_ref, wu_ref, wd_ref, f_ref, xs_scr, cs_scr, fs_scr, pos_scr,
                   *, TE, NP, nt):
    t = pl.program_id(0)
    g = pl.program_id(1)
    lane = lax.broadcasted_iota(jnp.int32, (MOE_BLK, 128), 1)
    for hf in range(NP):
        base = hf * TE
        n = [cnt_ref[gg * nt + t * NP + hf] for gg in range(N_GROUPS)]
        off = [jnp.int32(0)]
        for gg in range(N_GROUPS - 1):
            off.append(off[-1] + n[gg])

        @pl.when(g == 0)
        def _(base=base, off=off):
            rt = rt_ref[:, base:base + TE]
            gs = rt[N_EXPERTS:N_EXPERTS + 8, :]
            r_i = lax.broadcasted_iota(jnp.int32, (TE, TE), 0)
            c_i = lax.broadcasted_iota(jnp.int32, (TE, TE), 1)
            before = jnp.where(r_i < c_i, 1.0, 0.0).astype(MM)
            cnt_before = jnp.dot(gs.astype(MM), before, preferred_element_type=F32)
            pos = jnp.zeros((1, TE), F32)
            for gg in range(N_GROUPS):
                pos = pos + gs[gg:gg + 1, :] * (off[gg].astype(F32) + cnt_before[gg:gg + 1, :])
            perm = jnp.where(r_i.astype(F32) == pos, 1.0, 0.0).astype(MM)
            xs_scr[base:base + TE, :] = jnp.dot(perm, u_ref[base:base + TE, :],
                                                preferred_element_type=F32).astype(MM)
            rt_pad = jnp.concatenate([rt, jnp.zeros((128 - MOE_RT_ROWS, TE), F32)], axis=0)
            hi, lo = _split(rt_pad)
            cs_scr[base:base + TE, :] = _dot_nt(perm, hi) + _dot_nt(perm, lo)
            pos_scr[base:base + TE, :] = jnp.broadcast_to(pos, (128, TE)).T
            fs_scr[base:base + TE, :] = jnp.zeros((TE, D), F32)

        lo_g = jnp.int32(0)
        n_g = jnp.int32(0)
        for gg in range(N_GROUPS):
            lo_g = jnp.where(g == gg, off[gg], lo_g)
            n_g = jnp.where(g == gg, n[gg], n_g)
        hi_g = lo_g + n_g
        for rb in range(TE // MOE_BLK):
            @pl.when(jnp.logical_and(lo_g < MOE_BLK * (rb + 1), hi_g > MOE_BLK * rb))
            def _(r0=base + rb * MOE_BLK):
                rows = slice(r0, r0 + MOE_BLK)
                xb = xs_scr[rows, :]
                c = cs_scr[rows, :]
                hg = [jnp.dot(xb, wg_ref[e], preferred_element_type=F32) for e in range(EPG)]
                hu = [jnp.dot(xb, wu_ref[e], preferred_element_type=F32) for e in range(EPG)]
                col = [jnp.sum(jnp.where(lane == EPG * g + e, c, 0.0), axis=-1, keepdims=True) for e in range(EPG)]
                he = jnp.concatenate([(hg[e] * _sigmoid(hg[e]) * hu[e] * col[e]).astype(MM) for e in range(EPG)],
                                     axis=1)
                fs_scr[rows, :] += jnp.dot(he, wd_ref[...], preferred_element_type=F32)

        @pl.when(g == N_GROUPS - 1)
        def _(base=base):
            c_i = lax.broadcasted_iota(jnp.int32, (TE, TE), 1)
            pos_t = jnp.tile(pos_scr[base:base + TE, :], (1, TE // 128))
            unperm = jnp.where(c_i.astype(F32) == pos_t, 1.0, 0.0).astype(MM)
            f_ref[base:base + TE, :] = jnp.dot(unperm, fs_scr[base:base + TE, :].astype(MM),
                                               preferred_element_type=F32).astype(f_ref.dtype)


def _expert_call(u, rt, wg, wu, wd):
    N = u.shape[0]
    TE = MOE_TE if N % MOE_TE == 0 else TM
    nt = N // TE
    NP = MOE_NP if nt % MOE_NP == 0 else 1
    cnt = jnp.sum(rt[N_EXPERTS:N_EXPERTS + N_GROUPS].reshape(N_GROUPS, nt, TE), axis=-1)
    cnt = cnt.astype(jnp.int32).reshape(N_GROUPS * nt)
    gw = EPG * D_EXPERT
    grid_spec = pltpu.PrefetchScalarGridSpec(
        num_scalar_prefetch=1,
        grid=(nt // NP, N_GROUPS),
        in_specs=[pl.BlockSpec((NP * TE, D), lambda t, g, c: (t, 0)),
                  pl.BlockSpec((MOE_RT_ROWS, NP * TE), lambda t, g, c: (0, t)),
                  pl.BlockSpec((None, EPG, D, D_EXPERT), lambda t, g, c: (g, 0, 0, 0)),
                  pl.BlockSpec((None, EPG, D, D_EXPERT), lambda t, g, c: (g, 0, 0, 0)),
                  pl.BlockSpec((None, gw, D), lambda t, g, c: (g, 0, 0))],
        out_specs=pl.BlockSpec((NP * TE, D), lambda t, g, c: (t, 0)),
        scratch_shapes=[pltpu.VMEM((NP * TE, D), MM), pltpu.VMEM((NP * TE, 128), F32),
                        pltpu.VMEM((NP * TE, D), F32), pltpu.VMEM((NP * TE, 128), F32)])
    return pl.pallas_call(
        functools.partial(_expert_kernel, TE=TE, NP=NP, nt=nt),
        grid_spec=grid_spec,
        out_shape=jax.ShapeDtypeStruct((N, D), MM),
        compiler_params=_params(("parallel", "arbitrary")),
        name="moe_experts",
    )(cnt, u, rt, wg, wu, wd)


def _moe_ln_kernel(h_ref, mod_ref, f_ref, lng_ref, lnb_ref, o_ref, *, nB, tile0):
    b = pl.program_id(0)
    i = pl.program_id(1)
    row = jnp.where(i + tile0 == 0, nB, b)
    _res_ln(h_ref, mod_ref, row, 5, f_ref[...].astype(F32), lng_ref, lnb_ref, o_ref)


def _moe_ln_call(h, mod, f, lng, lnb, nB, tile0):
    B, S, _ = h.shape
    R = mod.shape[0]
    nt = S // TM - tile0
    full = lambda shape: pl.BlockSpec(shape, lambda b, i: (0,) * len(shape))
    return pl.pallas_call(
        functools.partial(_moe_ln_kernel, nB=nB, tile0=tile0),
        grid=(B, nt),
        in_specs=[pl.BlockSpec((None, TM, D), lambda b, i: (b, i + tile0, 0)),
                  full((R, 6 * D)),
                  pl.BlockSpec((None, TM, D), lambda b, i: (b, i, 0)),
                  full((1, D)), full((1, D))],
        out_specs=pl.BlockSpec((None, TM, D), lambda b, i: (b, i, 0)),
        out_shape=jax.ShapeDtypeStruct((B, nt * TM, D), F32),
        compiler_params=_params(("parallel", "arbitrary")),
        name="moe_res_ln",
    )(h, mod, f, lng, lnb)


def _moe_call(h, mod, rwt, rb, wg, wu, wd, lng, lnb, nB, latent_only, L):
    B = h.shape[0]
    tile0 = L // TM if latent_only else 0
    u, rt = _router_call(h, mod, rwt, rb, nB, tile0)
    f = _expert_call(u.reshape(-1, D), rt, wg.reshape(N_GROUPS, EPG, D, D_EXPERT),
                     wu.reshape(N_GROUPS, EPG, D, D_EXPERT), wd.reshape(N_GROUPS, EPG * D_EXPERT, D))
    return _moe_ln_call(h, mod, f.reshape(B, -1, D), lng, lnb, nB, tile0)


def kernel(x, c, ctx, c_ctx, ada_w, ada_b, ln_g, ln_b, even_w_in, even_w_out, shift_w, na_rpb, rw_w0, rw_w_up, rw_a0, rw_a_up, rw_g_up, rw_k_k, rw_k_a, rw_r_k, rw_gn_g, rw_gn_b, odd_w_in, odd_w_out, da_lq1, da_lk1, da_lq2, da_lk2, da_subln_g, router_w, router_b, exp_w_gate, exp_w_up, exp_w_down):
    B, T, _ = x.shape
    L = ctx.shape[1]
    assert L == TM and T % TM == 0 and (T // GRID_W) >= 12
    rows = T // GRID_W
    R = -(-(B + 1) // 8) * 8

    cv = jnp.zeros((R, D), F32).at[:B].set(c).at[B].set(c_ctx)
    mods = _ada_call(cv, ada_w, ada_b)

    cos_t, sin_t = _rope_tables(L, T)
    rwt = router_w.T
    rb = router_b.reshape(N_EXPERTS, 1)
    zpad = jnp.zeros((2, 64, RW_W), F32)

    h = jnp.concatenate([ctx, x], axis=1)
    for l in range(DEPTH):
        mod = mods[l]
        i = l // 2
        lng = ln_g[l].reshape(2, 1, D)
        lnb = ln_b[l].reshape(2, 1, D)
        if l % 2 == 0:
            w_in = even_w_in[i].astype(MM)
            qkv, prw = _even_in_call(h, mod, w_in[:, :3 * NA_W], w_in[:, 3 * NA_W:], B)
            ona = _na_call(qkv, _na_bias(na_rpb[i], rows), L)
            yf, yr, bg = _rw_call(
                prw, shift_w[i], rw_w0[i].reshape(2, 1, RW_W),
                jnp.concatenate([rw_w_up[i], zpad], axis=1).astype(MM), rw_a0[i].reshape(2, 1, RW_W),
                jnp.concatenate([zpad, rw_a_up[i]], axis=1).astype(MM), rw_g_up[i].astype(MM),
                rw_k_k[i].reshape(1, RW_W), rw_k_a[i].reshape(1, RW_W), rw_r_k[i].reshape(1, RW_W), L)
            h = _even_out_call(h, mod, ona, yf, yr, bg, rw_gn_g[i].reshape(1, RW_W), rw_gn_b[i].reshape(1, RW_W),
                               even_w_out[i].astype(MM), lng[0], lnb[0], B)
        else:
            lam_init = 0.8 - 0.6 * math.exp(-0.3 * l)
            qkv = _odd_in_call(h, mod, odd_w_in[i].astype(MM), cos_t, sin_t, B)
            oda = _da_call(qkv, da_lq1[i].reshape(1, 64), da_lk1[i].reshape(1, 64), da_lq2[i].reshape(1, 64),
                           da_lk2[i].reshape(1, 64), da_subln_g[i].reshape(1, 128), L, lam_init)
            h = _odd_out_call(h, mod, oda, odd_w_out[i].astype(MM), lng[0], lnb[0], B)
        h = _moe_call(h, mod, rwt, rb, exp_w_gate[l].astype(MM), exp_w_up[l].astype(MM), exp_w_down[l].astype(MM),
                      lng[1], lnb[1], B, l == DEPTH - 1, L)
    return h
```

```python
import functools
import math

import numpy as np
import jax
import jax.numpy as jnp
from jax import lax
from jax.experimental import pallas as pl
from jax.experimental.pallas import tpu as pltpu

F32 = jnp.float32
MM = jnp.bfloat16
HIGHEST = lax.Precision.HIGHEST

D = 1024
DEPTH = 4
GRID_W = 64
NA_HEADS = 8
NA_W = 512
NA_WIN_R = 8
NA_WIN_C = 16
NA_QB = 128
NA_NB = 2
NA_KROWS = 9
NA_NLOC = NA_KROWS * GRID_W
RW_W = 512
RW_HEAD = 64
RW_IN = 1792
RW_GN_EPS = 64e-5
RW_CHUNK = 64
RW_GRP = 256
RW_NSUB = 2
RW_INV_BASE = 8
RW_INV_LEVELS = 3
RW_NMASK = 7 + RW_INV_LEVELS
DA_HEADS = 8
DA_W = 1024
ROPE_THETA = 10000.0
N_EXPERTS = 16
N_GROUPS = 4
EPG = 4
D_EXPERT = 512
MOE_RT_ROWS = 32
MOE_TE = 512
MOE_NP = 2
MOE_BLK = 128
ALPHA = (2 * DEPTH) ** 0.25
LN_EPS = 1e-5
TM = 256
DA_TQ = 256
DA_NH = 2
DA_TK = (768, 512, 256)
DA_SUB = 10
DA_QSCALE = 0.125 * math.log2(math.e)
NEG = -1e30
VMEM_LIMIT = 56 * 1024 * 1024


def _dot(a, b):
    return jnp.dot(a.astype(MM), b.astype(MM), preferred_element_type=F32)


def _dot_nt(a, b):
    return lax.dot_general(a.astype(MM), b.astype(MM), (((1,), (1,)), ((), ())), preferred_element_type=F32)


def _split(x):
    hi = x.astype(MM)
    lo = (x - hi.astype(F32)).astype(MM)
    return hi, lo


def _dot_split_lhs(x, w):
    hi, lo = _split(x)
    return jnp.dot(hi, w, preferred_element_type=F32) + jnp.dot(lo, w, preferred_element_type=F32)


def _dot_split_rhs(w, x):
    hi, lo = _split(x)
    return jnp.dot(w, hi, preferred_element_type=F32) + jnp.dot(w, lo, preferred_element_type=F32)


def _sigmoid(x):
    return 1.0 / (1.0 + jnp.exp(-x))


def _params(sem):
    return pltpu.CompilerParams(dimension_semantics=sem, vmem_limit_bytes=VMEM_LIMIT)


def _mod_row(mod_ref, row, j):
    return mod_ref[pl.ds(row, 1), j * D:(j + 1) * D]


def _layer_norm(z, g, b):
    mu = jnp.mean(z, axis=-1, keepdims=True)
    zc = z - mu
    var = jnp.mean(zc * zc, axis=-1, keepdims=True)
    return zc * lax.rsqrt(var + LN_EPS) * g + b


def _ada_kernel(cv_ref, w_ref, b_ref, o_ref):
    x = cv_ref[...]
    x = x * _sigmoid(x)
    o_ref[...] = jnp.dot(x, w_ref[...], precision=HIGHEST, preferred_element_type=F32) + b_ref[...]


def _ada_call(cv, ada_w, ada_b):
    R = cv.shape[0]
    tn = 1536
    return pl.pallas_call(
        _ada_kernel,
        grid=(DEPTH, 6 * D // tn),
        in_specs=[pl.BlockSpec((R, D), lambda l, j: (0, 0)),
                  pl.BlockSpec((None, D, tn), lambda l, j: (l, 0, j)),
                  pl.BlockSpec((None, 1, tn), lambda l, j: (l, 0, j))],
        out_specs=pl.BlockSpec((None, R, tn), lambda l, j: (l, 0, j)),
        out_shape=jax.ShapeDtypeStruct((DEPTH, R, 6 * D), F32),
        compiler_params=_params(("arbitrary", "arbitrary")),
        name="ada_mod",
    )(cv, ada_w, ada_b.reshape(DEPTH, 1, 6 * D))


def _even_in_kernel(h_ref, mod_ref, wna_ref, wrw_ref, qkv_ref, prw_ref, *, nB):
    b = pl.program_id(0)
    i = pl.program_id(1)
    row = jnp.where(i == 0, nB, b)
    u = (h_ref[...] * (1.0 + _mod_row(mod_ref, row, 1)) + _mod_row(mod_ref, row, 0)).astype(MM)
    res = jnp.dot(u, wna_ref[...], preferred_element_type=F32)
    qkv_ref[:, 0:NA_W] = (res[:, 0:NA_W] * 0.125).astype(qkv_ref.dtype)
    qkv_ref[:, NA_W:] = res[:, NA_W:].astype(qkv_ref.dtype)
    prw_ref[...] = jnp.dot(u, wrw_ref[...], preferred_element_type=F32)


def _even_in_call(h, mod, w_na, w_rw, nB):
    B, S, _ = h.shape
    R = mod.shape[0]
    return pl.pallas_call(
        functools.partial(_even_in_kernel, nB=nB),
        grid=(B, S // TM),
        in_specs=[pl.BlockSpec((None, TM, D), lambda b, i: (b, i, 0)),
                  pl.BlockSpec((R, 6 * D), lambda b, i: (0, 0)),
                  pl.BlockSpec((D, 3 * NA_W), lambda b, i: (0, 0)),
                  pl.BlockSpec((D, RW_IN), lambda b, i: (0, 0))],
        out_specs=[pl.BlockSpec((None, TM, 3 * NA_W), lambda b, i: (b, i, 0)),
                   pl.BlockSpec((None, TM, RW_IN), lambda b, i: (b, i, 0))],
        out_shape=[jax.ShapeDtypeStruct((B, S, 3 * NA_W), MM),
                   jax.ShapeDtypeStruct((B, S, RW_IN), F32)],
        compiler_params=_params(("parallel", "arbitrary")),
        name="even_in_proj",
    )(h, mod, w_na, w_rw)


def _na_bias(rpb, rows):
    cq = np.arange(GRID_W)[:, None]
    ck = np.arange(GRID_W)[None, :]
    cs = np.clip(cq - NA_WIN_C // 2, 0, GRID_W - NA_WIN_C)
    ok_c = (ck >= cs) & (ck < cs + NA_WIN_C)
    dc = np.clip(ck - cq + NA_WIN_C - 1, 0, 2 * NA_WIN_C - 2)
    onehot = np.zeros((2 * NA_WIN_C - 1, GRID_W * GRID_W), np.float32)
    onehot[dc.ravel(), np.arange(GRID_W * GRID_W)] = 1.0
    rc = jnp.einsum('hrd,dn->hrn', rpb, jnp.asarray(onehot), precision=HIGHEST)
    rc = rc.reshape(rpb.shape[0], 2 * NA_WIN_R - 1, GRID_W, GRID_W)
    cases = []
    for r0 in (0, 2, 4, rows - 4, rows - 2):
        ws = int(np.clip(r0 - NA_WIN_R // 2, 0, rows - NA_KROWS))
        qrows = []
        for qi in range(NA_QB // GRID_W):
            rq = r0 + qi
            rs = int(np.clip(rq - NA_WIN_R // 2, 0, rows - NA_WIN_R))
            pieces = []
            for j in range(NA_KROWS):
                rk = ws + j
                if rs <= rk < rs + NA_WIN_R:
                    pieces.append(jnp.where(ok_c[None], rc[:, rk - rq + NA_WIN_R - 1], NEG))
                else:
                    pieces.append(jnp.full((rpb.shape[0], GRID_W, GRID_W), NEG, F32))
            qrows.append(jnp.concatenate(pieces, axis=-1))
        cases.append(jnp.concatenate(qrows, axis=-2))
    return jnp.stack(cases, axis=0)


def _na_kernel(q_ref, k_ref, v_ref, *rest, L, rows):
    bias_refs, o_ref = rest[:NA_NB], rest[NA_NB]
    i = pl.program_id(2)
    nq = L // (NA_NB * NA_QB)
    lane = lax.broadcasted_iota(jnp.int32, (NA_QB, 128), 1)
    lo = lane < 64
    kc = k_ref[0:L, :]
    vc = v_ref[0:L, :]
    ch = [(n, hh) for n in range(NA_NB) for hh in range(2)]
    qs = [q_ref[n * NA_QB:(n + 1) * NA_QB, :] for n in range(NA_NB)]
    qm = {(n, hh): jnp.where(lo if hh == 0 else jnp.logical_not(lo), qs[n], jnp.zeros_like(qs[n])) for n, hh in ch}

    def store(outs):
        for n in range(NA_NB):
            o_ref[n * NA_QB:(n + 1) * NA_QB, :] = jnp.where(lo, outs[(n, 0)], outs[(n, 1)]).astype(o_ref.dtype)

    @pl.when(i < nq)
    def _():
        s = {c: _dot_nt(qm[c], kc) for c in ch}
        m = {c: jnp.max(s[c], axis=-1, keepdims=True) for c in ch}
        p = {c: jnp.exp(s[c] - m[c]) for c in ch}
        l = {c: jnp.sum(p[c], axis=-1, keepdims=True) for c in ch}
        store({c: _dot(p[c], vc) * (1.0 / l[c]) for c in ch})

    @pl.when(i >= nq)
    def _():
        kl, vl = [], []
        for n in range(NA_NB):
            r0 = 2 * ((i - nq) * NA_NB + n)
            ws = jnp.clip(r0 - NA_WIN_R // 2, 0, rows - NA_KROWS)
            start = pl.multiple_of(L + GRID_W * ws, GRID_W)
            kl.append(k_ref[pl.ds(start, NA_NLOC), :])
            vl.append(v_ref[pl.ds(start, NA_NLOC), :])
        s_loc = {c: _dot_nt(qm[c], kl[c[0]]) + bias_refs[c[0]][c[1]] for c in ch}
        s_ctx = {c: _dot_nt(qm[c], kc) for c in ch}
        m = {c: jnp.maximum(jnp.max(s_loc[c], axis=-1, keepdims=True), jnp.max(s_ctx[c], axis=-1, keepdims=True))
             for c in ch}
        p_loc = {c: jnp.exp(s_loc[c] - m[c]) for c in ch}
        p_ctx = {c: jnp.exp(s_ctx[c] - m[c]) for c in ch}
        l = {c: jnp.sum(p_loc[c], axis=-1, keepdims=True) + jnp.sum(p_ctx[c], axis=-1, keepdims=True) for c in ch}
        store({c: (_dot(p_loc[c], vl[c[0]]) + _dot(p_ctx[c], vc)) * (1.0 / l[c]) for c in ch})


def _na_call(qkv, bias, L):
    B, S, _ = qkv.shape
    rows = (S - L) // GRID_W
    tq = NA_NB * NA_QB
    assert L % tq == 0 and (S - L) % tq == 0
    nq = L // tq
    nhp = NA_HEADS // 2

    def bias_idx(n):
        def idx(b, hp, i):
            r0 = 2 * ((i - nq) * NA_NB + n)
            c = jnp.where(r0 == 0, 0,
                          jnp.where(r0 == 2, 1, jnp.where(r0 == rows - 4, 3, jnp.where(r0 == rows - 2, 4, 2))))
            return (jnp.where(i < nq, 2, c), hp, 0, 0)
        return idx

    return pl.pallas_call(
        functools.partial(_na_kernel, L=L, rows=rows),
        grid=(B, nhp, S // tq),
        in_specs=[pl.BlockSpec((None, tq, 128), lambda b, hp, i: (b, i, hp)),
                  pl.BlockSpec((None, S, 128), lambda b, hp, i: (b, 0, nhp + hp)),
                  pl.BlockSpec((None, S, 128), lambda b, hp, i: (b, 0, 2 * nhp + hp))]
                 + [pl.BlockSpec((None, 2, NA_QB, NA_NLOC), bias_idx(n)) for n in range(NA_NB)],
        out_specs=pl.BlockSpec((None, tq, 128), lambda b, hp, i: (b, i, hp)),
        out_shape=jax.ShapeDtypeStruct((B, S, NA_W), MM),
        compiler_params=_params(("parallel", "parallel", "arbitrary")),
        name="na_attention",
    )(qkv, qkv, qkv, *([bias] * NA_NB))


def _rw_conv(p_ref, prev_ref, next_ref, c, conv_ref, nL, NC):
    p = p_ref[...]
    at_start = jnp.logical_or(c == 0, c == nL)
    at_end = jnp.logical_or(c == nL - 1, c == NC - 1)
    prow = jnp.where(at_start, 0.0, prev_ref[7:8, :])
    nrow = jnp.where(at_end, 0.0, next_ref[0:1, :])
    rid = lax.broadcasted_iota(jnp.int32, p.shape, 0)
    last = p.shape[0] - 1
    pm = jnp.where(rid == 0, prow, pltpu.roll(p, 1, axis=0))
    pp = jnp.where(rid == last, nrow, pltpu.roll(p, last, axis=0))
    return pm * conv_ref[0:1, :] + p * conv_ref[1:2, :] + pp * conv_ref[2:3, :]


def _rw_expand(z, bd):
    return jnp.where(bd, jnp.concatenate([z, z, z, z], axis=0), 0.0)


def _rw_prep(pc, d, w0_ref, wup_ref, a0_ref, aup_ref, kk_ref, ka_ref, ones_ref, cum_ref):
    r = pc[:, 0:RW_W]
    k = pc[:, RW_W:2 * RW_W]
    v = pc[:, 2 * RW_W:3 * RW_W]
    xwa = pc[:, 3 * RW_W:3 * RW_W + 128]
    z = w0_ref[d] + _dot(jnp.tanh(xwa), wup_ref[d])
    lw = -math.exp(-0.5) * _sigmoid(z)
    a = _sigmoid(a0_ref[d] + _dot(xwa, aup_ref[d]))
    kk = k * kk_ref[...]
    n2 = _dot_split_lhs(kk * kk, ones_ref[...])
    kk = kk * (1.0 / jnp.maximum(jnp.sqrt(n2), 1e-12))
    keff = k * (1.0 + (a - 1.0) * ka_ref[...])
    av = -kk
    bv = kk * a
    lc = _dot_split_rhs(cum_ref[d], lw)
    ltot = lc[RW_CHUNK - 1:RW_CHUNK, :] if d == 0 else lc[0:1, :]
    at = av * jnp.exp(lc - lw)
    rt = r * jnp.exp(lc)
    ginv = jnp.exp(-lc)
    bt = bv * ginv
    kt = keff * ginv
    grest = jnp.exp(ltot - lc)
    bh = bv * grest
    kh = keff * grest
    gc = jnp.exp(ltot)

    return dict(at=at, rt=rt, bt=bt, kt=kt, bh=bh, kh=kh, v=v, gc=gc, keff=keff, r=r)


def _rw_chains(preps, mask_ref, g_scr):
    bd = mask_ref[0] > 0.5
    bdm = mask_ref[0].astype(MM)
    eye = mask_ref[1]
    ngrp = RW_W // RW_GRP
    order = {0: list(range(RW_NSUB)), 1: list(range(RW_NSUB - 1, -1, -1))}
    chains = [(d, g, order[d][n]) for n in range(RW_NSUB) for d in range(2) for g in range(ngrp)]
    ex = {}
    for c in chains:
        d, g, slot = c
        sl = slice(g * RW_GRP, (g + 1) * RW_GRP)
        p = preps[(d, slot)]
        e = {}
        for n in ("at", "rt", "bt", "kt", "bh", "kh"):
            z = p[n][:, sl].astype(MM)
            e[n] = jnp.concatenate([z, z, z, z], axis=0) * bdm
        e["vt"] = _rw_expand(p["v"][:, sl], bd).T.astype(MM)
        e["gc"] = p["gc"][:, sl]
        ex[c] = e
    xb = {c: _dot_nt(ex[c]["bt"], jnp.concatenate([ex[c]["at"], ex[c]["rt"]], axis=0)) for c in chains}
    xk = {c: _dot_nt(ex[c]["kt"], jnp.concatenate([ex[c]["at"], ex[c]["rt"]], axis=0)) for c in chains}
    xab, xrb, xak, xrk = {}, {}, {}, {}
    for c in chains:
        ms = mask_ref[2 + 2 * c[0]]
        mi = mask_ref[3 + 2 * c[0]]
        xab[c] = xb[c][:, 0:RW_GRP] * ms
        xrb[c] = (xb[c][:, RW_GRP:] * mi).astype(MM)
        xak[c] = (xk[c][:, 0:RW_GRP] * ms).astype(MM)
        xrk[c] = (xk[c][:, RW_GRP:] * mi).astype(MM)
    xp = {c: xab[c] * mask_ref[6] for c in chains}
    tt = {c: eye + xp[c] for c in chains}
    for _ in range(RW_INV_BASE.bit_length() - 2):
        xp = {c: _dot(xp[c], xp[c]) for c in chains}
        tt = {c: tt[c] + _dot(tt[c], xp[c]) for c in chains}
    for lvl in range(RW_INV_LEVELS):
        tc = {c: _dot(tt[c], xab[c] * mask_ref[7 + lvl]) for c in chains}
        tt = {c: tt[c] + _dot(tc[c], tt[c]) for c in chains}
    ys = {}
    for n in range(RW_NSUB):
        cur = [(d, g, order[d][n]) for d in range(2) for g in range(ngrp)]
        gs = {c: g_scr[c[0], c[1]] for c in cur}
        gsb = {c: gs[c].astype(MM) for c in cur}
        w1 = {c: _dot_nt(gsb[c], ex[c]["at"]) + _dot(ex[c]["vt"], xak[c]) for c in cur}
        ut = {c: _dot(w1[c], tt[c]).astype(MM) for c in cur}
        yt = {c: _dot_nt(gsb[c], ex[c]["rt"]) + _dot(ut[c], xrb[c]) + _dot(ex[c]["vt"], xrk[c]) for c in cur}
        for c in cur:
            g_scr[c[0], c[1]] = gs[c] * ex[c]["gc"] + _dot(ut[c], ex[c]["bh"]) + _dot(ex[c]["vt"], ex[c]["kh"])
        for c in cur:
            ybd = yt[c].T
            ys[c] = ybd[0:64] + ybd[64:128] + ybd[128:192] + ybd[192:256]
    return [jnp.concatenate([jnp.concatenate([ys[(d, g, slot)] for g in range(ngrp)], axis=1)
                             for slot in range(RW_NSUB)], axis=0) for d in range(2)]


def _rw_kernel(pf_ref, pfp_ref, pfn_ref, pr_ref, prp_ref, prn_ref, conv_ref, w0_ref, wup_ref, a0_ref, aup_ref,
               gup_ref, kk_ref, ka_ref, rk_ref, ones_ref, cum_ref, mask_ref, yf_ref, yr_ref, bg_ref, g_scr,
               *, nL, NC):
    i = pl.program_id(1)

    @pl.when(i == 0)
    def _():
        g_scr[...] = jnp.zeros_like(g_scr)

    cf = i
    cr = jnp.where(i < nL, nL - 1 - i, NC - 1 - i + nL)
    args = (w0_ref, wup_ref, a0_ref, aup_ref, kk_ref, ka_ref, ones_ref, cum_ref)

    pcf = _rw_conv(pf_ref, pfp_ref, pfn_ref, cf, conv_ref, nL, NC)
    pcr = _rw_conv(pr_ref, prp_ref, prn_ref, cr, conv_ref, nL, NC)
    preps = {}
    for n in range(RW_NSUB):
        sf, sr = n, RW_NSUB - 1 - n
        preps[(0, sf)] = _rw_prep(pcf[sf * RW_CHUNK:(sf + 1) * RW_CHUNK], 0, *args)
        preps[(1, sr)] = _rw_prep(pcr[sr * RW_CHUNK:(sr + 1) * RW_CHUNK], 1, *args)
    xwa = pcf[:, 3 * RW_W:3 * RW_W + 128]
    a_r = _sigmoid(a0_ref[1] + _dot(xwa, aup_ref[1]))
    keff_r = pcf[:, RW_W:2 * RW_W] * (1.0 + (a_r - 1.0) * ka_ref[...])
    keff_f = jnp.concatenate([preps[(0, s)]["keff"] for s in range(RW_NSUB)], axis=0)
    rv = pcf[:, 0:RW_W] * rk_ref[...]
    bsum = _dot_split_lhs(rv * (keff_f + keff_r), ones_ref[...]) * pcf[:, 2 * RW_W:3 * RW_W]
    bg_ref[:, 0:RW_W] = bsum
    bg_ref[:, RW_W:] = _dot(_sigmoid(pcf[:, 3 * RW_W + 128:]), gup_ref[...])

    y_f, y_r = _rw_chains(preps, mask_ref, g_scr)
    yf_ref[...] = y_f
    yr_ref[...] = y_r


def _rw_masks():
    n = 4 * RW_CHUNK
    idx = np.arange(n)
    hd = idx // RW_CHUNK
    t = idx % RW_CHUNK
    bd = hd[:, None] == hd[None, :]
    m = np.zeros((RW_NMASK, n, n), np.float32)
    m[0] = bd
    m[1] = np.eye(n)
    m[2] = bd & (t[:, None] < t[None, :])
    m[3] = bd & (t[:, None] <= t[None, :])
    m[4] = bd & (t[:, None] > t[None, :])
    m[5] = bd & (t[:, None] >= t[None, :])
    same = lambda w: bd & ((t[:, None] // w) == (t[None, :] // w))
    m[6] = same(RW_INV_BASE)
    for lvl in range(RW_INV_LEVELS):
        w = RW_INV_BASE << lvl
        m[7 + lvl] = same(2 * w) & ~same(w)
    tt = np.arange(RW_CHUNK)
    cum = np.stack([tt[None, :] <= tt[:, None], tt[None, :] >= tt[:, None]]).astype(np.float32)
    ch = np.arange(RW_W) // RW_HEAD
    ones = (ch[:, None] == ch[None, :]).astype(np.float32)
    return m, cum, ones


def _rw_call(prw, conv_w, w0, wup_pad, a0, aup_pad, gup, kk, ka, rk, L):
    B, S, _ = prw.shape
    rb = RW_NSUB * RW_CHUNK
    assert L % rb == 0 and S % rb == 0
    NC = S // rb
    nL = L // rb
    n8 = S // 8
    r8 = rb // 8
    m, cum, ones = _rw_masks()

    def cr_of(i):
        return jnp.where(i < nL, nL - 1 - i, NC - 1 - i + nL)

    full = lambda shape: pl.BlockSpec(shape, lambda b, i: (0,) * len(shape))
    in_specs = [
        pl.BlockSpec((None, rb, RW_IN), lambda b, i: (b, i, 0)),
        pl.BlockSpec((None, 8, RW_IN), lambda b, i: (b, jnp.maximum(i * r8 - 1, 0), 0)),
        pl.BlockSpec((None, 8, RW_IN), lambda b, i: (b, jnp.minimum(i * r8 + r8, n8 - 1), 0)),
        pl.BlockSpec((None, rb, RW_IN), lambda b, i: (b, cr_of(i), 0)),
        pl.BlockSpec((None, 8, RW_IN), lambda b, i: (b, jnp.maximum(cr_of(i) * r8 - 1, 0), 0)),
        pl.BlockSpec((None, 8, RW_IN), lambda b, i: (b, jnp.minimum(cr_of(i) * r8 + r8, n8 - 1), 0)),
        full((3, RW_IN)), full((2, 1, RW_W)), full((2, 128, RW_W)), full((2, 1, RW_W)), full((2, 128, RW_W)),
        full((128, RW_W)), full((1, RW_W)), full((1, RW_W)), full((1, RW_W)),
        full((RW_W, RW_W)), full((2, RW_CHUNK, RW_CHUNK)), full((RW_NMASK, 4 * RW_CHUNK, 4 * RW_CHUNK)),
    ]
    out_specs = [
        pl.BlockSpec((None, rb, RW_W), lambda b, i: (b, i, 0)),
        pl.BlockSpec((None, rb, RW_W), lambda b, i: (b, cr_of(i), 0)),
        pl.BlockSpec((None, rb, 2 * RW_W), lambda b, i: (b, i, 0)),
    ]
    return pl.pallas_call(
        functools.partial(_rw_kernel, nL=nL, NC=NC),
        grid=(B, NC),
        in_specs=in_specs,
        out_specs=out_specs,
        out_shape=[jax.ShapeDtypeStruct((B, S, RW_W), F32),
                   jax.ShapeDtypeStruct((B, S, RW_W), F32),
                   jax.ShapeDtypeStruct((B, S, 2 * RW_W), F32)],
        scratch_shapes=[pltpu.VMEM((2, RW_W // RW_GRP, RW_GRP, RW_GRP), F32)],
        compiler_params=_params(("parallel", "arbitrary")),
        name="rwkv7_chunked",
    )(prw, prw, prw, prw, prw, prw, conv_w, w0, wup_pad, a0, aup_pad, gup, kk, ka, rk,
      jnp.asarray(ones, MM), jnp.asarray(cum, MM), jnp.asarray(m))


def _res_ln(h_ref, mod_ref, row, j, y, lng_ref, lnb_ref, o_ref):
    z = ALPHA * h_ref[...] + _mod_row(mod_ref, row, j) * y
    o_ref[...] = _layer_norm(z, lng_ref[...], lnb_ref[...])


def _even_out_kernel(h_ref, mod_ref, ona_ref, yf_ref, yr_ref, bg_ref, ones_ref, gng_ref, gnb_ref, wo_ref,
                     lng_ref, lnb_ref, o_ref, *, nB):
    b = pl.program_id(0)
    i = pl.program_id(1)
    row = jnp.where(i == 0, nB, b)
    y = yf_ref[...] + yr_ref[...]
    mu = _dot_split_lhs(y, ones_ref[...]) * (1.0 / RW_HEAD)
    yc = y - mu
    var = _dot_split_lhs(yc * yc, ones_ref[...]) * (1.0 / RW_HEAD)
    yn = yc * lax.rsqrt(var + RW_GN_EPS) * gng_ref[...] + gnb_ref[...]
    orw = (yn + bg_ref[:, 0:RW_W]) * bg_ref[:, RW_W:]
    yy = (jnp.dot(ona_ref[...], wo_ref[0:NA_W, :], preferred_element_type=F32)
          + jnp.dot(orw.astype(MM), wo_ref[NA_W:, :], preferred_element_type=F32))
    _res_ln(h_ref, mod_ref, row, 2, yy, lng_ref, lnb_ref, o_ref)


def _even_out_call(h, mod, ona, yf, yr, bg, gng, gnb, wo, lng, lnb, nB):
    B, S, _ = h.shape
    R = mod.shape[0]
    _, _, ones = _rw_masks()
    tile = lambda w: pl.BlockSpec((None, TM, w), lambda b, i: (b, i, 0))
    full = lambda shape: pl.BlockSpec(shape, lambda b, i: (0,) * len(shape))
    return pl.pallas_call(
        functools.partial(_even_out_kernel, nB=nB),
        grid=(B, S // TM),
        in_specs=[tile(D), full((R, 6 * D)), tile(NA_W), tile(RW_W), tile(RW_W), tile(2 * RW_W),
                  full((RW_W, RW_W)), full((1, RW_W)), full((1, RW_W)), full((D, D)), full((1, D)), full((1, D))],
        out_specs=tile(D),
        out_shape=jax.ShapeDtypeStruct((B, S, D), F32),
        compiler_params=_params(("parallel", "arbitrary")),
        name="even_out_proj_ln",
    )(h, mod, ona, yf, yr, bg, jnp.asarray(ones, MM), gng, gnb, wo, lng, lnb)


def _odd_out_kernel(h_ref, mod_ref, oda_ref, wo_ref, lng_ref, lnb_ref, o_ref, *, nB):
    b = pl.program_id(0)
    i = pl.program_id(1)
    row = jnp.where(i == 0, nB, b)
    yy = jnp.dot(oda_ref[...], wo_ref[...], preferred_element_type=F32)
    _res_ln(h_ref, mod_ref, row, 2, yy, lng_ref, lnb_ref, o_ref)


def _odd_out_call(h, mod, oda, wo, lng, lnb, nB):
    B, S, _ = h.shape
    R = mod.shape[0]
    tile = lambda w: pl.BlockSpec((None, TM, w), lambda b, i: (b, i, 0))
    full = lambda shape: pl.BlockSpec(shape, lambda b, i: (0,) * len(shape))
    return pl.pallas_call(
        functools.partial(_odd_out_kernel, nB=nB),
        grid=(B, S // TM),
        in_specs=[tile(D), full((R, 6 * D)), tile(DA_W), full((DA_W, D)), full((1, D)), full((1, D))],
        out_specs=tile(D),
        out_shape=jax.ShapeDtypeStruct((B, S, D), F32),
        compiler_params=_params(("parallel", "arbitrary")),
        name="odd_out_proj_ln",
    )(h, mod, oda, wo, lng, lnb)


def _odd_in_kernel(h_ref, mod_ref, w_ref, cos_ref, sin_ref, o_ref, *, nB):
    b = pl.program_id(0)
    i = pl.program_id(1)
    row = jnp.where(i == 0, nB, b)
    u = (h_ref[...] * (1.0 + _mod_row(mod_ref, row, 1)) + _mod_row(mod_ref, row, 0)).astype(MM)
    res = jnp.dot(u, w_ref[...], preferred_element_type=F32)
    cs = cos_ref[...]
    sn = sin_ref[...]
    lane = lax.broadcasted_iota(jnp.int32, cs.shape, 1)
    first = (lane % 32) < 16
    for j in range(2 * DA_W // 128):
        zj = res[:, j * 128:(j + 1) * 128]
        sw = jnp.where(first, pltpu.roll(zj, 112, axis=1), pltpu.roll(zj, 16, axis=1))
        rot = zj * cs + sw * sn
        if j < DA_W // 128:
            rot = rot * DA_QSCALE
        o_ref[:, j * 128:(j + 1) * 128] = rot.astype(o_ref.dtype)
    o_ref[:, 2 * DA_W:] = res[:, 2 * DA_W:].astype(o_ref.dtype)


def _odd_in_call(h, mod, w, cos_t, sin_t, nB):
    B, S, _ = h.shape
    R = mod.shape[0]
    return pl.pallas_call(
        functools.partial(_odd_in_kernel, nB=nB),
        grid=(B, S // TM),
        in_specs=[pl.BlockSpec((None, TM, D), lambda b, i: (b, i, 0)),
                  pl.BlockSpec((R, 6 * D), lambda b, i: (0, 0)),
                  pl.BlockSpec((D, 3 * DA_W), lambda b, i: (0, 0)),
                  pl.BlockSpec((TM, 128), lambda b, i: (i, 0)),
                  pl.BlockSpec((TM, 128), lambda b, i: (i, 0))],
        out_specs=pl.BlockSpec((None, TM, 3 * DA_W), lambda b, i: (b, i, 0)),
        out_shape=jax.ShapeDtypeStruct((B, S, 3 * DA_W), MM),
        compiler_params=_params(("parallel", "arbitrary")),
        name="odd_in_proj_rope",
    )(h, mod, w, cos_t, sin_t)


def _rope_tables(L, T):
    nf = 16
    inv = ROPE_THETA ** (-jnp.arange(nf, dtype=F32) / nf)
    t = jnp.arange(T)
    ang_r = (t // GRID_W).astype(F32)[:, None] * inv
    ang_c = (t % GRID_W).astype(F32)[:, None] * inv
    cos64 = jnp.concatenate([jnp.cos(ang_r), jnp.cos(ang_r), jnp.cos(ang_c), jnp.cos(ang_c)], -1)
    sin64 = jnp.concatenate([-jnp.sin(ang_r), jnp.sin(ang_r), -jnp.sin(ang_c), jnp.sin(ang_c)], -1)
    cos_t = jnp.concatenate([jnp.ones((L, 128), F32), jnp.tile(cos64, (1, 2))], 0)
    sin_t = jnp.concatenate([jnp.zeros((L, 128), F32), jnp.tile(sin64, (1, 2))], 0)
    return cos_t, sin_t


def _da_kernel(q_ref, k_ref, v_ref, lq1_ref, lk1_ref, lq2_ref, lk2_ref, sg_ref, o_ref, m_scr, acc_scr, s_scr, va_scr,
               *, L, tk, n_chunks, sub, lam_init):
    i = pl.program_id(2)
    n_rest = n_chunks - 1
    hs = range(DA_NH)
    hl = [slice(128 * hh, 128 * (hh + 1)) for hh in hs]

    @pl.when(i == 0)
    def _():
        for hh in hs:
            va_scr[hh, :, 0:128] = v_ref[:, hl[hh]]
            va_scr[hh, :, 128:256] = jnp.ones((va_scr.shape[1], 128), va_scr.dtype)

    lane = lax.broadcasted_iota(jnp.int32, (DA_TQ, 128), 1)
    lo = lane < 64
    qq = []
    for hh in hs:
        q = q_ref[:, hl[hh]]
        zero = jnp.zeros_like(q)
        qq.append(jnp.concatenate([jnp.where(lo, q, zero), jnp.where(lo, zero, q)], axis=0))

    def rows(j):
        if isinstance(j, int):
            return slice(j * tk, (j + 1) * tk)
        return pl.ds(pl.multiple_of(j * tk, tk), tk)

    def scores(hh, j):
        return _dot_nt(qq[hh], k_ref[rows(j), hl[hh]])

    def softmax_pv(hh, s, vb, first):
        m_cur = jnp.max(s, axis=-1, keepdims=True)
        if first:
            m_new = jnp.broadcast_to(m_cur, m_scr.shape[1:])
            acc_scr[hh] = _dot(jnp.exp2(s - m_cur), vb)
        else:
            m_old = m_scr[hh]
            m_new = jnp.maximum(m_old, m_cur)
            alpha = jnp.exp2(m_old - m_new)
            p = jnp.exp2(s - jnp.tile(m_new, (1, s.shape[1] // 128)))
            acc_scr[hh] = jnp.tile(alpha, (1, 2)) * acc_scr[hh] + _dot(p, vb)
        m_scr[hh] = m_new

    @pl.when(i == 0)
    def _():
        s = [_dot_nt(qq[hh], k_ref[0:L, hl[hh]]) for hh in hs]
        for hh in hs:
            softmax_pv(hh, s[hh], va_scr[hh, 0:L, :], True)

    @pl.when(i > 0)
    def _():
        s_first = [scores(hh, 0) for hh in hs]
        for hh in hs:
            s_scr[hh, 0] = scores(hh, 1)
        for hh in hs:
            softmax_pv(hh, s_first[hh], va_scr[hh, rows(0), :], True)

        def body(jj, carry):
            for si in range(sub):
                j = 1 + si if sub == n_rest else 1 + sub * jj + si
                for hh in hs:
                    if sub < n_rest:
                        s_scr[hh, 1 - si % 2] = scores(hh, jnp.minimum(j + 1, n_chunks - 1))
                    elif j + 1 < n_chunks:
                        s_scr[hh, 1 - si % 2] = scores(hh, j + 1)
                    softmax_pv(hh, s_scr[hh, si % 2], va_scr[hh, rows(j), :], False)
            return carry
        if sub < n_rest:
            lax.fori_loop(0, n_rest // sub, body, 0)
        else:
            body(0, 0)

    lam = (jnp.exp(jnp.sum(lq1_ref[...] * lk1_ref[...], axis=-1, keepdims=True))
           - jnp.exp(jnp.sum(lq2_ref[...] * lk2_ref[...], axis=-1, keepdims=True)) + lam_init)
    for hh in hs:
        o_all = acc_scr[hh, :, 0:128] * (1.0 / acc_scr[hh, :, 128:256])
        o = o_all[0:DA_TQ] - lam * o_all[DA_TQ:]
        o = o * lax.rsqrt(jnp.mean(o * o, axis=-1, keepdims=True) + 1e-5) * sg_ref[...] * (1.0 - lam_init)
        o_ref[:, hl[hh]] = o.astype(o_ref.dtype)


def _da_call(qkv, lq1, lk1, lq2, lk2, sg, L, lam_init):
    B, S, _ = qkv.shape
    tk = next(t for t in DA_TK if S % t == 0 and t >= L)
    n_chunks = S // tk
    n_rest = n_chunks - 1
    sub = DA_SUB if n_rest % DA_SUB == 0 else (2 if n_rest % 2 == 0 else n_rest)
    nh = DA_HEADS // DA_NH
    w = 128 * DA_NH
    full = lambda shape: pl.BlockSpec(shape, lambda b, h, i: (0,) * len(shape))
    return pl.pallas_call(
        functools.partial(_da_kernel, L=L, tk=tk, n_chunks=n_chunks, sub=sub, lam_init=lam_init),
        grid=(B, nh, S // DA_TQ),
        in_specs=[pl.BlockSpec((None, DA_TQ, w), lambda b, h, i: (b, i, h)),
                  pl.BlockSpec((None, S, w), lambda b, h, i: (b, 0, nh + h)),
                  pl.BlockSpec((None, S, w), lambda b, h, i: (b, 0, 2 * nh + h)),
                  full((1, 64)), full((1, 64)), full((1, 64)), full((1, 64)), full((1, 128))],
        out_specs=pl.BlockSpec((None, DA_TQ, w), lambda b, h, i: (b, i, h)),
        out_shape=jax.ShapeDtypeStruct((B, S, DA_W), MM),
        scratch_shapes=[pltpu.VMEM((DA_NH, 2 * DA_TQ, 128), F32), pltpu.VMEM((DA_NH, 2 * DA_TQ, 256), F32),
                        pltpu.VMEM((DA_NH, 2, 2 * DA_TQ, tk), F32), pltpu.VMEM((DA_NH, S, 256), MM)],
        compiler_params=_params(("arbitrary", "arbitrary", "arbitrary")),
        name="diff_attention",
    )(qkv, qkv, qkv, lq1, lk1, lq2, lk2, sg)


def _route(logits_t, rb):
    s = _sigmoid(logits_t)
    sel = s + rb
    x = [sel[e:e + 1, :] for e in range(N_EXPERTS)]
    tg = []
    for g in range(N_GROUPS):
        best = None
        for a in range(EPG):
            for b in range(a + 1, EPG):
                pair = x[EPG * g + a] + x[EPG * g + b]
                best = pair if best is None else jnp.maximum(best, pair)
        tg.append(best)
    rows = []
    gsel = []
    for g in range(N_GROUPS):
        chosen = None
        for g2 in range(N_GROUPS):
            if g2 == g:
                continue
            c = (tg[g] > tg[g2]) if g2 < g else (tg[g] >= tg[g2])
            chosen = c if chosen is None else jnp.logical_and(chosen, c)
        for a in range(EPG):
            rank = jnp.zeros_like(x[0])
            for b in range(EPG):
                if b == a:
                    continue
                ahead = (x[EPG * g + b] >= x[EPG * g + a]) if b < a else (x[EPG * g + b] > x[EPG * g + a])
                rank = rank + jnp.where(ahead, 1.0, 0.0)
            picked = jnp.logical_and(chosen, rank < 1.5)
            rows.append(jnp.where(picked, s[EPG * g + a:EPG * g + a + 1, :], 0.0))
        gsel.append(jnp.where(chosen, 1.0, 0.0))
    comb = jnp.concatenate(rows, axis=0)
    return comb * (1.0 / jnp.sum(comb, axis=0, keepdims=True)), jnp.concatenate(gsel, axis=0)


def _router_kernel(h_ref, mod_ref, rwt_ref, rb_ref, u_ref, rt_ref, *, nB, tile0):
    b = pl.program_id(0)
    i = pl.program_id(1)
    row = jnp.where(i + tile0 == 0, nB, b)
    u = h_ref[...] * (1.0 + _mod_row(mod_ref, row, 4)) + _mod_row(mod_ref, row, 3)
    u_ref[...] = u.astype(MM)
    logits_t = lax.dot_general(rwt_ref[...], u, (((1,), (1,)), ((), ())), precision=HIGHEST,
                               preferred_element_type=F32)
    comb_t, gsel_t = _route(logits_t, rb_ref[...])
    pad = jnp.zeros((MOE_RT_ROWS - N_EXPERTS - N_GROUPS, TM), F32)
    rt_ref[...] = jnp.concatenate([comb_t, gsel_t, pad], axis=0)


def _router_call(h, mod, rwt, rb, nB, tile0):
    B, S, _ = h.shape
    R = mod.shape[0]
    nt = S // TM - tile0
    full = lambda shape: pl.BlockSpec(shape, lambda b, i: (0,) * len(shape))
    return pl.pallas_call(
        functools.partial(_router_kernel, nB=nB, tile0=tile0),
        grid=(B, nt),
        in_specs=[pl.BlockSpec((None, TM, D), lambda b, i: (b, i + tile0, 0)),
                  full((R, 6 * D)), full((N_EXPERTS, D)), full((N_EXPERTS, 1))],
        out_specs=[pl.BlockSpec((None, TM, D), lambda b, i: (b, i, 0)),
                   pl.BlockSpec((MOE_RT_ROWS, TM), lambda b, i: (0, b * nt + i))],
        out_shape=[jax.ShapeDtypeStruct((B, nt * TM, D), MM),
                   jax.ShapeDtypeStruct((MOE_RT_ROWS, B * nt * TM), F32)],
        compiler_params=_params(("parallel", "arbitrary")),
        name="moe_router",
    )(h, mod, rwt, rb)


def _expert_kernel(cnt_ref, u_ref, rt_ref, wg_ref, wu_ref, wd_ref, f_ref, xs_scr, cs_scr, fs_scr, pos_scr,
                   *, TE, NP, nt):
    t = pl.program_id(0)
    g = pl.program_id(1)
    lane = lax.broadcasted_iota(jnp.int32, (MOE_BLK, 128), 1)
    for hf in range(NP):
        base = hf * TE
        n = [cnt_ref[gg * nt + t * NP + hf] for gg in range(N_GROUPS)]
        off = [jnp.int32(0)]
        for gg in range(N_GROUPS - 1):
            off.append(off[-1] + n[gg])

        @pl.when(g == 0)
        def _(base=base, off=off):
            rt = rt_ref[:, base:base + TE]
            gs = rt[N_EXPERTS:N_EXPERTS + 8, :]
            r_i = lax.broadcasted_iota(jnp.int32, (TE, TE), 0)
            c_i = lax.broadcasted_iota(jnp.int32, (TE, TE), 1)
            before = jnp.where(r_i < c_i, 1.0, 0.0).astype(MM)
            cnt_before = jnp.dot(gs.astype(MM), before, preferred_element_type=F32)
            pos = jnp.zeros((1, TE), F32)
            for gg in range(N_GROUPS):
                pos = pos + gs[gg:gg + 1, :] * (off[gg].astype(F32) + cnt_before[gg:gg + 1, :])
            perm = jnp.where(r_i.astype(F32) == pos, 1.0, 0.0).astype(MM)
            xs_scr[base:base + TE, :] = jnp.dot(perm, u_ref[base:base + TE, :],
                                                preferred_element_type=F32).astype(MM)
            rt_pad = jnp.concatenate([rt, jnp.zeros((128 - MOE_RT_ROWS, TE), F32)], axis=0)
            hi, lo = _split(rt_pad)
            cs_scr[base:base + TE, :] = _dot_nt(perm, hi) + _dot_nt(perm, lo)
            pos_scr[base:base + TE, :] = jnp.broadcast_to(pos, (128, TE)).T
            fs_scr[base:base + TE, :] = jnp.zeros((TE, D), F32)

        lo_g = jnp.int32(0)
        n_g = jnp.int32(0)
        for gg in range(N_GROUPS):
            lo_g = jnp.where(g == gg, off[gg], lo_g)
            n_g = jnp.where(g == gg, n[gg], n_g)
        hi_g = lo_g + n_g
        for rb in range(TE // MOE_BLK):
            @pl.when(jnp.logical_and(lo_g < MOE_BLK * (rb + 1), hi_g > MOE_BLK * rb))
            def _(r0=base + rb * MOE_BLK):
                rows = slice(r0, r0 + MOE_BLK)
                xb = xs_scr[rows, :]
                c = cs_scr[rows, :]
                hg = [jnp.dot(xb, wg_ref[e], preferred_element_type=F32) for e in range(EPG)]
                hu = [jnp.dot(xb, wu_ref[e], preferred_element_type=F32) for e in range(EPG)]
                col = [jnp.sum(jnp.where(lane == EPG * g + e, c, 0.0), axis=-1, keepdims=True) for e in range(EPG)]
                he = jnp.concatenate([(hg[e] * _sigmoid(hg[e]) * hu[e] * col[e]).astype(MM) for e in range(EPG)],
                                     axis=1)
                fs_scr[rows, :] += jnp.dot(he, wd_ref[...], preferred_element_type=F32)

        @pl.when(g == N_GROUPS - 1)
        def _(base=base):
            c_i = lax.broadcasted_iota(jnp.int32, (TE, TE), 1)
            pos_t = jnp.tile(pos_scr[base:base + TE, :], (1, TE // 128))
            unperm = jnp.where(c_i.astype(F32) == pos_t, 1.0, 0.0).astype(MM)
            f_ref[base:base + TE, :] = jnp.dot(unperm, fs_scr[base:base + TE, :].astype(MM),
                                               preferred_element_type=F32).astype(f_ref.dtype)


def _expert_call(u, rt, wg, wu, wd):
    N = u.shape[0]
    TE = MOE_TE if N % MOE_TE == 0 else TM
    nt = N // TE
    NP = MOE_NP if nt % MOE_NP == 0 else 1
    cnt = jnp.sum(rt[N_EXPERTS:N_EXPERTS + N_GROUPS].reshape(N_GROUPS, nt, TE), axis=-1)
    cnt = cnt.astype(jnp.int32).reshape(N_GROUPS * nt)
    gw = EPG * D_EXPERT
    grid_spec = pltpu.PrefetchScalarGridSpec(
        num_scalar_prefetch=1,
        grid=(nt // NP, N_GROUPS),
        in_specs=[pl.BlockSpec((NP * TE, D), lambda t, g, c: (t, 0)),
                  pl.BlockSpec((MOE_RT_ROWS, NP * TE), lambda t, g, c: (0, t)),
                  pl.BlockSpec((None, EPG, D, D_EXPERT), lambda t, g, c: (g, 0, 0, 0)),
                  pl.BlockSpec((None, EPG, D, D_EXPERT), lambda t, g, c: (g, 0, 0, 0)),
                  pl.BlockSpec((None, gw, D), lambda t, g, c: (g, 0, 0))],
        out_specs=pl.BlockSpec((NP * TE, D), lambda t, g, c: (t, 0)),
        scratch_shapes=[pltpu.VMEM((NP * TE, D), MM), pltpu.VMEM((NP * TE, 128), F32),
                        pltpu.VMEM((NP * TE, D), F32), pltpu.VMEM((NP * TE, 128), F32)])
    return pl.pallas_call(
        functools.partial(_expert_kernel, TE=TE, NP=NP, nt=nt),
        grid_spec=grid_spec,
        out_shape=jax.ShapeDtypeStruct((N, D), MM),
        compiler_params=_params(("parallel", "arbitrary")),
        name="moe_experts",
    )(cnt, u, rt, wg, wu, wd)


def _moe_ln_kernel(h_ref, mod_ref, f_ref, lng_ref, lnb_ref, o_ref, *, nB, tile0):
    b = pl.program_id(0)
    i = pl.program_id(1)
    row = jnp.where(i + tile0 == 0, nB, b)
    _res_ln(h_ref, mod_ref, row, 5, f_ref[...].astype(F32), lng_ref, lnb_ref, o_ref)


def _moe_ln_call(h, mod, f, lng, lnb, nB, tile0):
    B, S, _ = h.shape
    R = mod.shape[0]
    nt = S // TM - tile0
    full = lambda shape: pl.BlockSpec(shape, lambda b, i: (0,) * len(shape))
    return pl.pallas_call(
        functools.partial(_moe_ln_kernel, nB=nB, tile0=tile0),
        grid=(B, nt),
        in_specs=[pl.BlockSpec((None, TM, D), lambda b, i: (b, i + tile0, 0)),
                  full((R, 6 * D)),
                  pl.BlockSpec((None, TM, D), lambda b, i: (b, i, 0)),
                  full((1, D)), full((1, D))],
        out_specs=pl.BlockSpec((None, TM, D), lambda b, i: (b, i, 0)),
        out_shape=jax.ShapeDtypeStruct((B, nt * TM, D), F32),
        compiler_params=_params(("parallel", "arbitrary")),
        name="moe_res_ln",
    )(h, mod, f, lng, lnb)


def _moe_call(h, mod, rwt, rb, wg, wu, wd, lng, lnb, nB, latent_only, L):
    B = h.shape[0]
    tile0 = L // TM if latent_only else 0
    u, rt = _router_call(h, mod, rwt, rb, nB, tile0)
    f = _expert_call(u.reshape(-1, D), rt, wg.reshape(N_GROUPS, EPG, D, D_EXPERT),
                     wu.reshape(N_GROUPS, EPG, D, D_EXPERT), wd.reshape(N_GROUPS, EPG * D_EXPERT, D))
    return _moe_ln_call(h, mod, f.reshape(B, -1, D), lng, lnb, nB, tile0)


def kernel(x, c, ctx, c_ctx, ada_w, ada_b, ln_g, ln_b, even_w_in, even_w_out, shift_w, na_rpb, rw_w0, rw_w_up, rw_a0, rw_a_up, rw_g_up, rw_k_k, rw_k_a, rw_r_k, rw_gn_g, rw_gn_b, odd_w_in, odd_w_out, da_lq1, da_lk1, da_lq2, da_lk2, da_subln_g, router_w, router_b, exp_w_gate, exp_w_up, exp_w_down):
    B, T, _ = x.shape
    L = ctx.shape[1]
    assert L == TM and T % TM == 0 and (T // GRID_W) >= 12
    rows = T // GRID_W
    R = -(-(B + 1) // 8) * 8

    cv = jnp.zeros((R, D), F32).at[:B].set(c).at[B].set(c_ctx)
    mods = _ada_call(cv, ada_w, ada_b)

    cos_t, sin_t = _rope_tables(L, T)
    rwt = router_w.T
    rb = router_b.reshape(N_EXPERTS, 1)
    zpad = jnp.zeros((2, 64, RW_W), F32)

    h = jnp.concatenate([ctx, x], axis=1)
    for l in range(DEPTH):
        mod = mods[l]
        i = l // 2
        lng = ln_g[l].reshape(2, 1, D)
        lnb = ln_b[l].reshape(2, 1, D)
        if l % 2 == 0:
            w_in = even_w_in[i].astype(MM)
            qkv, prw = _even_in_call(h, mod, w_in[:, :3 * NA_W], w_in[:, 3 * NA_W:], B)
            ona = _na_call(qkv, _na_bias(na_rpb[i], rows), L)
            yf, yr, bg = _rw_call(
                prw, shift_w[i], rw_w0[i].reshape(2, 1, RW_W),
                jnp.concatenate([rw_w_up[i], zpad], axis=1).astype(MM), rw_a0[i].reshape(2, 1, RW_W),
                jnp.concatenate([zpad, rw_a_up[i]], axis=1).astype(MM), rw_g_up[i].astype(MM),
                rw_k_k[i].reshape(1, RW_W), rw_k_a[i].reshape(1, RW_W), rw_r_k[i].reshape(1, RW_W), L)
            h = _even_out_call(h, mod, ona, yf, yr, bg, rw_gn_g[i].reshape(1, RW_W), rw_gn_b[i].reshape(1, RW_W),
                               even_w_out[i].astype(MM), lng[0], lnb[0], B)
        else:
            lam_init = 0.8 - 0.6 * math.exp(-0.3 * l)
            qkv = _odd_in_call(h, mod, odd_w_in[i].astype(MM), cos_t, sin_t, B)
            oda = _da_call(qkv, da_lq1[i].reshape(1, 64), da_lk1[i].reshape(1, 64), da_lq2[i].reshape(1, 64),
                           da_lk2[i].reshape(1, 64), da_subln_g[i].reshape(1, 128), L, lam_init)
            h = _odd_out_call(h, mod, oda, odd_w_out[i].astype(MM), lng[0], lnb[0], B)
        h = _moe_call(h, mod, rwt, rb, exp_w_gate[l].astype(MM), exp_w_up[l].astype(MM), exp_w_down[l].astype(MM),
                      lng[1], lnb[1], B, l == DEPTH - 1, L)
    return h
```

```python
import functools
import math

import numpy as np
import jax
import jax.numpy as jnp
from jax import lax
from jax.experimental import pallas as pl
from jax.experimental.pallas import tpu as pltpu

F32 = jnp.float32
MM = jnp.bfloat16
HIGHEST = lax.Precision.HIGHEST

D = 1024
DEPTH = 4
GRID_W = 64
NA_HEADS = 8
NA_W = 512
NA_WIN_R = 8
NA_WIN_C = 16
NA_QB = 128
NA_NB = 2
NA_KROWS = 9
NA_NLOC = NA_KROWS * GRID_W
RW_W = 512
RW_HEAD = 64
RW_IN = 1792
RW_GN_EPS = 64e-5
RW_CHUNK = 64
RW_GRP = 256
RW_NSUB = 2
RW_INV_BASE = 8
RW_INV_LEVELS = 3
RW_NMASK = 7 + RW_INV_LEVELS
DA_HEADS = 8
DA_W = 1024
ROPE_THETA = 10000.0
N_EXPERTS = 16
N_GROUPS = 4
EPG = 4
D_EXPERT = 512
MOE_RT_ROWS = 32
MOE_TE = 512
MOE_NP = 2
MOE_BLK = 128
ALPHA = (2 * DEPTH) ** 0.25
LN_EPS = 1e-5
TM = 256
DA_TQ = 256
DA_NH = 2
DA_TK = (768, 512, 256)
DA_NBUF = 3
DA_QSCALE = 0.125 * math.log2(math.e)
NEG = -1e30
VMEM_LIMIT = 56 * 1024 * 1024


def _dot(a, b):
    return jnp.dot(a.astype(MM), b.astype(MM), preferred_element_type=F32)


def _dot_nt(a, b):
    return lax.dot_general(a.astype(MM), b.astype(MM), (((1,), (1,)), ((), ())), preferred_element_type=F32)


def _split(x):
    hi = x.astype(MM)
    lo = (x - hi.astype(F32)).astype(MM)
    return hi, lo


def _dot_split_lhs(x, w):
    hi, lo = _split(x)
    return jnp.dot(hi, w, preferred_element_type=F32) + jnp.dot(lo, w, preferred_element_type=F32)


def _dot_split_rhs(w, x):
    hi, lo = _split(x)
    return jnp.dot(w, hi, preferred_element_type=F32) + jnp.dot(w, lo, preferred_element_type=F32)


def _sigmoid(x):
    return 1.0 / (1.0 + jnp.exp(-x))


def _params(sem):
    return pltpu.CompilerParams(dimension_semantics=sem, vmem_limit_bytes=VMEM_LIMIT)


def _mod_row(mod_ref, row, j):
    return mod_ref[pl.ds(row, 1), j * D:(j + 1) * D]


def _layer_norm(z, g, b):
    mu = jnp.mean(z, axis=-1, keepdims=True)
    zc = z - mu
    var = jnp.mean(zc * zc, axis=-1, keepdims=True)
    return zc * lax.rsqrt(var + LN_EPS) * g + b


def _ada_kernel(cv_ref, w_ref, b_ref, o_ref):
    x = cv_ref[...]
    x = x * _sigmoid(x)
    o_ref[...] = jnp.dot(x, w_ref[...], precision=HIGHEST, preferred_element_type=F32) + b_ref[...]


def _ada_call(cv, ada_w, ada_b):
    R = cv.shape[0]
    tn = 1536
    return pl.pallas_call(
        _ada_kernel,
        grid=(DEPTH, 6 * D // tn),
        in_specs=[pl.BlockSpec((R, D), lambda l, j: (0, 0)),
                  pl.BlockSpec((None, D, tn), lambda l, j: (l, 0, j)),
                  pl.BlockSpec((None, 1, tn), lambda l, j: (l, 0, j))],
        out_specs=pl.BlockSpec((None, R, tn), lambda l, j: (l, 0, j)),
        out_shape=jax.ShapeDtypeStruct((DEPTH, R, 6 * D), F32),
        compiler_params=_params(("arbitrary", "arbitrary")),
        name="ada_mod",
    )(cv, ada_w, ada_b.reshape(DEPTH, 1, 6 * D))


def _even_in_kernel(h_ref, mod_ref, wna_ref, wrw_ref, qkv_ref, prw_ref, *, nB):
    b = pl.program_id(0)
    i = pl.program_id(1)
    row = jnp.where(i == 0, nB, b)
    u = (h_ref[...] * (1.0 + _mod_row(mod_ref, row, 1)) + _mod_row(mod_ref, row, 0)).astype(MM)
    res = jnp.dot(u, wna_ref[...], preferred_element_type=F32)
    qkv_ref[:, 0:NA_W] = (res[:, 0:NA_W] * 0.125).astype(qkv_ref.dtype)
    qkv_ref[:, NA_W:] = res[:, NA_W:].astype(qkv_ref.dtype)
    prw_ref[...] = jnp.dot(u, wrw_ref[...], preferred_element_type=F32)


def _even_in_call(h, mod, w_na, w_rw, nB):
    B, S, _ = h.shape
    R = mod.shape[0]
    return pl.pallas_call(
        functools.partial(_even_in_kernel, nB=nB),
        grid=(B, S // TM),
        in_specs=[pl.BlockSpec((None, TM, D), lambda b, i: (b, i, 0)),
                  pl.BlockSpec((R, 6 * D), lambda b, i: (0, 0)),
                  pl.BlockSpec((D, 3 * NA_W), lambda b, i: (0, 0)),
                  pl.BlockSpec((D, RW_IN), lambda b, i: (0, 0))],
        out_specs=[pl.BlockSpec((None, TM, 3 * NA_W), lambda b, i: (b, i, 0)),
                   pl.BlockSpec((None, TM, RW_IN), lambda b, i: (b, i, 0))],
        out_shape=[jax.ShapeDtypeStruct((B, S, 3 * NA_W), MM),
                   jax.ShapeDtypeStruct((B, S, RW_IN), F32)],
        compiler_params=_params(("parallel", "arbitrary")),
        name="even_in_proj",
    )(h, mod, w_na, w_rw)


def _na_bias(rpb, rows):
    cq = np.arange(GRID_W)[:, None]
    ck = np.arange(GRID_W)[None, :]
    cs = np.clip(cq - NA_WIN_C // 2, 0, GRID_W - NA_WIN_C)
    ok_c = (ck >= cs) & (ck < cs + NA_WIN_C)
    dc = np.clip(ck - cq + NA_WIN_C - 1, 0, 2 * NA_WIN_C - 2)
    onehot = np.zeros((2 * NA_WIN_C - 1, GRID_W * GRID_W), np.float32)
    onehot[dc.ravel(), np.arange(GRID_W * GRID_W)] = 1.0
    rc = jnp.einsum('hrd,dn->hrn', rpb, jnp.asarray(onehot), precision=HIGHEST)
    rc = rc.reshape(rpb.shape[0], 2 * NA_WIN_R - 1, GRID_W, GRID_W)
    cases = []
    for r0 in (0, 2, 4, rows - 4, rows - 2):
        ws = int(np.clip(r0 - NA_WIN_R // 2, 0, rows - NA_KROWS))
        qrows = []
        for qi in range(NA_QB // GRID_W):
            rq = r0 + qi
            rs = int(np.clip(rq - NA_WIN_R // 2, 0, rows - NA_WIN_R))
            pieces = []
            for j in range(NA_KROWS):
                rk = ws + j
                if rs <= rk < rs + NA_WIN_R:
                    pieces.append(jnp.where(ok_c[None], rc[:, rk - rq + NA_WIN_R - 1], NEG))
                else:
                    pieces.append(jnp.full((rpb.shape[0], GRID_W, GRID_W), NEG, F32))
            qrows.append(jnp.concatenate(pieces, axis=-1))
        cases.append(jnp.concatenate(qrows, axis=-2))
    return jnp.stack(cases, axis=0)


def _na_kernel(q_ref, k_ref, v_ref, *rest, L, rows):
    bias_refs, o_ref = rest[:NA_NB], rest[NA_NB]
    i = pl.program_id(2)
    nq = L // (NA_NB * NA_QB)
    lane = lax.broadcasted_iota(jnp.int32, (NA_QB, 128), 1)
    lo = lane < 64
    kc = k_ref[0:L, :]
    vc = v_ref[0:L, :]
    ch = [(n, hh) for n in range(NA_NB) for hh in range(2)]
    qs = [q_ref[n * NA_QB:(n + 1) * NA_QB, :] for n in range(NA_NB)]
    qm = {(n, hh): jnp.where(lo if hh == 0 else jnp.logical_not(lo), qs[n], jnp.zeros_like(qs[n])) for n, hh in ch}

    def store(outs):
        for n in range(NA_NB):
            o_ref[n * NA_QB:(n + 1) * NA_QB, :] = jnp.where(lo, outs[(n, 0)], outs[(n, 1)]).astype(o_ref.dtype)

    @pl.when(i < nq)
    def _():
        s = {c: _dot_nt(qm[c], kc) for c in ch}
        m = {c: jnp.max(s[c], axis=-1, keepdims=True) for c in ch}
        p = {c: jnp.exp(s[c] - m[c]) for c in ch}
        l = {c: jnp.sum(p[c], axis=-1, keepdims=True) for c in ch}
        store({c: _dot(p[c], vc) * (1.0 / l[c]) for c in ch})

    @pl.when(i >= nq)
    def _():
        kl, vl = [], []
        for n in range(NA_NB):
            r0 = 2 * ((i - nq) * NA_NB + n)
            ws = jnp.clip(r0 - NA_WIN_R // 2, 0, rows - NA_KROWS)
            start = pl.multiple_of(L + GRID_W * ws, GRID_W)
            kl.append(k_ref[pl.ds(start, NA_NLOC), :])
            vl.append(v_ref[pl.ds(start, NA_NLOC), :])
        s_loc = {c: _dot_nt(qm[c], kl[c[0]]) + bias_refs[c[0]][c[1]] for c in ch}
        s_ctx = {c: _dot_nt(qm[c], kc) for c in ch}
        m = {c: jnp.maximum(jnp.max(s_loc[c], axis=-1, keepdims=True), jnp.max(s_ctx[c], axis=-1, keepdims=True))
             for c in ch}
        p_loc = {c: jnp.exp(s_loc[c] - m[c]) for c in ch}
        p_ctx = {c: jnp.exp(s_ctx[c] - m[c]) for c in ch}
        l = {c: jnp.sum(p_loc[c], axis=-1, keepdims=True) + jnp.sum(p_ctx[c], axis=-1, keepdims=True) for c in ch}
        store({c: (_dot(p_loc[c], vl[c[0]]) + _dot(p_ctx[c], vc)) * (1.0 / l[c]) for c in ch})


def _na_call(qkv, bias, L):
    B, S, _ = qkv.shape
    rows = (S - L) // GRID_W
    tq = NA_NB * NA_QB
    assert L % tq == 0 and (S - L) % tq == 0
    nq = L // tq
    nhp = NA_HEADS // 2

    def bias_idx(n):
        def idx(b, hp, i):
            r0 = 2 * ((i - nq) * NA_NB + n)
            c = jnp.where(r0 == 0, 0,
                          jnp.where(r0 == 2, 1, jnp.where(r0 == rows - 4, 3, jnp.where(r0 == rows - 2, 4, 2))))
            return (jnp.where(i < nq, 2, c), hp, 0, 0)
        return idx

    return pl.pallas_call(
        functools.partial(_na_kernel, L=L, rows=rows),
        grid=(B, nhp, S // tq),
        in_specs=[pl.BlockSpec((None, tq, 128), lambda b, hp, i: (b, i, hp)),
                  pl.BlockSpec((None, S, 128), lambda b, hp, i: (b, 0, nhp + hp)),
                  pl.BlockSpec((None, S, 128), lambda b, hp, i: (b, 0, 2 * nhp + hp))]
                 + [pl.BlockSpec((None, 2, NA_QB, NA_NLOC), bias_idx(n)) for n in range(NA_NB)],
        out_specs=pl.BlockSpec((None, tq, 128), lambda b, hp, i: (b, i, hp)),
        out_shape=jax.ShapeDtypeStruct((B, S, NA_W), MM),
        compiler_params=_params(("parallel", "parallel", "arbitrary")),
        name="na_attention",
    )(qkv, qkv, qkv, *([bias] * NA_NB))


def _rw_conv(p_ref, prev_ref, next_ref, c, conv_ref, nL, NC):
    p = p_ref[...]
    at_start = jnp.logical_or(c == 0, c == nL)
    at_end = jnp.logical_or(c == nL - 1, c == NC - 1)
    prow = jnp.where(at_start, 0.0, prev_ref[7:8, :])
    nrow = jnp.where(at_end, 0.0, next_ref[0:1, :])
    rid = lax.broadcasted_iota(jnp.int32, p.shape, 0)
    last = p.shape[0] - 1
    pm = jnp.where(rid == 0, prow, pltpu.roll(p, 1, axis=0))
    pp = jnp.where(rid == last, nrow, pltpu.roll(p, last, axis=0))
    return pm * conv_ref[0:1, :] + p * conv_ref[1:2, :] + pp * conv_ref[2:3, :]


def _rw_expand(z, bd):
    return jnp.where(bd, jnp.concatenate([z, z, z, z], axis=0), 0.0)


def _rw_prep(pc, d, w0_ref, wup_ref, a0_ref, aup_ref, kk_ref, ka_ref, ones_ref, cum_ref):
    r = pc[:, 0:RW_W]
    k = pc[:, RW_W:2 * RW_W]
    v = pc[:, 2 * RW_W:3 * RW_W]
    xwa = pc[:, 3 * RW_W:3 * RW_W + 128]
    z = w0_ref[d] + _dot(jnp.tanh(xwa), wup_ref[d])
    lw = -math.exp(-0.5) * _sigmoid(z)
    a = _sigmoid(a0_ref[d] + _dot(xwa, aup_ref[d]))
    kk = k * kk_ref[...]
    n2 = _dot_split_lhs(kk * kk, ones_ref[...])
    kk = kk * (1.0 / jnp.maximum(jnp.sqrt(n2), 1e-12))
    keff = k * (1.0 + (a - 1.0) * ka_ref[...])
    av = -kk
    bv = kk * a
    lc = _dot_split_rhs(cum_ref[d], lw)
    ltot = lc[RW_CHUNK - 1:RW_CHUNK, :] if d == 0 else lc[0:1, :]
    at = av * jnp.exp(lc - lw)
    rt = r * jnp.exp(lc)
    ginv = jnp.exp(-lc)
    bt = bv * ginv
    kt = keff * ginv
    grest = jnp.exp(ltot - lc)
    bh = bv * grest
    kh = keff * grest
    gc = jnp.exp(ltot)

    return dict(at=at, rt=rt, bt=bt, kt=kt, bh=bh, kh=kh, v=v, gc=gc, keff=keff, r=r)


def _rw_chains(preps, mask_ref, g_scr):
    bd = mask_ref[0] > 0.5
    bdm = mask_ref[0].astype(MM)
    eye = mask_ref[1]
    ngrp = RW_W // RW_GRP
    order = {0: list(range(RW_NSUB)), 1: list(range(RW_NSUB - 1, -1, -1))}
    chains = [(d, g, order[d][n]) for n in range(RW_NSUB) for d in range(2) for g in range(ngrp)]
    ex = {}
    for c in chains:
        d, g, slot = c
        sl = slice(g * RW_GRP, (g + 1) * RW_GRP)
        p = preps[(d, slot)]
        e = {}
        for n in ("at", "rt", "bt", "kt", "bh", "kh", "v"):
            z = p[n][:, sl].astype(MM)
            e[n] = jnp.concatenate([z, z, z, z], axis=0) * bdm
        e["vt"] = e["v"].T
        e["gc"] = p["gc"][:, sl]
        ex[c] = e
    xab, xrb, xak, xrk = {}, {}, {}, {}
    for c in chains:
        ar = jnp.concatenate([ex[c]["at"], ex[c]["rt"]], axis=0)
        ms = mask_ref[2 + 2 * c[0]]
        mi = mask_ref[3 + 2 * c[0]]
        xb = _dot_nt(ex[c]["bt"], ar)
        xab[c] = (xb[:, 0:RW_GRP] * ms).astype(MM)
        xrb[c] = (xb[:, RW_GRP:] * mi).astype(MM)
        xk = _dot_nt(ex[c]["kt"], ar)
        xak[c] = (xk[:, 0:RW_GRP] * ms).astype(MM)
        xrk[c] = (xk[:, RW_GRP:] * mi).astype(MM)
    eyeb = eye.astype(MM)
    xp = {c: xab[c] * mask_ref[6].astype(MM) for c in chains}
    tt = {c: eyeb + xp[c] for c in chains}
    for _ in range(RW_INV_BASE.bit_length() - 2):
        xp = {c: _dot(xp[c], xp[c]).astype(MM) for c in chains}
        tt = {c: _dot(tt[c], eyeb + xp[c]).astype(MM) for c in chains}
    for lvl in range(RW_INV_LEVELS):
        cm = mask_ref[7 + lvl].astype(MM)
        tc = {c: _dot(tt[c], xab[c] * cm).astype(MM) for c in chains}
        tt = {c: _dot(eyeb + tc[c], tt[c]).astype(MM) for c in chains}
    ys = {}
    for n in range(RW_NSUB):
        cur = [(d, g, order[d][n]) for d in range(2) for g in range(ngrp)]
        gs = {c: g_scr[c[0], c[1]] for c in cur}
        gsb = {c: gs[c].astype(MM) for c in cur}
        w1 = {c: (_dot_nt(gsb[c], ex[c]["at"]) + _dot(ex[c]["vt"], xak[c])).astype(MM) for c in cur}
        ut = {c: _dot(w1[c], tt[c]).astype(MM) for c in cur}
        yt = {c: _dot_nt(gsb[c], ex[c]["rt"]) + _dot(ut[c], xrb[c]) + _dot(ex[c]["vt"], xrk[c]) for c in cur}
        for c in cur:
            g_scr[c[0], c[1]] = gs[c] * ex[c]["gc"] + _dot(ut[c], ex[c]["bh"]) + _dot(ex[c]["vt"], ex[c]["kh"])
        for c in cur:
            ybd = yt[c].T
            ys[c] = ybd[0:64] + ybd[64:128] + ybd[128:192] + ybd[192:256]
    return [jnp.concatenate([jnp.concatenate([ys[(d, g, slot)] for g in range(ngrp)], axis=1)
                             for slot in range(RW_NSUB)], axis=0) for d in range(2)]


def _rw_kernel(pf_ref, pfp_ref, pfn_ref, pr_ref, prp_ref, prn_ref, conv_ref, w0_ref, wup_ref, a0_ref, aup_ref,
               gup_ref, kk_ref, ka_ref, rk_ref, ones_ref, cum_ref, mask_ref, yf_ref, yr_ref, bg_ref, g_scr,
               *, nL, NC):
    i = pl.program_id(1)

    @pl.when(i == 0)
    def _():
        g_scr[...] = jnp.zeros_like(g_scr)

    cf = i
    cr = jnp.where(i < nL, nL - 1 - i, NC - 1 - i + nL)
    args = (w0_ref, wup_ref, a0_ref, aup_ref, kk_ref, ka_ref, ones_ref, cum_ref)

    pcf = _rw_conv(pf_ref, pfp_ref, pfn_ref, cf, conv_ref, nL, NC)
    pcr = _rw_conv(pr_ref, prp_ref, prn_ref, cr, conv_ref, nL, NC)
    preps = {}
    for n in range(RW_NSUB):
        sf, sr = n, RW_NSUB - 1 - n
        preps[(0, sf)] = _rw_prep(pcf[sf * RW_CHUNK:(sf + 1) * RW_CHUNK], 0, *args)
        preps[(1, sr)] = _rw_prep(pcr[sr * RW_CHUNK:(sr + 1) * RW_CHUNK], 1, *args)
    xwa = pcf[:, 3 * RW_W:3 * RW_W + 128]
    a_r = _sigmoid(a0_ref[1] + _dot(xwa, aup_ref[1]))
    keff_r = pcf[:, RW_W:2 * RW_W] * (1.0 + (a_r - 1.0) * ka_ref[...])
    keff_f = jnp.concatenate([preps[(0, s)]["keff"] for s in range(RW_NSUB)], axis=0)
    rv = pcf[:, 0:RW_W] * rk_ref[...]
    bsum = _dot_split_lhs(rv * (keff_f + keff_r), ones_ref[...]) * pcf[:, 2 * RW_W:3 * RW_W]
    bg_ref[:, 0:RW_W] = bsum
    bg_ref[:, RW_W:] = _dot(_sigmoid(pcf[:, 3 * RW_W + 128:]), gup_ref[...])

    y_f, y_r = _rw_chains(preps, mask_ref, g_scr)
    yf_ref[...] = y_f
    yr_ref[...] = y_r


def _rw_masks():
    n = 4 * RW_CHUNK
    idx = np.arange(n)
    hd = idx // RW_CHUNK
    t = idx % RW_CHUNK
    bd = hd[:, None] == hd[None, :]
    m = np.zeros((RW_NMASK, n, n), np.float32)
    m[0] = bd
    m[1] = np.eye(n)
    m[2] = bd & (t[:, None] < t[None, :])
    m[3] = bd & (t[:, None] <= t[None, :])
    m[4] = bd & (t[:, None] > t[None, :])
    m[5] = bd & (t[:, None] >= t[None, :])
    same = lambda w: bd & ((t[:, None] // w) == (t[None, :] // w))
    m[6] = same(RW_INV_BASE)
    for lvl in range(RW_INV_LEVELS):
        w = RW_INV_BASE << lvl
        m[7 + lvl] = same(2 * w) & ~same(w)
    tt = np.arange(RW_CHUNK)
    cum = np.stack([tt[None, :] <= tt[:, None], tt[None, :] >= tt[:, None]]).astype(np.float32)
    ch = np.arange(RW_W) // RW_HEAD
    ones = (ch[:, None] == ch[None, :]).astype(np.float32)
    return m, cum, ones


def _rw_call(prw, conv_w, w0, wup_pad, a0, aup_pad, gup, kk, ka, rk, L):
    B, S, _ = prw.shape
    rb = RW_NSUB * RW_CHUNK
    assert L % rb == 0 and S % rb == 0
    NC = S // rb
    nL = L // rb
    n8 = S // 8
    r8 = rb // 8
    m, cum, ones = _rw_masks()

    def cr_of(i):
        return jnp.where(i < nL, nL - 1 - i, NC - 1 - i + nL)

    full = lambda shape: pl.BlockSpec(shape, lambda b, i: (0,) * len(shape))
    in_specs = [
        pl.BlockSpec((None, rb, RW_IN), lambda b, i: (b, i, 0)),
        pl.BlockSpec((None, 8, RW_IN), lambda b, i: (b, jnp.maximum(i * r8 - 1, 0), 0)),
        pl.BlockSpec((None, 8, RW_IN), lambda b, i: (b, jnp.minimum(i * r8 + r8, n8 - 1), 0)),
        pl.BlockSpec((None, rb, RW_IN), lambda b, i: (b, cr_of(i), 0)),
        pl.BlockSpec((None, 8, RW_IN), lambda b, i: (b, jnp.maximum(cr_of(i) * r8 - 1, 0), 0)),
        pl.BlockSpec((None, 8, RW_IN), lambda b, i: (b, jnp.minimum(cr_of(i) * r8 + r8, n8 - 1), 0)),
        full((3, RW_IN)), full((2, 1, RW_W)), full((2, 128, RW_W)), full((2, 1, RW_W)), full((2, 128, RW_W)),
        full((128, RW_W)), full((1, RW_W)), full((1, RW_W)), full((1, RW_W)),
        full((RW_W, RW_W)), full((2, RW_CHUNK, RW_CHUNK)), full((RW_NMASK, 4 * RW_CHUNK, 4 * RW_CHUNK)),
    ]
    out_specs = [
        pl.BlockSpec((None, rb, RW_W), lambda b, i: (b, i, 0)),
        pl.BlockSpec((None, rb, RW_W), lambda b, i: (b, cr_of(i), 0)),
        pl.BlockSpec((None, rb, 2 * RW_W), lambda b, i: (b, i, 0)),
    ]
    return pl.pallas_call(
        functools.partial(_rw_kernel, nL=nL, NC=NC),
        grid=(B, NC),
        in_specs=in_specs,
        out_specs=out_specs,
        out_shape=[jax.ShapeDtypeStruct((B, S, RW_W), F32),
                   jax.ShapeDtypeStruct((B, S, RW_W), F32),
                   jax.ShapeDtypeStruct((B, S, 2 * RW_W), F32)],
        scratch_shapes=[pltpu.VMEM((2, RW_W // RW_GRP, RW_GRP, RW_GRP), F32)],
        compiler_params=_params(("parallel", "arbitrary")),
        name="rwkv7_chunked",
    )(prw, prw, prw, prw, prw, prw, conv_w, w0, wup_pad, a0, aup_pad, gup, kk, ka, rk,
      jnp.asarray(ones, MM), jnp.asarray(cum, MM), jnp.asarray(m))


def _res_ln(h_ref, mod_ref, row, j, y, lng_ref, lnb_ref, o_ref):
    z = ALPHA * h_ref[...] + _mod_row(mod_ref, row, j) * y
    o_ref[...] = _layer_norm(z, lng_ref[...], lnb_ref[...])


def _even_out_kernel(h_ref, mod_ref, ona_ref, yf_ref, yr_ref, bg_ref, ones_ref, gng_ref, gnb_ref, wo_ref,
                     lng_ref, lnb_ref, o_ref, *, nB):
    b = pl.program_id(0)
    i = pl.program_id(1)
    row = jnp.where(i == 0, nB, b)
    y = yf_ref[...] + yr_ref[...]
    mu = _dot_split_lhs(y, ones_ref[...]) * (1.0 / RW_HEAD)
    yc = y - mu
    var = _dot_split_lhs(yc * yc, ones_ref[...]) * (1.0 / RW_HEAD)
    yn = yc * lax.rsqrt(var + RW_GN_EPS) * gng_ref[...] + gnb_ref[...]
    orw = (yn + bg_ref[:, 0:RW_W]) * bg_ref[:, RW_W:]
    yy = (jnp.dot(ona_ref[...], wo_ref[0:NA_W, :], preferred_element_type=F32)
          + jnp.dot(orw.astype(MM), wo_ref[NA_W:, :], preferred_element_type=F32))
    _res_ln(h_ref, mod_ref, row, 2, yy, lng_ref, lnb_ref, o_ref)


def _even_out_call(h, mod, ona, yf, yr, bg, gng, gnb, wo, lng, lnb, nB):
    B, S, _ = h.shape
    R = mod.shape[0]
    _, _, ones = _rw_masks()
    tile = lambda w: pl.BlockSpec((None, TM, w), lambda b, i: (b, i, 0))
    full = lambda shape: pl.BlockSpec(shape, lambda b, i: (0,) * len(shape))
    return pl.pallas_call(
        functools.partial(_even_out_kernel, nB=nB),
        grid=(B, S // TM),
        in_specs=[tile(D), full((R, 6 * D)), tile(NA_W), tile(RW_W), tile(RW_W), tile(2 * RW_W),
                  full((RW_W, RW_W)), full((1, RW_W)), full((1, RW_W)), full((D, D)), full((1, D)), full((1, D))],
        out_specs=tile(D),
        out_shape=jax.ShapeDtypeStruct((B, S, D), F32),
        compiler_params=_params(("parallel", "arbitrary")),
        name="even_out_proj_ln",
    )(h, mod, ona, yf, yr, bg, jnp.asarray(ones, MM), gng, gnb, wo, lng, lnb)


def _odd_out_kernel(h_ref, mod_ref, oda_ref, wo_ref, lng_ref, lnb_ref, o_ref, *, nB):
    b = pl.program_id(0)
    i = pl.program_id(1)
    row = jnp.where(i == 0, nB, b)
    yy = jnp.dot(oda_ref[...], wo_ref[...], preferred_element_type=F32)
    _res_ln(h_ref, mod_ref, row, 2, yy, lng_ref, lnb_ref, o_ref)


def _odd_out_call(h, mod, oda, wo, lng, lnb, nB):
    B, S, _ = h.shape
    R = mod.shape[0]
    tile = lambda w: pl.BlockSpec((None, TM, w), lambda b, i: (b, i, 0))
    full = lambda shape: pl.BlockSpec(shape, lambda b, i: (0,) * len(shape))
    return pl.pallas_call(
        functools.partial(_odd_out_kernel, nB=nB),
        grid=(B, S // TM),
        in_specs=[tile(D), full((R, 6 * D)), tile(DA_W), full((DA_W, D)), full((1, D)), full((1, D))],
        out_specs=tile(D),
        out_shape=jax.ShapeDtypeStruct((B, S, D), F32),
        compiler_params=_params(("parallel", "arbitrary")),
        name="odd_out_proj_ln",
    )(h, mod, oda, wo, lng, lnb)


def _odd_in_kernel(h_ref, mod_ref, w_ref, cos_ref, sin_ref, o_ref, *, nB):
    b = pl.program_id(0)
    i = pl.program_id(1)
    row = jnp.where(i == 0, nB, b)
    u = (h_ref[...] * (1.0 + _mod_row(mod_ref, row, 1)) + _mod_row(mod_ref, row, 0)).astype(MM)
    res = jnp.dot(u, w_ref[...], preferred_element_type=F32)
    cs = cos_ref[...]
    sn = sin_ref[...]
    lane = lax.broadcasted_iota(jnp.int32, cs.shape, 1)
    first = (lane % 32) < 16
    for j in range(2 * DA_W // 128):
        zj = res[:, j * 128:(j + 1) * 128]
        sw = jnp.where(first, pltpu.roll(zj, 112, axis=1), pltpu.roll(zj, 16, axis=1))
        rot = zj * cs + sw * sn
        if j < DA_W // 128:
            rot = rot * DA_QSCALE
        o_ref[:, j * 128:(j + 1) * 128] = rot.astype(o_ref.dtype)
    o_ref[:, 2 * DA_W:] = res[:, 2 * DA_W:].astype(o_ref.dtype)


def _odd_in_call(h, mod, w, cos_t, sin_t, nB):
    B, S, _ = h.shape
    R = mod.shape[0]
    return pl.pallas_call(
        functools.partial(_odd_in_kernel, nB=nB),
        grid=(B, S // TM),
        in_specs=[pl.BlockSpec((None, TM, D), lambda b, i: (b, i, 0)),
                  pl.BlockSpec((R, 6 * D), lambda b, i: (0, 0)),
                  pl.BlockSpec((D, 3 * DA_W), lambda b, i: (0, 0)),
                  pl.BlockSpec((TM, 128), lambda b, i: (i, 0)),
                  pl.BlockSpec((TM, 128), lambda b, i: (i, 0))],
        out_specs=pl.BlockSpec((None, TM, 3 * DA_W), lambda b, i: (b, i, 0)),
        out_shape=jax.ShapeDtypeStruct((B, S, 3 * DA_W), MM),
        compiler_params=_params(("parallel", "arbitrary")),
        name="odd_in_proj_rope",
    )(h, mod, w, cos_t, sin_t)


def _rope_tables(L, T):
    nf = 16
    inv = ROPE_THETA ** (-jnp.arange(nf, dtype=F32) / nf)
    t = jnp.arange(T)
    ang_r = (t // GRID_W).astype(F32)[:, None] * inv
    ang_c = (t % GRID_W).astype(F32)[:, None] * inv
    cos64 = jnp.concatenate([jnp.cos(ang_r), jnp.cos(ang_r), jnp.cos(ang_c), jnp.cos(ang_c)], -1)
    sin64 = jnp.concatenate([-jnp.sin(ang_r), jnp.sin(ang_r), -jnp.sin(ang_c), jnp.sin(ang_c)], -1)
    cos_t = jnp.concatenate([jnp.ones((L, 128), F32), jnp.tile(cos64, (1, 2))], 0)
    sin_t = jnp.concatenate([jnp.zeros((L, 128), F32), jnp.tile(sin64, (1, 2))], 0)
    return cos_t, sin_t


def _da_kernel(q_ref, k_ref, v_ref, lq1_ref, lk1_ref, lq2_ref, lk2_ref, sg_ref, o_ref, m_scr, acc_scr, s_scr, va_scr,
               *, L, tk, n_chunks, lam_init):
    i = pl.program_id(2)
    hs = range(DA_NH)
    hl = [slice(128 * hh, 128 * (hh + 1)) for hh in hs]

    @pl.when(i == 0)
    def _():
        for hh in hs:
            va_scr[hh, :, 0:128] = v_ref[:, hl[hh]]
            va_scr[hh, :, 128:256] = jnp.ones((va_scr.shape[1], 128), va_scr.dtype)

    lane = lax.broadcasted_iota(jnp.int32, (DA_TQ, 128), 1)
    lo = lane < 64
    qq = []
    for hh in hs:
        q = q_ref[:, hl[hh]]
        zero = jnp.zeros_like(q)
        qq.append(jnp.concatenate([jnp.where(lo, q, zero), jnp.where(lo, zero, q)], axis=0))

    def rows(j):
        return slice(j * tk, (j + 1) * tk)

    def scores(hh, j):
        return _dot_nt(qq[hh], k_ref[rows(j), hl[hh]])

    def softmax_pv(hh, s_ref, vb, first):
        ncol = s_ref.shape[1] // 128
        mx = s_ref[:, 0:128]
        for cb in range(1, ncol):
            mx = jnp.maximum(mx, s_ref[:, cb * 128:(cb + 1) * 128])
        m_cur = jnp.max(mx, axis=-1, keepdims=True)
        if first:
            m_new = jnp.broadcast_to(m_cur, m_scr.shape[1:])
            acc_scr[hh] = _dot(jnp.exp2(s_ref[...] - m_cur), vb)
        else:
            m_old = m_scr[hh]
            m_new = jnp.maximum(m_old, m_cur)
            alpha = jnp.exp2(m_old - m_new)
            p = jnp.exp2(s_ref[...] - jnp.tile(m_new, (1, ncol)))
            acc_scr[hh] = jnp.tile(alpha, (1, 2)) * acc_scr[hh] + _dot(p, vb)
        m_scr[hh] = m_new

    @pl.when(i == 0)
    def _():
        for hh in hs:
            s_scr[hh, 0, :, 0:L] = _dot_nt(qq[hh], k_ref[0:L, hl[hh]])
        for hh in hs:
            softmax_pv(hh, s_scr.at[hh, 0, :, 0:L], va_scr[hh, 0:L, :], True)

    @pl.when(i > 0)
    def _():
        for hh in hs:
            s_scr[hh, 0] = scores(hh, 0)
        for j in range(n_chunks):
            for hh in hs:
                if j + 1 < n_chunks:
                    s_scr[hh, (j + 1) % DA_NBUF] = scores(hh, j + 1)
                softmax_pv(hh, s_scr.at[hh, j % DA_NBUF], va_scr[hh, rows(j), :], j == 0)

    lam = (jnp.exp(jnp.sum(lq1_ref[...] * lk1_ref[...], axis=-1, keepdims=True))
           - jnp.exp(jnp.sum(lq2_ref[...] * lk2_ref[...], axis=-1, keepdims=True)) + lam_init)
    for hh in hs:
        o_all = acc_scr[hh, :, 0:128] * (1.0 / acc_scr[hh, :, 128:256])
        o = o_all[0:DA_TQ] - lam * o_all[DA_TQ:]
        o = o * lax.rsqrt(jnp.mean(o * o, axis=-1, keepdims=True) + 1e-5) * sg_ref[...] * (1.0 - lam_init)
        o_ref[:, hl[hh]] = o.astype(o_ref.dtype)


def _da_call(qkv, lq1, lk1, lq2, lk2, sg, L, lam_init):
    B, S, _ = qkv.shape
    tk = next(t for t in DA_TK if S % t == 0 and t >= L)
    n_chunks = S // tk
    nh = DA_HEADS // DA_NH
    w = 128 * DA_NH
    full = lambda shape: pl.BlockSpec(shape, lambda b, h, i: (0,) * len(shape))
    return pl.pallas_call(
        functools.partial(_da_kernel, L=L, tk=tk, n_chunks=n_chunks, lam_init=lam_init),
        grid=(B, nh, S // DA_TQ),
        in_specs=[pl.BlockSpec((None, DA_TQ, w), lambda b, h, i: (b, i, h)),
                  pl.BlockSpec((None, S, w), lambda b, h, i: (b, 0, nh + h)),
                  pl.BlockSpec((None, S, w), lambda b, h, i: (b, 0, 2 * nh + h)),
                  full((1, 64)), full((1, 64)), full((1, 64)), full((1, 64)), full((1, 128))],
        out_specs=pl.BlockSpec((None, DA_TQ, w), lambda b, h, i: (b, i, h)),
        out_shape=jax.ShapeDtypeStruct((B, S, DA_W), MM),
        scratch_shapes=[pltpu.VMEM((DA_NH, 2 * DA_TQ, 128), F32), pltpu.VMEM((DA_NH, 2 * DA_TQ, 256), F32),
                        pltpu.VMEM((DA_NH, DA_NBUF, 2 * DA_TQ, tk), F32), pltpu.VMEM((DA_NH, S, 256), MM)],
        compiler_params=_params(("arbitrary", "arbitrary", "arbitrary")),
        name="diff_attention",
    )(qkv, qkv, qkv, lq1, lk1, lq2, lk2, sg)


def _route(logits_t, rb):
    s = _sigmoid(logits_t)
    sel = s + rb
    x = [sel[e:e + 1, :] for e in range(N_EXPERTS)]
    tg = []
    for g in range(N_GROUPS):
        best = None
        for a in range(EPG):
            for b in range(a + 1, EPG):
                pair = x[EPG * g + a] + x[EPG * g + b]
                best = pair if best is None else jnp.maximum(best, pair)
        tg.append(best)
    rows = []
    gsel = []
    for g in range(N_GROUPS):
        chosen = None
        for g2 in range(N_GROUPS):
            if g2 == g:
                continue
            c = (tg[g] > tg[g2]) if g2 < g else (tg[g] >= tg[g2])
            chosen = c if chosen is None else jnp.logical_and(chosen, c)
        for a in range(EPG):
            rank = jnp.zeros_like(x[0])
            for b in range(EPG):
                if b == a:
                    continue
                ahead = (x[EPG * g + b] >= x[EPG * g + a]) if b < a else (x[EPG * g + b] > x[EPG * g + a])
                rank = rank + jnp.where(ahead, 1.0, 0.0)
            picked = jnp.logical_and(chosen, rank < 1.5)
            rows.append(jnp.where(picked, s[EPG * g + a:EPG * g + a + 1, :], 0.0))
        gsel.append(jnp.where(chosen, 1.0, 0.0))
    comb = jnp.concatenate(rows, axis=0)
    return comb * (1.0 / jnp.sum(comb, axis=0, keepdims=True)), jnp.concatenate(gsel, axis=0)


def _router_kernel(h_ref, mod_ref, rwt_ref, rb_ref, u_ref, rt_ref, *, nB, tile0):
    b = pl.program_id(0)
    i = pl.program_id(1)
    row = jnp.where(i + tile0 == 0, nB, b)
    u = h_ref[...] * (1.0 + _mod_row(mod_ref, row, 4)) + _mod_row(mod_ref, row, 3)
    u_ref[...] = u.astype(MM)
    logits_t = lax.dot_general(rwt_ref[...], u, (((1,), (1,)), ((), ())), precision=HIGHEST,
                               preferred_element_type=F32)
    comb_t, gsel_t = _route(logits_t, rb_ref[...])
    pad = jnp.zeros((MOE_RT_ROWS - N_EXPERTS - N_GROUPS, TM), F32)
    rt_ref[...] = jnp.concatenate([comb_t, gsel_t, pad], axis=0)


def _router_call(h, mod, rwt, rb, nB, tile0):
    B, S, _ = h.shape
    R = mod.shape[0]
    nt = S // TM - tile0
    full = lambda shape: pl.BlockSpec(shape, lambda b, i: (0,) * len(shape))
    return pl.pallas_call(
        functools.partial(_router_kernel, nB=nB, tile0=tile0),
        grid=(B, nt),
        in_specs=[pl.BlockSpec((None, TM, D), lambda b, i: (b, i + tile0, 0)),
                  full((R, 6 * D)), full((N_EXPERTS, D)), full((N_EXPERTS, 1))],
        out_specs=[pl.BlockSpec((None, TM, D), lambda b, i: (b, i, 0)),
                   pl.BlockSpec((MOE_RT_ROWS, TM), lambda b, i: (0, b * nt + i))],
        out_shape=[jax.ShapeDtypeStruct((B, nt * TM, D), MM),
                   jax.ShapeDtypeStruct((MOE_RT_ROWS, B * nt * TM), F32)],
        compiler_params=_params(("parallel", "arbitrary")),
        name="moe_router",
    )(h, mod, rwt, rb)


def _expert_kernel(cnt_ref, u_ref, rt_ref, wg_ref, wu_ref, wd_ref, f_ref, xs_scr, cs_scr, fs_scr, pos_scr,
                   *, TE, NP, nt):
    t = pl.program_id(0)
    g = pl.program_id(1)
    lane = lax.broadcasted_iota(jnp.int32, (MOE_BLK, 128), 1)
    for hf in range(NP):
        base = hf * TE
        n = [cnt_ref[gg * nt + t * NP + hf] for gg in range(N_GROUPS)]
        off = [jnp.int32(0)]
        for gg in range(N_GROUPS - 1):
            off.append(off[-1] + n[gg])

        @pl.when(g == 0)
        def _(base=base, off=off):
            rt = rt_ref[:, base:base + TE]
            gs = rt[N_EXPERTS:N_EXPERTS + 8, :]
            r_i = lax.broadcasted_iota(jnp.int32, (TE, TE), 0)
            c_i = lax.broadcasted_iota(jnp.int32, (TE, TE), 1)
            before = jnp.where(r_i < c_i, 1.0, 0.0).astype(MM)
            cnt_before = jnp.dot(gs.astype(MM), before, preferred_element_type=F32)
            pos = jnp.zeros((1, TE), F32)
            for gg in range(N_GROUPS):
                pos = pos + gs[gg:gg + 1, :] * (off[gg].astype(F32) + cnt_before[gg:gg + 1, :])
            perm = jnp.where(r_i.astype(F32) == pos, 1.0, 0.0).astype(MM)
            xs_scr[base:base + TE, :] = jnp.dot(perm, u_ref[base:base + TE, :],
                                                preferred_element_type=F32).astype(MM)
            rt_pad = jnp.concatenate([rt, jnp.zeros((128 - MOE_RT_ROWS, TE), F32)], axis=0)
            hi, lo = _split(rt_pad)
            cs_scr[base:base + TE, :] = _dot_nt(perm, hi) + _dot_nt(perm, lo)
            pos_scr[base:base + TE, :] = jnp.broadcast_to(pos, (128, TE)).T
            fs_scr[base:base + TE, :] = jnp.zeros((TE, D), F32)

        lo_g = jnp.int32(0)
        n_g = jnp.int32(0)
        for gg in range(N_GROUPS):
            lo_g = jnp.where(g == gg, off[gg], lo_g)
            n_g = jnp.where(g == gg, n[gg], n_g)
        hi_g = lo_g + n_g
        for rb in range(TE // MOE_BLK):
            @pl.when(jnp.logical_and(lo_g < MOE_BLK * (rb + 1), hi_g > MOE_BLK * rb))
            def _(r0=base + rb * MOE_BLK):
                rows = slice(r0, r0 + MOE_BLK)
                xb = xs_scr[rows, :]
                c = cs_scr[rows, :]
                hg = [jnp.dot(xb, wg_ref[e], preferred_element_type=F32) for e in range(EPG)]
                hu = [jnp.dot(xb, wu_ref[e], preferred_element_type=F32) for e in range(EPG)]
                col = [jnp.sum(jnp.where(lane == EPG * g + e, c, 0.0), axis=-1, keepdims=True) for e in range(EPG)]
                he = jnp.concatenate([(hg[e] * _sigmoid(hg[e]) * hu[e] * col[e]).astype(MM) for e in range(EPG)],
                                     axis=1)
                fs_scr[rows, :] += jnp.dot(he, wd_ref[...], preferred_element_type=F32)

        @pl.when(g == N_GROUPS - 1)
        def _(base=base):
            c_i = lax.broadcasted_iota(jnp.int32, (TE, TE), 1)
            pos_t = jnp.tile(pos_scr[base:base + TE, :], (1, TE // 128))
            unperm = jnp.where(c_i.astype(F32) == pos_t, 1.0, 0.0).astype(MM)
            f_ref[base:base + TE, :] = jnp.dot(unperm, fs_scr[base:base + TE, :].astype(MM),
                                               preferred_element_type=F32).astype(f_ref.dtype)


def _expert_call(u, rt, wg, wu, wd):
    N = u.shape[0]
    TE = MOE_TE if N % MOE_TE == 0 else TM
    nt = N // TE
    NP = MOE_NP if nt % MOE_NP == 0 else 1
    cnt = jnp.sum(rt[N_EXPERTS:N_EXPERTS + N_GROUPS].reshape(N_GROUPS, nt, TE), axis=-1)
    cnt = cnt.astype(jnp.int32).reshape(N_GROUPS * nt)
    gw = EPG * D_EXPERT
    grid_spec = pltpu.PrefetchScalarGridSpec(
        num_scalar_prefetch=1,
        grid=(nt // NP, N_GROUPS),
        in_specs=[pl.BlockSpec((NP * TE, D), lambda t, g, c: (t, 0)),
                  pl.BlockSpec((MOE_RT_ROWS, NP * TE), lambda t, g, c: (0, t)),
                  pl.BlockSpec((None, EPG, D, D_EXPERT), lambda t, g, c: (g, 0, 0, 0)),
                  pl.BlockSpec((None, EPG, D, D_EXPERT), lambda t, g, c: (g, 0, 0, 0)),
                  pl.BlockSpec((None, gw, D), lambda t, g, c: (g, 0, 0))],
        out_specs=pl.BlockSpec((NP * TE, D), lambda t, g, c: (t, 0)),
        scratch_shapes=[pltpu.VMEM((NP * TE, D), MM), pltpu.VMEM((NP * TE, 128), F32),
                        pltpu.VMEM((NP * TE, D), F32), pltpu.VMEM((NP * TE, 128), F32)])
    return pl.pallas_call(
        functools.partial(_expert_kernel, TE=TE, NP=NP, nt=nt),
        grid_spec=grid_spec,
        out_shape=jax.ShapeDtypeStruct((N, D), MM),
        compiler_params=_params(("parallel", "arbitrary")),
        name="moe_experts",
    )(cnt, u, rt, wg, wu, wd)


def _moe_ln_kernel(h_ref, mod_ref, f_ref, lng_ref, lnb_ref, o_ref, *, nB, tile0):
    b = pl.program_id(0)
    i = pl.program_id(1)
    row = jnp.where(i + tile0 == 0, nB, b)
    _res_ln(h_ref, mod_ref, row, 5, f_ref[...].astype(F32), lng_ref, lnb_ref, o_ref)


def _moe_ln_call(h, mod, f, lng, lnb, nB, tile0):
    B, S, _ = h.shape
    R = mod.shape[0]
    nt = S // TM - tile0
    full = lambda shape: pl.BlockSpec(shape, lambda b, i: (0,) * len(shape))
    return pl.pallas_call(
        functools.partial(_moe_ln_kernel, nB=nB, tile0=tile0),
        grid=(B, nt),
        in_specs=[pl.BlockSpec((None, TM, D), lambda b, i: (b, i + tile0, 0)),
                  full((R, 6 * D)),
                  pl.BlockSpec((None, TM, D), lambda b, i: (b, i, 0)),
                  full((1, D)), full((1, D))],
        out_specs=pl.BlockSpec((None, TM, D), lambda b, i: (b, i, 0)),
        out_shape=jax.ShapeDtypeStruct((B, nt * TM, D), F32),
        compiler_params=_params(("parallel", "arbitrary")),
        name="moe_res_ln",
    )(h, mod, f, lng, lnb)


def _moe_call(h, mod, rwt, rb, wg, wu, wd, lng, lnb, nB, latent_only, L):
    B = h.shape[0]
    tile0 = L // TM if latent_only else 0
    u, rt = _router_call(h, mod, rwt, rb, nB, tile0)
    f = _expert_call(u.reshape(-1, D), rt, wg.reshape(N_GROUPS, EPG, D, D_EXPERT),
                     wu.reshape(N_GROUPS, EPG, D, D_EXPERT), wd.reshape(N_GROUPS, EPG * D_EXPERT, D))
    return _moe_ln_call(h, mod, f.reshape(B, -1, D), lng, lnb, nB, tile0)


def kernel(x, c, ctx, c_ctx, ada_w, ada_b, ln_g, ln_b, even_w_in, even_w_out, shift_w, na_rpb, rw_w0, rw_w_up, rw_a0, rw_a_up, rw_g_up, rw_k_k, rw_k_a, rw_r_k, rw_gn_g, rw_gn_b, odd_w_in, odd_w_out, da_lq1, da_lk1, da_lq2, da_lk2, da_subln_g, router_w, router_b, exp_w_gate, exp_w_up, exp_w_down):
    B, T, _ = x.shape
    L = ctx.shape[1]
    assert L == TM and T % TM == 0 and (T // GRID_W) >= 12
    rows = T // GRID_W
    R = -(-(B + 1) // 8) * 8

    cv = jnp.zeros((R, D), F32).at[:B].set(c).at[B].set(c_ctx)
    mods = _ada_call(cv, ada_w, ada_b)

    cos_t, sin_t = _rope_tables(L, T)
    rwt = router_w.T
    rb = router_b.reshape(N_EXPERTS, 1)
    zpad = jnp.zeros((2, 64, RW_W), F32)

    h = jnp.concatenate([ctx, x], axis=1)
    for l in range(DEPTH):
        mod = mods[l]
        i = l // 2
        lng = ln_g[l].reshape(2, 1, D)
        lnb = ln_b[l].reshape(2, 1, D)
        if l % 2 == 0:
            w_in = even_w_in[i].astype(MM)
            qkv, prw = _even_in_call(h, mod, w_in[:, :3 * NA_W], w_in[:, 3 * NA_W:], B)
            ona = _na_call(qkv, _na_bias(na_rpb[i], rows), L)
            yf, yr, bg = _rw_call(
                prw, shift_w[i], rw_w0[i].reshape(2, 1, RW_W),
                jnp.concatenate([rw_w_up[i], zpad], axis=1).astype(MM), rw_a0[i].reshape(2, 1, RW_W),
                jnp.concatenate([zpad, rw_a_up[i]], axis=1).astype(MM), rw_g_up[i].astype(MM),
                rw_k_k[i].reshape(1, RW_W), rw_k_a[i].reshape(1, RW_W), rw_r_k[i].reshape(1, RW_W), L)
            h = _even_out_call(h, mod, ona, yf, yr, bg, rw_gn_g[i].reshape(1, RW_W), rw_gn_b[i].reshape(1, RW_W),
                               even_w_out[i].astype(MM), lng[0], lnb[0], B)
        else:
            lam_init = 0.8 - 0.6 * math.exp(-0.3 * l)
            qkv = _odd_in_call(h, mod, odd_w_in[i].astype(MM), cos_t, sin_t, B)
            oda = _da_call(qkv, da_lq1[i].reshape(1, 64), da_lk1[i].reshape(1, 64), da_lq2[i].reshape(1, 64),
                           da_lk2[i].reshape(1, 64), da_subln_g[i].reshape(1, 128), L, lam_init)
            h = _odd_out_call(h, mod, oda, odd_w_out[i].astype(MM), lng[0], lnb[0], B)
        h = _moe_call(h, mod, rwt, rb, exp_w_gate[l].astype(MM), exp_w_up[l].astype(MM), exp_w_down[l].astype(MM),
                      lng[1], lnb[1], B, l == DEPTH - 1, L)
    return h
```

```python
import functools
import math

import numpy as np
import jax
import jax.numpy as jnp
from jax import lax
from jax.experimental import pallas as pl
from jax.experimental.pallas import tpu as pltpu

F32 = jnp.float32
MM = jnp.bfloat16
HIGHEST = lax.Precision.HIGHEST

D = 1024
DEPTH = 4
GRID_W = 64
NA_HEADS = 8
NA_W = 512
NA_WIN_R = 8
NA_WIN_C = 16
NA_QB = 128
NA_NB = 2
NA_KROWS = 9
NA_NLOC = NA_KROWS * GRID_W
RW_W = 512
RW_HEAD = 64
RW_IN = 1792
RW_GN_EPS = 64e-5
RW_CHUNK = 64
RW_GRP = 256
RW_NSUB = 2
RW_INV_BASE = 8
RW_INV_LEVELS = 3
RW_NMASK = 7 + RW_INV_LEVELS
DA_HEADS = 8
DA_W = 1024
ROPE_THETA = 10000.0
N_EXPERTS = 16
N_GROUPS = 4
EPG = 4
D_EXPERT = 512
MOE_RT_ROWS = 32
MOE_TE = 512
MOE_NP = 2
MOE_BLK = 128
ALPHA = (2 * DEPTH) ** 0.25
LN_EPS = 1e-5
TM = 256
TM_BIG = 768
DA_TQ = 256
DA_NH = 2
DA_TK = (768, 512, 256)
DA_NBUF = 3
DA_QSCALE = 0.125 * math.log2(math.e)
NEG = -1e30
VMEM_LIMIT = 56 * 1024 * 1024


def _dot(a, b):
    return jnp.dot(a.astype(MM), b.astype(MM), preferred_element_type=F32)


def _dot_nt(a, b):
    return lax.dot_general(a.astype(MM), b.astype(MM), (((1,), (1,)), ((), ())), preferred_element_type=F32)


def _split(x):
    hi = x.astype(MM)
    lo = (x - hi.astype(F32)).astype(MM)
    return hi, lo


def _dot_split_lhs(x, w):
    hi, lo = _split(x)
    return jnp.dot(hi, w, preferred_element_type=F32) + jnp.dot(lo, w, preferred_element_type=F32)


def _dot_split_rhs(w, x):
    hi, lo = _split(x)
    return jnp.dot(w, hi, preferred_element_type=F32) + jnp.dot(w, lo, preferred_element_type=F32)


def _sigmoid(x):
    return 1.0 / (1.0 + jnp.exp(-x))


def _params(sem):
    return pltpu.CompilerParams(dimension_semantics=sem, vmem_limit_bytes=VMEM_LIMIT)


def _mod_row(mod_ref, row, j):
    return mod_ref[pl.ds(row, 1), j * D:(j + 1) * D]


def _mod_tile(mod_ref, b, i, j, nB, tm, L, tile0=0):
    if L % tm == 0:
        return _mod_row(mod_ref, jnp.where(i + tile0 < L // tm, nB, b), j)
    assert tile0 == 0 and tm > L
    rid = lax.broadcasted_iota(jnp.int32, (tm, 1), 0)
    is_ctx = jnp.logical_and(i == 0, rid < L)
    return jnp.where(is_ctx, _mod_row(mod_ref, nB, j), _mod_row(mod_ref, b, j))


def _row_tile(S):
    return TM_BIG if S % TM_BIG == 0 else TM


def _layer_norm(z, g, b):
    mu = jnp.mean(z, axis=-1, keepdims=True)
    zc = z - mu
    var = jnp.mean(zc * zc, axis=-1, keepdims=True)
    return zc * lax.rsqrt(var + LN_EPS) * g + b


def _ada_kernel(cv_ref, w_ref, b_ref, o_ref):
    x = cv_ref[...]
    x = x * _sigmoid(x)
    o_ref[...] = jnp.dot(x, w_ref[...], precision=HIGHEST, preferred_element_type=F32) + b_ref[...]


def _ada_call(cv, ada_w, ada_b):
    R = cv.shape[0]
    tn = 1536
    return pl.pallas_call(
        _ada_kernel,
        grid=(DEPTH, 6 * D // tn),
        in_specs=[pl.BlockSpec((R, D), lambda l, j: (0, 0)),
                  pl.BlockSpec((None, D, tn), lambda l, j: (l, 0, j)),
                  pl.BlockSpec((None, 1, tn), lambda l, j: (l, 0, j))],
        out_specs=pl.BlockSpec((None, R, tn), lambda l, j: (l, 0, j)),
        out_shape=jax.ShapeDtypeStruct((DEPTH, R, 6 * D), F32),
        compiler_params=_params(("arbitrary", "arbitrary")),
        name="ada_mod",
    )(cv, ada_w, ada_b.reshape(DEPTH, 1, 6 * D))


def _even_in_kernel(h_ref, mod_ref, wna_ref, wrw_ref, qkv_ref, prw_ref, *, nB, L):
    b = pl.program_id(0)
    i = pl.program_id(1)
    mt = functools.partial(_mod_tile, mod_ref, b, i, nB=nB, tm=h_ref.shape[0], L=L)
    u = (h_ref[...] * (1.0 + mt(1)) + mt(0)).astype(MM)
    res = jnp.dot(u, wna_ref[...], preferred_element_type=F32)
    qkv_ref[:, 0:NA_W] = (res[:, 0:NA_W] * 0.125).astype(qkv_ref.dtype)
    qkv_ref[:, NA_W:] = res[:, NA_W:].astype(qkv_ref.dtype)
    prw_ref[...] = jnp.dot(u, wrw_ref[...], preferred_element_type=F32)


def _even_in_call(h, mod, w_na, w_rw, nB, L):
    B, S, _ = h.shape
    R = mod.shape[0]
    tm = _row_tile(S)
    return pl.pallas_call(
        functools.partial(_even_in_kernel, nB=nB, L=L),
        grid=(B, S // tm),
        in_specs=[pl.BlockSpec((None, tm, D), lambda b, i: (b, i, 0)),
                  pl.BlockSpec((R, 6 * D), lambda b, i: (0, 0)),
                  pl.BlockSpec((D, 3 * NA_W), lambda b, i: (0, 0)),
                  pl.BlockSpec((D, RW_IN), lambda b, i: (0, 0))],
        out_specs=[pl.BlockSpec((None, tm, 3 * NA_W), lambda b, i: (b, i, 0)),
                   pl.BlockSpec((None, tm, RW_IN), lambda b, i: (b, i, 0))],
        out_shape=[jax.ShapeDtypeStruct((B, S, 3 * NA_W), MM),
                   jax.ShapeDtypeStruct((B, S, RW_IN), F32)],
        compiler_params=_params(("parallel", "arbitrary")),
        name="even_in_proj",
    )(h, mod, w_na, w_rw)


def _na_bias(rpb, rows):
    cq = np.arange(GRID_W)[:, None]
    ck = np.arange(GRID_W)[None, :]
    cs = np.clip(cq - NA_WIN_C // 2, 0, GRID_W - NA_WIN_C)
    ok_c = (ck >= cs) & (ck < cs + NA_WIN_C)
    dc = np.clip(ck - cq + NA_WIN_C - 1, 0, 2 * NA_WIN_C - 2)
    onehot = np.zeros((2 * NA_WIN_C - 1, GRID_W * GRID_W), np.float32)
    onehot[dc.ravel(), np.arange(GRID_W * GRID_W)] = 1.0
    rc = jnp.einsum('hrd,dn->hrn', rpb, jnp.asarray(onehot), precision=HIGHEST)
    rc = rc.reshape(rpb.shape[0], 2 * NA_WIN_R - 1, GRID_W, GRID_W)
    cases = []
    for r0 in (0, 2, 4, rows - 4, rows - 2):
        ws = int(np.clip(r0 - NA_WIN_R // 2, 0, rows - NA_KROWS))
        qrows = []
        for qi in range(NA_QB // GRID_W):
            rq = r0 + qi
            rs = int(np.clip(rq - NA_WIN_R // 2, 0, rows - NA_WIN_R))
            pieces = []
            for j in range(NA_KROWS):
                rk = ws + j
                if rs <= rk < rs + NA_WIN_R:
                    pieces.append(jnp.where(ok_c[None], rc[:, rk - rq + NA_WIN_R - 1], NEG))
                else:
                    pieces.append(jnp.full((rpb.shape[0], GRID_W, GRID_W), NEG, F32))
            qrows.append(jnp.concatenate(pieces, axis=-1))
        cases.append(jnp.concatenate(qrows, axis=-2))
    return jnp.stack(cases, axis=0)


def _na_kernel(q_ref, k_ref, v_ref, *rest, L, rows):
    bias_refs, o_ref = rest[:NA_NB], rest[NA_NB]
    i = pl.program_id(2)
    nq = L // (NA_NB * NA_QB)
    lane = lax.broadcasted_iota(jnp.int32, (NA_QB, 128), 1)
    lo = lane < 64
    kc = k_ref[0:L, :]
    vc = v_ref[0:L, :]
    ch = [(n, hh) for n in range(NA_NB) for hh in range(2)]
    qs = [q_ref[n * NA_QB:(n + 1) * NA_QB, :] for n in range(NA_NB)]
    qm = {(n, hh): jnp.where(lo if hh == 0 else jnp.logical_not(lo), qs[n], jnp.zeros_like(qs[n])) for n, hh in ch}

    def store(outs):
        for n in range(NA_NB):
            o_ref[n * NA_QB:(n + 1) * NA_QB, :] = jnp.where(lo, outs[(n, 0)], outs[(n, 1)]).astype(o_ref.dtype)

    @pl.when(i < nq)
    def _():
        s = {c: _dot_nt(qm[c], kc) for c in ch}
        m = {c: jnp.max(s[c], axis=-1, keepdims=True) for c in ch}
        p = {c: jnp.exp(s[c] - m[c]) for c in ch}
        l = {c: jnp.sum(p[c], axis=-1, keepdims=True) for c in ch}
        store({c: _dot(p[c], vc) * (1.0 / l[c]) for c in ch})

    @pl.when(i >= nq)
    def _():
        kl, vl = [], []
        for n in range(NA_NB):
            r0 = 2 * ((i - nq) * NA_NB + n)
            ws = jnp.clip(r0 - NA_WIN_R // 2, 0, rows - NA_KROWS)
            start = pl.multiple_of(L + GRID_W * ws, GRID_W)
            kl.append(k_ref[pl.ds(start, NA_NLOC), :])
            vl.append(v_ref[pl.ds(start, NA_NLOC), :])
        s_loc = {c: _dot_nt(qm[c], kl[c[0]]) + bias_refs[c[0]][c[1]] for c in ch}
        s_ctx = {c: _dot_nt(qm[c], kc) for c in ch}
        m = {c: jnp.maximum(jnp.max(s_loc[c], axis=-1, keepdims=True), jnp.max(s_ctx[c], axis=-1, keepdims=True))
             for c in ch}
        p_loc = {c: jnp.exp(s_loc[c] - m[c]) for c in ch}
        p_ctx = {c: jnp.exp(s_ctx[c] - m[c]) for c in ch}
        l = {c: jnp.sum(p_loc[c], axis=-1, keepdims=True) + jnp.sum(p_ctx[c], axis=-1, keepdims=True) for c in ch}
        store({c: (_dot(p_loc[c], vl[c[0]]) + _dot(p_ctx[c], vc)) * (1.0 / l[c]) for c in ch})


def _na_call(qkv, bias, L):
    B, S, _ = qkv.shape
    rows = (S - L) // GRID_W
    tq = NA_NB * NA_QB
    assert L % tq == 0 and (S - L) % tq == 0
    nq = L // tq
    nhp = NA_HEADS // 2

    def bias_idx(n):
        def idx(b, hp, i):
            r0 = 2 * ((i - nq) * NA_NB + n)
            c = jnp.where(r0 == 0, 0,
                          jnp.where(r0 == 2, 1, jnp.where(r0 == rows - 4, 3, jnp.where(r0 == rows - 2, 4, 2))))
            return (jnp.where(i < nq, 2, c), hp, 0, 0)
        return idx

    return pl.pallas_call(
        functools.partial(_na_kernel, L=L, rows=rows),
        grid=(B, nhp, S // tq),
        in_specs=[pl.BlockSpec((None, tq, 128), lambda b, hp, i: (b, i, hp)),
                  pl.BlockSpec((None, S, 128), lambda b, hp, i: (b, 0, nhp + hp)),
                  pl.BlockSpec((None, S, 128), lambda b, hp, i: (b, 0, 2 * nhp + hp))]
                 + [pl.BlockSpec((None, 2, NA_QB, NA_NLOC), bias_idx(n)) for n in range(NA_NB)],
        out_specs=pl.BlockSpec((None, tq, 128), lambda b, hp, i: (b, i, hp)),
        out_shape=jax.ShapeDtypeStruct((B, S, NA_W), MM),
        compiler_params=_params(("parallel", "parallel", "arbitrary")),
        name="na_attention",
    )(qkv, qkv, qkv, *([bias] * NA_NB))


def _rw_conv(p_ref, prev_ref, next_ref, c, conv_ref, nL, NC):
    p = p_ref[...]
    at_start = jnp.logical_or(c == 0, c == nL)
    at_end = jnp.logical_or(c == nL - 1, c == NC - 1)
    prow = jnp.where(at_start, 0.0, prev_ref[7:8, :])
    nrow = jnp.where(at_end, 0.0, next_ref[0:1, :])
    rid = lax.broadcasted_iota(jnp.int32, p.shape, 0)
    last = p.shape[0] - 1
    pm = jnp.where(rid == 0, prow, pltpu.roll(p, 1, axis=0))
    pp = jnp.where(rid == last, nrow, pltpu.roll(p, last, axis=0))
    return pm * conv_ref[0:1, :] + p * conv_ref[1:2, :] + pp * conv_ref[2:3, :]


def _rw_expand(z, bd):
    return jnp.where(bd, jnp.concatenate([z, z, z, z], axis=0), 0.0)


def _rw_prep(pc, d, w0_ref, wup_ref, a0_ref, aup_ref, kk_ref, ka_ref, ones_ref, cum_ref):
    r = pc[:, 0:RW_W]
    k = pc[:, RW_W:2 * RW_W]
    v = pc[:, 2 * RW_W:3 * RW_W]
    xwa = pc[:, 3 * RW_W:3 * RW_W + 128]
    z = w0_ref[d] + _dot(jnp.tanh(xwa), wup_ref[d])
    lw = -math.exp(-0.5) * _sigmoid(z)
    a = _sigmoid(a0_ref[d] + _dot(xwa, aup_ref[d]))
    kk = k * kk_ref[...]
    n2 = _dot_split_lhs(kk * kk, ones_ref[...])
    kk = kk * (1.0 / jnp.maximum(jnp.sqrt(n2), 1e-12))
    keff = k * (1.0 + (a - 1.0) * ka_ref[...])
    av = -kk
    bv = kk * a
    lc = _dot_split_rhs(cum_ref[d], lw)
    ltot = lc[RW_CHUNK - 1:RW_CHUNK, :] if d == 0 else lc[0:1, :]
    at = av * jnp.exp(lc - lw)
    rt = r * jnp.exp(lc)
    ginv = jnp.exp(-lc)
    bt = bv * ginv
    kt = keff * ginv
    grest = jnp.exp(ltot - lc)
    bh = bv * grest
    kh = keff * grest
    gc = jnp.exp(ltot)

    return dict(at=at, rt=rt, bt=bt, kt=kt, bh=bh, kh=kh, v=v, gc=gc, keff=keff, r=r)


def _rw_chains(preps, mask_ref, g_scr):
    bd = mask_ref[0] > 0.5
    bdm = mask_ref[0].astype(MM)
    eye = mask_ref[1]
    ngrp = RW_W // RW_GRP
    order = {0: list(range(RW_NSUB)), 1: list(range(RW_NSUB - 1, -1, -1))}
    chains = [(d, g, order[d][n]) for n in range(RW_NSUB) for d in range(2) for g in range(ngrp)]
    ex = {}
    for c in chains:
        d, g, slot = c
        sl = slice(g * RW_GRP, (g + 1) * RW_GRP)
        p = preps[(d, slot)]
        e = {}
        for n in ("at", "rt", "bt", "kt", "bh", "kh", "v"):
            z = p[n][:, sl].astype(MM)
            e[n] = jnp.concatenate([z, z, z, z], axis=0) * bdm
        e["vt"] = e["v"].T
        e["gc"] = p["gc"][:, sl]
        ex[c] = e
    xab, xrb, xak, xrk = {}, {}, {}, {}
    for c in chains:
        ar = jnp.concatenate([ex[c]["at"], ex[c]["rt"]], axis=0)
        ms = mask_ref[2 + 2 * c[0]]
        mi = mask_ref[3 + 2 * c[0]]
        xb = _dot_nt(ex[c]["bt"], ar)
        xab[c] = (xb[:, 0:RW_GRP] * ms).astype(MM)
        xrb[c] = (xb[:, RW_GRP:] * mi).astype(MM)
        xk = _dot_nt(ex[c]["kt"], ar)
        xak[c] = (xk[:, 0:RW_GRP] * ms).astype(MM)
        xrk[c] = (xk[:, RW_GRP:] * mi).astype(MM)
    eyeb = eye.astype(MM)
    xp = {c: xab[c] * mask_ref[6].astype(MM) for c in chains}
    tt = {c: eyeb + xp[c] for c in chains}
    for _ in range(RW_INV_BASE.bit_length() - 2):
        xp = {c: _dot(xp[c], xp[c]).astype(MM) for c in chains}
        tt = {c: _dot(tt[c], eyeb + xp[c]).astype(MM) for c in chains}
    for lvl in range(RW_INV_LEVELS):
        cm = mask_ref[7 + lvl].astype(MM)
        tc = {c: _dot(tt[c], xab[c] * cm).astype(MM) for c in chains}
        tt = {c: _dot(eyeb + tc[c], tt[c]).astype(MM) for c in chains}
    ys = {}
    for n in range(RW_NSUB):
        cur = [(d, g, order[d][n]) for d in range(2) for g in range(ngrp)]
        gs = {c: g_scr[c[0], c[1]] for c in cur}
        gsb = {c: gs[c].astype(MM) for c in cur}
        w1 = {c: (_dot_nt(gsb[c], ex[c]["at"]) + _dot(ex[c]["vt"], xak[c])).astype(MM) for c in cur}
        ut = {c: _dot(w1[c], tt[c]).astype(MM) for c in cur}
        yt = {c: _dot_nt(gsb[c], ex[c]["rt"]) + _dot(ut[c], xrb[c]) + _dot(ex[c]["vt"], xrk[c]) for c in cur}
        for c in cur:
            g_scr[c[0], c[1]] = gs[c] * ex[c]["gc"] + _dot(ut[c], ex[c]["bh"]) + _dot(ex[c]["vt"], ex[c]["kh"])
        for c in cur:
            ybd = yt[c].T
            ys[c] = ybd[0:64] + ybd[64:128] + ybd[128:192] + ybd[192:256]
    return [jnp.concatenate([jnp.concatenate([ys[(d, g, slot)] for g in range(ngrp)], axis=1)
                             for slot in range(RW_NSUB)], axis=0) for d in range(2)]


def _rw_kernel(pf_ref, pfp_ref, pfn_ref, pr_ref, prp_ref, prn_ref, conv_ref, w0_ref, wup_ref, a0_ref, aup_ref,
               gup_ref, kk_ref, ka_ref, rk_ref, ones_ref, cum_ref, mask_ref, yf_ref, yr_ref, bg_ref, g_scr,
               *, nL, NC):
    i = pl.program_id(1)

    @pl.when(i == 0)
    def _():
        g_scr[...] = jnp.zeros_like(g_scr)

    cf = i
    cr = jnp.where(i < nL, nL - 1 - i, NC - 1 - i + nL)
    args = (w0_ref, wup_ref, a0_ref, aup_ref, kk_ref, ka_ref, ones_ref, cum_ref)

    pcf = _rw_conv(pf_ref, pfp_ref, pfn_ref, cf, conv_ref, nL, NC)
    pcr = _rw_conv(pr_ref, prp_ref, prn_ref, cr, conv_ref, nL, NC)
    preps = {}
    for n in range(RW_NSUB):
        sf, sr = n, RW_NSUB - 1 - n
        preps[(0, sf)] = _rw_prep(pcf[sf * RW_CHUNK:(sf + 1) * RW_CHUNK], 0, *args)
        preps[(1, sr)] = _rw_prep(pcr[sr * RW_CHUNK:(sr + 1) * RW_CHUNK], 1, *args)
    xwa = pcf[:, 3 * RW_W:3 * RW_W + 128]
    a_r = _sigmoid(a0_ref[1] + _dot(xwa, aup_ref[1]))
    keff_r = pcf[:, RW_W:2 * RW_W] * (1.0 + (a_r - 1.0) * ka_ref[...])
    keff_f = jnp.concatenate([preps[(0, s)]["keff"] for s in range(RW_NSUB)], axis=0)
    rv = pcf[:, 0:RW_W] * rk_ref[...]
    bsum = _dot_split_lhs(rv * (keff_f + keff_r), ones_ref[...]) * pcf[:, 2 * RW_W:3 * RW_W]
    bg_ref[:, 0:RW_W] = bsum
    bg_ref[:, RW_W:] = _dot(_sigmoid(pcf[:, 3 * RW_W + 128:]), gup_ref[...])

    y_f, y_r = _rw_chains(preps, mask_ref, g_scr)
    yf_ref[...] = y_f
    yr_ref[...] = y_r


def _rw_masks():
    n = 4 * RW_CHUNK
    idx = np.arange(n)
    hd = idx // RW_CHUNK
    t = idx % RW_CHUNK
    bd = hd[:, None] == hd[None, :]
    m = np.zeros((RW_NMASK, n, n), np.float32)
    m[0] = bd
    m[1] = np.eye(n)
    m[2] = bd & (t[:, None] < t[None, :])
    m[3] = bd & (t[:, None] <= t[None, :])
    m[4] = bd & (t[:, None] > t[None, :])
    m[5] = bd & (t[:, None] >= t[None, :])
    same = lambda w: bd & ((t[:, None] // w) == (t[None, :] // w))
    m[6] = same(RW_INV_BASE)
    for lvl in range(RW_INV_LEVELS):
        w = RW_INV_BASE << lvl
        m[7 + lvl] = same(2 * w) & ~same(w)
    tt = np.arange(RW_CHUNK)
    cum = np.stack([tt[None, :] <= tt[:, None], tt[None, :] >= tt[:, None]]).astype(np.float32)
    ch = np.arange(RW_W) // RW_HEAD
    ones = (ch[:, None] == ch[None, :]).astype(np.float32)
    return m, cum, ones


def _rw_call(prw, conv_w, w0, wup_pad, a0, aup_pad, gup, kk, ka, rk, L):
    B, S, _ = prw.shape
    rb = RW_NSUB * RW_CHUNK
    assert L % rb == 0 and S % rb == 0
    NC = S // rb
    nL = L // rb
    n8 = S // 8
    r8 = rb // 8
    m, cum, ones = _rw_masks()

    def cr_of(i):
        return jnp.where(i < nL, nL - 1 - i, NC - 1 - i + nL)

    full = lambda shape: pl.BlockSpec(shape, lambda b, i: (0,) * len(shape))
    in_specs = [
        pl.BlockSpec((None, rb, RW_IN), lambda b, i: (b, i, 0)),
        pl.BlockSpec((None, 8, RW_IN), lambda b, i: (b, jnp.maximum(i * r8 - 1, 0), 0)),
        pl.BlockSpec((None, 8, RW_IN), lambda b, i: (b, jnp.minimum(i * r8 + r8, n8 - 1), 0)),
        pl.BlockSpec((None, rb, RW_IN), lambda b, i: (b, cr_of(i), 0)),
        pl.BlockSpec((None, 8, RW_IN), lambda b, i: (b, jnp.maximum(cr_of(i) * r8 - 1, 0), 0)),
        pl.BlockSpec((None, 8, RW_IN), lambda b, i: (b, jnp.minimum(cr_of(i) * r8 + r8, n8 - 1), 0)),
        full((3, RW_IN)), full((2, 1, RW_W)), full((2, 128, RW_W)), full((2, 1, RW_W)), full((2, 128, RW_W)),
        full((128, RW_W)), full((1, RW_W)), full((1, RW_W)), full((1, RW_W)),
        full((RW_W, RW_W)), full((2, RW_CHUNK, RW_CHUNK)), full((RW_NMASK, 4 * RW_CHUNK, 4 * RW_CHUNK)),
    ]
    out_specs = [
        pl.BlockSpec((None, rb, RW_W), lambda b, i: (b, i, 0)),
        pl.BlockSpec((None, rb, RW_W), lambda b, i: (b, cr_of(i), 0)),
        pl.BlockSpec((None, rb, 2 * RW_W), lambda b, i: (b, i, 0)),
    ]
    return pl.pallas_call(
        functools.partial(_rw_kernel, nL=nL, NC=NC),
        grid=(B, NC),
        in_specs=in_specs,
        out_specs=out_specs,
        out_shape=[jax.ShapeDtypeStruct((B, S, RW_W), F32),
                   jax.ShapeDtypeStruct((B, S, RW_W), F32),
                   jax.ShapeDtypeStruct((B, S, 2 * RW_W), F32)],
        scratch_shapes=[pltpu.VMEM((2, RW_W // RW_GRP, RW_GRP, RW_GRP), F32)],
        compiler_params=_params(("parallel", "arbitrary")),
        name="rwkv7_chunked",
    )(prw, prw, prw, prw, prw, prw, conv_w, w0, wup_pad, a0, aup_pad, gup, kk, ka, rk,
      jnp.asarray(ones, MM), jnp.asarray(cum, MM), jnp.asarray(m))


def _res_ln(h_ref, gate, y, lng_ref, lnb_ref, o_ref):
    z = ALPHA * h_ref[...] + gate * y
    o_ref[...] = _layer_norm(z, lng_ref[...], lnb_ref[...])


def _even_out_kernel(h_ref, mod_ref, ona_ref, yf_ref, yr_ref, bg_ref, ones_ref, gng_ref, gnb_ref, wo_ref,
                     lng_ref, lnb_ref, o_ref, *, nB, L):
    b = pl.program_id(0)
    i = pl.program_id(1)
    gate = _mod_tile(mod_ref, b, i, 2, nB, h_ref.shape[0], L)
    y = yf_ref[...] + yr_ref[...]
    mu = _dot_split_lhs(y, ones_ref[...]) * (1.0 / RW_HEAD)
    yc = y - mu
    var = _dot_split_lhs(yc * yc, ones_ref[...]) * (1.0 / RW_HEAD)
    yn = yc * lax.rsqrt(var + RW_GN_EPS) * gng_ref[...] + gnb_ref[...]
    orw = (yn + bg_ref[:, 0:RW_W]) * bg_ref[:, RW_W:]
    yy = (jnp.dot(ona_ref[...], wo_ref[0:NA_W, :], preferred_element_type=F32)
          + jnp.dot(orw.astype(MM), wo_ref[NA_W:, :], preferred_element_type=F32))
    _res_ln(h_ref, gate, yy, lng_ref, lnb_ref, o_ref)


def _even_out_call(h, mod, ona, yf, yr, bg, gng, gnb, wo, lng, lnb, nB, L):
    B, S, _ = h.shape
    R = mod.shape[0]
    _, _, ones = _rw_masks()
    tm = _row_tile(S)
    tile = lambda w: pl.BlockSpec((None, tm, w), lambda b, i: (b, i, 0))
    full = lambda shape: pl.BlockSpec(shape, lambda b, i: (0,) * len(shape))
    return pl.pallas_call(
        functools.partial(_even_out_kernel, nB=nB, L=L),
        grid=(B, S // tm),
        in_specs=[tile(D), full((R, 6 * D)), tile(NA_W), tile(RW_W), tile(RW_W), tile(2 * RW_W),
                  full((RW_W, RW_W)), full((1, RW_W)), full((1, RW_W)), full((D, D)), full((1, D)), full((1, D))],
        out_specs=tile(D),
        out_shape=jax.ShapeDtypeStruct((B, S, D), F32),
        compiler_params=_params(("parallel", "arbitrary")),
        name="even_out_proj_ln",
    )(h, mod, ona, yf, yr, bg, jnp.asarray(ones, MM), gng, gnb, wo, lng, lnb)


def _odd_out_kernel(h_ref, mod_ref, oda_ref, wo_ref, lng_ref, lnb_ref, o_ref, *, nB, L):
    b = pl.program_id(0)
    i = pl.program_id(1)
    gate = _mod_tile(mod_ref, b, i, 2, nB, h_ref.shape[0], L)
    yy = jnp.dot(oda_ref[...], wo_ref[...], preferred_element_type=F32)
    _res_ln(h_ref, gate, yy, lng_ref, lnb_ref, o_ref)


def _odd_out_call(h, mod, oda, wo, lng, lnb, nB, L):
    B, S, _ = h.shape
    R = mod.shape[0]
    tm = _row_tile(S)
    tile = lambda w: pl.BlockSpec((None, tm, w), lambda b, i: (b, i, 0))
    full = lambda shape: pl.BlockSpec(shape, lambda b, i: (0,) * len(shape))
    return pl.pallas_call(
        functools.partial(_odd_out_kernel, nB=nB, L=L),
        grid=(B, S // tm),
        in_specs=[tile(D), full((R, 6 * D)), tile(DA_W), full((DA_W, D)), full((1, D)), full((1, D))],
        out_specs=tile(D),
        out_shape=jax.ShapeDtypeStruct((B, S, D), F32),
        compiler_params=_params(("parallel", "arbitrary")),
        name="odd_out_proj_ln",
    )(h, mod, oda, wo, lng, lnb)


def _odd_in_kernel(h_ref, mod_ref, w_ref, cos_ref, sin_ref, o_ref, *, nB, L):
    b = pl.program_id(0)
    i = pl.program_id(1)
    mt = functools.partial(_mod_tile, mod_ref, b, i, nB=nB, tm=h_ref.shape[0], L=L)
    u = (h_ref[...] * (1.0 + mt(1)) + mt(0)).astype(MM)
    res = jnp.dot(u, w_ref[...], preferred_element_type=F32)
    cs = cos_ref[...]
    sn = sin_ref[...]
    lane = lax.broadcasted_iota(jnp.int32, cs.shape, 1)
    first = (lane % 32) < 16
    for j in range(2 * DA_W // 128):
        zj = res[:, j * 128:(j + 1) * 128]
        sw = jnp.where(first, pltpu.roll(zj, 112, axis=1), pltpu.roll(zj, 16, axis=1))
        rot = zj * cs + sw * sn
        if j < DA_W // 128:
            rot = rot * DA_QSCALE
        o_ref[:, j * 128:(j + 1) * 128] = rot.astype(o_ref.dtype)
    o_ref[:, 2 * DA_W:] = res[:, 2 * DA_W:].astype(o_ref.dtype)


def _odd_in_call(h, mod, w, cos_t, sin_t, nB, L):
    B, S, _ = h.shape
    R = mod.shape[0]
    tm = _row_tile(S)
    return pl.pallas_call(
        functools.partial(_odd_in_kernel, nB=nB, L=L),
        grid=(B, S // tm),
        in_specs=[pl.BlockSpec((None, tm, D), lambda b, i: (b, i, 0)),
                  pl.BlockSpec((R, 6 * D), lambda b, i: (0, 0)),
                  pl.BlockSpec((D, 3 * DA_W), lambda b, i: (0, 0)),
                  pl.BlockSpec((tm, 128), lambda b, i: (i, 0)),
                  pl.BlockSpec((tm, 128), lambda b, i: (i, 0))],
        out_specs=pl.BlockSpec((None, tm, 3 * DA_W), lambda b, i: (b, i, 0)),
        out_shape=jax.ShapeDtypeStruct((B, S, 3 * DA_W), MM),
        compiler_params=_params(("parallel", "arbitrary")),
        name="odd_in_proj_rope",
    )(h, mod, w, cos_t, sin_t)


def _rope_tables(L, T):
    nf = 16
    inv = ROPE_THETA ** (-jnp.arange(nf, dtype=F32) / nf)
    t = jnp.arange(T)
    ang_r = (t // GRID_W).astype(F32)[:, None] * inv
    ang_c = (t % GRID_W).astype(F32)[:, None] * inv
    cos64 = jnp.concatenate([jnp.cos(ang_r), jnp.cos(ang_r), jnp.cos(ang_c), jnp.cos(ang_c)], -1)
    sin64 = jnp.concatenate([-jnp.sin(ang_r), jnp.sin(ang_r), -jnp.sin(ang_c), jnp.sin(ang_c)], -1)
    cos_t = jnp.concatenate([jnp.ones((L, 128), F32), jnp.tile(cos64, (1, 2))], 0)
    sin_t = jnp.concatenate([jnp.zeros((L, 128), F32), jnp.tile(sin64, (1, 2))], 0)
    return cos_t, sin_t


def _da_kernel(q_ref, k_ref, v_ref, lq1_ref, lk1_ref, lq2_ref, lk2_ref, sg_ref, o_ref, m_scr, acc_scr, s_scr, va_scr,
               *, L, tk, n_chunks, lam_init):
    i = pl.program_id(2)
    hs = range(DA_NH)
    hl = [slice(128 * hh, 128 * (hh + 1)) for hh in hs]

    @pl.when(i == 0)
    def _():
        for hh in hs:
            va_scr[hh, :, 0:128] = v_ref[:, hl[hh]]
            va_scr[hh, :, 128:256] = jnp.ones((va_scr.shape[1], 128), va_scr.dtype)

    lane = lax.broadcasted_iota(jnp.int32, (DA_TQ, 128), 1)
    lo = lane < 64
    qq = []
    for hh in hs:
        q = q_ref[:, hl[hh]]
        zero = jnp.zeros_like(q)
        qq.append(jnp.concatenate([jnp.where(lo, q, zero), jnp.where(lo, zero, q)], axis=0))

    def rows(j):
        return slice(j * tk, (j + 1) * tk)

    def scores(hh, j):
        return _dot_nt(qq[hh], k_ref[rows(j), hl[hh]])

    def softmax_pv(hh, s_ref, vb, first):
        ncol = s_ref.shape[1] // 128
        mx = s_ref[:, 0:128]
        for cb in range(1, ncol):
            mx = jnp.maximum(mx, s_ref[:, cb * 128:(cb + 1) * 128])
        m_cur = jnp.max(mx, axis=-1, keepdims=True)
        if first:
            m_new = jnp.broadcast_to(m_cur, m_scr.shape[1:])
            acc_scr[hh] = _dot(jnp.exp2(s_ref[...] - m_cur), vb)
        else:
            m_old = m_scr[hh]
            m_new = jnp.maximum(m_old, m_cur)
            alpha = jnp.exp2(m_old - m_new)
            p = jnp.exp2(s_ref[...] - jnp.tile(m_new, (1, ncol)))
            acc_scr[hh] = jnp.tile(alpha, (1, 2)) * acc_scr[hh] + _dot(p, vb)
        m_scr[hh] = m_new

    @pl.when(i == 0)
    def _():
        for hh in hs:
            s_scr[hh, 0, :, 0:L] = _dot_nt(qq[hh], k_ref[0:L, hl[hh]])
        for hh in hs:
            softmax_pv(hh, s_scr.at[hh, 0, :, 0:L], va_scr[hh, 0:L, :], True)

    @pl.when(i > 0)
    def _():
        for hh in hs:
            s_scr[hh, 0] = scores(hh, 0)
        for j in range(n_chunks):
            for hh in hs:
                if j + 1 < n_chunks:
                    s_scr[hh, (j + 1) % DA_NBUF] = scores(hh, j + 1)
                softmax_pv(hh, s_scr.at[hh, j % DA_NBUF], va_scr[hh, rows(j), :], j == 0)

    lam = (jnp.exp(jnp.sum(lq1_ref[...] * lk1_ref[...], axis=-1, keepdims=True))
           - jnp.exp(jnp.sum(lq2_ref[...] * lk2_ref[...], axis=-1, keepdims=True)) + lam_init)
    for hh in hs:
        o_all = acc_scr[hh, :, 0:128] * (1.0 / acc_scr[hh, :, 128:256])
        o = o_all[0:DA_TQ] - lam * o_all[DA_TQ:]
        o = o * lax.rsqrt(jnp.mean(o * o, axis=-1, keepdims=True) + 1e-5) * sg_ref[...] * (1.0 - lam_init)
        o_ref[:, hl[hh]] = o.astype(o_ref.dtype)


def _da_call(qkv, lq1, lk1, lq2, lk2, sg, L, lam_init):
    B, S, _ = qkv.shape
    tk = next(t for t in DA_TK if S % t == 0 and t >= L)
    n_chunks = S // tk
    nh = DA_HEADS // DA_NH
    w = 128 * DA_NH
    full = lambda shape: pl.BlockSpec(shape, lambda b, h, i: (0,) * len(shape))
    return pl.pallas_call(
        functools.partial(_da_kernel, L=L, tk=tk, n_chunks=n_chunks, lam_init=lam_init),
        grid=(B, nh, S // DA_TQ),
        in_specs=[pl.BlockSpec((None, DA_TQ, w), lambda b, h, i: (b, i, h)),
                  pl.BlockSpec((None, S, w), lambda b, h, i: (b, 0, nh + h)),
                  pl.BlockSpec((None, S, w), lambda b, h, i: (b, 0, 2 * nh + h)),
                  full((1, 64)), full((1, 64)), full((1, 64)), full((1, 64)), full((1, 128))],
        out_specs=pl.BlockSpec((None, DA_TQ, w), lambda b, h, i: (b, i, h)),
        out_shape=jax.ShapeDtypeStruct((B, S, DA_W), MM),
        scratch_shapes=[pltpu.VMEM((DA_NH, 2 * DA_TQ, 128), F32), pltpu.VMEM((DA_NH, 2 * DA_TQ, 256), F32),
                        pltpu.VMEM((DA_NH, DA_NBUF, 2 * DA_TQ, tk), F32), pltpu.VMEM((DA_NH, S, 256), MM)],
        compiler_params=_params(("arbitrary", "arbitrary", "arbitrary")),
        name="diff_attention",
    )(qkv, qkv, qkv, lq1, lk1, lq2, lk2, sg)


def _route(logits_t, rb):
    s = _sigmoid(logits_t)
    sel = s + rb
    x = [sel[e:e + 1, :] for e in range(N_EXPERTS)]
    tg = []
    for g in range(N_GROUPS):
        best = None
        for a in range(EPG):
            for b in range(a + 1, EPG):
                pair = x[EPG * g + a] + x[EPG * g + b]
                best = pair if best is None else jnp.maximum(best, pair)
        tg.append(best)
    rows = []
    gsel = []
    for g in range(N_GROUPS):
        chosen = None
        for g2 in range(N_GROUPS):
            if g2 == g:
                continue
            c = (tg[g] > tg[g2]) if g2 < g else (tg[g] >= tg[g2])
            chosen = c if chosen is None else jnp.logical_and(chosen, c)
        for a in range(EPG):
            rank = jnp.zeros_like(x[0])
            for b in range(EPG):
                if b == a:
                    continue
                ahead = (x[EPG * g + b] >= x[EPG * g + a]) if b < a else (x[EPG * g + b] > x[EPG * g + a])
                rank = rank + jnp.where(ahead, 1.0, 0.0)
            picked = jnp.logical_and(chosen, rank < 1.5)
            rows.append(jnp.where(picked, s[EPG * g + a:EPG * g + a + 1, :], 0.0))
        gsel.append(jnp.where(chosen, 1.0, 0.0))
    comb = jnp.concatenate(rows, axis=0)
    return comb * (1.0 / jnp.sum(comb, axis=0, keepdims=True)), jnp.concatenate(gsel, axis=0)


def _router_kernel(h_ref, mod_ref, rwt_ref, rb_ref, u_ref, rt_ref, *, nB, tile0, L):
    b = pl.program_id(0)
    i = pl.program_id(1)
    tm = h_ref.shape[0]
    mt = functools.partial(_mod_tile, mod_ref, b, i, nB=nB, tm=tm, L=L, tile0=tile0)
    u = h_ref[...] * (1.0 + mt(4)) + mt(3)
    u_ref[...] = u.astype(MM)
    logits_t = lax.dot_general(rwt_ref[...], u, (((1,), (1,)), ((), ())), precision=HIGHEST,
                               preferred_element_type=F32)
    comb_t, gsel_t = _route(logits_t, rb_ref[...])
    pad = jnp.zeros((MOE_RT_ROWS - N_EXPERTS - N_GROUPS, tm), F32)
    rt_ref[...] = jnp.concatenate([comb_t, gsel_t, pad], axis=0)


def _router_call(h, mod, rwt, rb, nB, tm, tile0, L):
    B, S, _ = h.shape
    R = mod.shape[0]
    nt = S // tm - tile0
    full = lambda shape: pl.BlockSpec(shape, lambda b, i: (0,) * len(shape))
    return pl.pallas_call(
        functools.partial(_router_kernel, nB=nB, tile0=tile0, L=L),
        grid=(B, nt),
        in_specs=[pl.BlockSpec((None, tm, D), lambda b, i: (b, i + tile0, 0)),
                  full((R, 6 * D)), full((N_EXPERTS, D)), full((N_EXPERTS, 1))],
        out_specs=[pl.BlockSpec((None, tm, D), lambda b, i: (b, i, 0)),
                   pl.BlockSpec((MOE_RT_ROWS, tm), lambda b, i: (0, b * nt + i))],
        out_shape=[jax.ShapeDtypeStruct((B, nt * tm, D), MM),
                   jax.ShapeDtypeStruct((MOE_RT_ROWS, B * nt * tm), F32)],
        compiler_params=_params(("parallel", "arbitrary")),
        name="moe_router",
    )(h, mod, rwt, rb)


def _expert_kernel(cnt_ref, u_ref, rt_ref, wg_ref, wu_ref, wd_ref, f_ref, xs_scr, cs_scr, fs_scr, pos_scr,
                   *, TE, NP, nt):
    t = pl.program_id(0)
    g = pl.program_id(1)
    lane = lax.broadcasted_iota(jnp.int32, (MOE_BLK, 128), 1)
    for hf in range(NP):
        base = hf * TE
        n = [cnt_ref[gg * nt + t * NP + hf] for gg in range(N_GROUPS)]
        off = [jnp.int32(0)]
        for gg in range(N_GROUPS - 1):
            off.append(off[-1] + n[gg])

        @pl.when(g == 0)
        def _(base=base, off=off):
            rt = rt_ref[:, base:base + TE]
            gs = rt[N_EXPERTS:N_EXPERTS + 8, :]
            r_i = lax.broadcasted_iota(jnp.int32, (TE, TE), 0)
            c_i = lax.broadcasted_iota(jnp.int32, (TE, TE), 1)
            before = jnp.where(r_i < c_i, 1.0, 0.0).astype(MM)
            cnt_before = jnp.dot(gs.astype(MM), before, preferred_element_type=F32)
            pos = jnp.zeros((1, TE), F32)
            for gg in range(N_GROUPS):
                pos = pos + gs[gg:gg + 1, :] * (off[gg].astype(F32) + cnt_before[gg:gg + 1, :])
            perm = jnp.where(r_i.astype(F32) == pos, 1.0, 0.0).astype(MM)
            xs_scr[base:base + TE, :] = jnp.dot(perm, u_ref[base:base + TE, :],
                                                preferred_element_type=F32).astype(MM)
            rt_pad = jnp.concatenate([rt, jnp.zeros((128 - MOE_RT_ROWS, TE), F32)], axis=0)
            hi, lo = _split(rt_pad)
            cs_scr[base:base + TE, :] = _dot_nt(perm, hi) + _dot_nt(perm, lo)
            pos_scr[base:base + TE, :] = jnp.broadcast_to(pos, (128, TE)).T
            fs_scr[base:base + TE, :] = jnp.zeros((TE, D), F32)

        lo_g = jnp.int32(0)
        n_g = jnp.int32(0)
        for gg in range(N_GROUPS):
            lo_g = jnp.where(g == gg, off[gg], lo_g)
            n_g = jnp.where(g == gg, n[gg], n_g)
        hi_g = lo_g + n_g
        for rb in range(TE // MOE_BLK):
            @pl.when(jnp.logical_and(lo_g < MOE_BLK * (rb + 1), hi_g > MOE_BLK * rb))
            def _(r0=base + rb * MOE_BLK):
                rows = slice(r0, r0 + MOE_BLK)
                xb = xs_scr[rows, :]
                c = cs_scr[rows, :]
                hg = [jnp.dot(xb, wg_ref[e], preferred_element_type=F32) for e in range(EPG)]
                hu = [jnp.dot(xb, wu_ref[e], preferred_element_type=F32) for e in range(EPG)]
                col = [jnp.sum(jnp.where(lane == EPG * g + e, c, 0.0), axis=-1, keepdims=True) for e in range(EPG)]
                he = jnp.concatenate([(hg[e] * _sigmoid(hg[e]) * hu[e] * col[e]).astype(MM) for e in range(EPG)],
                                     axis=1)
                fs_scr[rows, :] += jnp.dot(he, wd_ref[...], preferred_element_type=F32)

        @pl.when(g == N_GROUPS - 1)
        def _(base=base):
            c_i = lax.broadcasted_iota(jnp.int32, (TE, TE), 1)
            pos_t = jnp.tile(pos_scr[base:base + TE, :], (1, TE // 128))
            unperm = jnp.where(c_i.astype(F32) == pos_t, 1.0, 0.0).astype(MM)
            f_ref[base:base + TE, :] = jnp.dot(unperm, fs_scr[base:base + TE, :].astype(MM),
                                               preferred_element_type=F32).astype(f_ref.dtype)


def _expert_call(u, rt, wg, wu, wd):
    N = u.shape[0]
    TE = MOE_TE if N % MOE_TE == 0 else TM
    nt = N // TE
    NP = MOE_NP if nt % MOE_NP == 0 else 1
    cnt = jnp.sum(rt[N_EXPERTS:N_EXPERTS + N_GROUPS].reshape(N_GROUPS, nt, TE), axis=-1)
    cnt = cnt.astype(jnp.int32).reshape(N_GROUPS * nt)
    gw = EPG * D_EXPERT
    grid_spec = pltpu.PrefetchScalarGridSpec(
        num_scalar_prefetch=1,
        grid=(nt // NP, N_GROUPS),
        in_specs=[pl.BlockSpec((NP * TE, D), lambda t, g, c: (t, 0)),
                  pl.BlockSpec((MOE_RT_ROWS, NP * TE), lambda t, g, c: (0, t)),
                  pl.BlockSpec((None, EPG, D, D_EXPERT), lambda t, g, c: (g, 0, 0, 0)),
                  pl.BlockSpec((None, EPG, D, D_EXPERT), lambda t, g, c: (g, 0, 0, 0)),
                  pl.BlockSpec((None, gw, D), lambda t, g, c: (g, 0, 0))],
        out_specs=pl.BlockSpec((NP * TE, D), lambda t, g, c: (t, 0)),
        scratch_shapes=[pltpu.VMEM((NP * TE, D), MM), pltpu.VMEM((NP * TE, 128), F32),
                        pltpu.VMEM((NP * TE, D), F32), pltpu.VMEM((NP * TE, 128), F32)])
    return pl.pallas_call(
        functools.partial(_expert_kernel, TE=TE, NP=NP, nt=nt),
        grid_spec=grid_spec,
        out_shape=jax.ShapeDtypeStruct((N, D), MM),
        compiler_params=_params(("parallel", "arbitrary")),
        name="moe_experts",
    )(cnt, u, rt, wg, wu, wd)


def _moe_ln_kernel(h_ref, mod_ref, f_ref, lng_ref, lnb_ref, o_ref, *, nB, tile0, L):
    b = pl.program_id(0)
    i = pl.program_id(1)
    gate = _mod_tile(mod_ref, b, i, 5, nB, h_ref.shape[0], L, tile0)
    _res_ln(h_ref, gate, f_ref[...].astype(F32), lng_ref, lnb_ref, o_ref)


def _moe_ln_call(h, mod, f, lng, lnb, nB, tm, tile0, L):
    B, S, _ = h.shape
    R = mod.shape[0]
    nt = S // tm - tile0
    full = lambda shape: pl.BlockSpec(shape, lambda b, i: (0,) * len(shape))
    return pl.pallas_call(
        functools.partial(_moe_ln_kernel, nB=nB, tile0=tile0, L=L),
        grid=(B, nt),
        in_specs=[pl.BlockSpec((None, tm, D), lambda b, i: (b, i + tile0, 0)),
                  full((R, 6 * D)),
                  pl.BlockSpec((None, tm, D), lambda b, i: (b, i, 0)),
                  full((1, D)), full((1, D))],
        out_specs=pl.BlockSpec((None, tm, D), lambda b, i: (b, i, 0)),
        out_shape=jax.ShapeDtypeStruct((B, nt * tm, D), F32),
        compiler_params=_params(("parallel", "arbitrary")),
        name="moe_res_ln",
    )(h, mod, f, lng, lnb)


def _moe_call(h, mod, rwt, rb, wg, wu, wd, lng, lnb, nB, latent_only, L):
    B, S, _ = h.shape
    tm, tile0 = (TM, L // TM) if latent_only else (_row_tile(S), 0)
    u, rt = _router_call(h, mod, rwt, rb, nB, tm, tile0, L)
    f = _expert_call(u.reshape(-1, D), rt, wg.reshape(N_GROUPS, EPG, D, D_EXPERT),
                     wu.reshape(N_GROUPS, EPG, D, D_EXPERT), wd.reshape(N_GROUPS, EPG * D_EXPERT, D))
    return _moe_ln_call(h, mod, f.reshape(B, -1, D), lng, lnb, nB, tm, tile0, L)


def kernel(x, c, ctx, c_ctx, ada_w, ada_b, ln_g, ln_b, even_w_in, even_w_out, shift_w, na_rpb, rw_w0, rw_w_up, rw_a0, rw_a_up, rw_g_up, rw_k_k, rw_k_a, rw_r_k, rw_gn_g, rw_gn_b, odd_w_in, odd_w_out, da_lq1, da_lk1, da_lq2, da_lk2, da_subln_g, router_w, router_b, exp_w_gate, exp_w_up, exp_w_down):
    B, T, _ = x.shape
    L = ctx.shape[1]
    assert L == TM and T % TM == 0 and (T // GRID_W) >= 12
    rows = T // GRID_W
    R = -(-(B + 1) // 8) * 8

    cv = jnp.zeros((R, D), F32).at[:B].set(c).at[B].set(c_ctx)
    mods = _ada_call(cv, ada_w, ada_b)

    cos_t, sin_t = _rope_tables(L, T)
    rwt = router_w.T
    rb = router_b.reshape(N_EXPERTS, 1)
    zpad = jnp.zeros((2, 64, RW_W), F32)

    h = jnp.concatenate([ctx, x], axis=1)
    for l in range(DEPTH):
        mod = mods[l]
        i = l // 2
        lng = ln_g[l].reshape(2, 1, D)
        lnb = ln_b[l].reshape(2, 1, D)
        if l % 2 == 0:
            w_in = even_w_in[i].astype(MM)
            qkv, prw = _even_in_call(h, mod, w_in[:, :3 * NA_W], w_in[:, 3 * NA_W:], B, L)
            ona = _na_call(qkv, _na_bias(na_rpb[i], rows), L)
            yf, yr, bg = _rw_call(
                prw, shift_w[i], rw_w0[i].reshape(2, 1, RW_W),
                jnp.concatenate([rw_w_up[i], zpad], axis=1).astype(MM), rw_a0[i].reshape(2, 1, RW_W),
                jnp.concatenate([zpad, rw_a_up[i]], axis=1).astype(MM), rw_g_up[i].astype(MM),
                rw_k_k[i].reshape(1, RW_W), rw_k_a[i].reshape(1, RW_W), rw_r_k[i].reshape(1, RW_W), L)
            h = _even_out_call(h, mod, ona, yf, yr, bg, rw_gn_g[i].reshape(1, RW_W), rw_gn_b[i].reshape(1, RW_W),
                               even_w_out[i].astype(MM), lng[0], lnb[0], B, L)
        else:
            lam_init = 0.8 - 0.6 * math.exp(-0.3 * l)
            qkv = _odd_in_call(h, mod, odd_w_in[i].astype(MM), cos_t, sin_t, B, L)
            oda = _da_call(qkv, da_lq1[i].reshape(1, 64), da_lk1[i].reshape(1, 64), da_lq2[i].reshape(1, 64),
                           da_lk2[i].reshape(1, 64), da_subln_g[i].reshape(1, 128), L, lam_init)
            h = _odd_out_call(h, mod, oda, odd_w_out[i].astype(MM), lng[0], lnb[0], B, L)
        h = _moe_call(h, mod, rwt, rb, exp_w_gate[l].astype(MM), exp_w_up[l].astype(MM), exp_w_down[l].astype(MM),
                      lng[1], lnb[1], B, l == DEPTH - 1, L)
    return h
```

```python
import functools
import math

import numpy as np
import jax
import jax.numpy as jnp
from jax import lax
from jax.experimental import pallas as pl
from jax.experimental.pallas import tpu as pltpu

F32 = jnp.float32
MM = jnp.bfloat16
HIGHEST = lax.Precision.HIGHEST

D = 1024
DEPTH = 4
GRID_W = 64
NA_HEADS = 8
NA_W = 512
NA_WIN_R = 8
NA_WIN_C = 16
NA_QB = 128
NA_NB = 2
NA_KROWS = 9
NA_NLOC = NA_KROWS * GRID_W
RW_W = 512
RW_HEAD = 64
RW_IN = 1792
RW_GN_EPS = 64e-5
RW_CHUNK = 64
RW_GRP = 256
RW_NSUB = 4
RW_INV_BASE = 8
RW_INV_LEVELS = 3
RW_NMASK = 7 + RW_INV_LEVELS
DA_HEADS = 8
DA_W = 1024
ROPE_THETA = 10000.0
N_EXPERTS = 16
N_GROUPS = 4
EPG = 4
D_EXPERT = 512
MOE_RT_ROWS = 32
MOE_TE = 512
MOE_NP = 2
MOE_BLK = 128
ALPHA = (2 * DEPTH) ** 0.25
LN_EPS = 1e-5
TM = 256
TM_BIG = 768
DA_TQ = 256
DA_NH = 2
DA_TK = (768, 512, 256)
DA_NBUF = 3
DA_QSCALE = 0.125 * math.log2(math.e)
NEG = -1e30
VMEM_LIMIT = 56 * 1024 * 1024


def _dot(a, b):
    return jnp.dot(a.astype(MM), b.astype(MM), preferred_element_type=F32)


def _dot_nt(a, b):
    return lax.dot_general(a.astype(MM), b.astype(MM), (((1,), (1,)), ((), ())), preferred_element_type=F32)


def _split(x):
    hi = x.astype(MM)
    lo = (x - hi.astype(F32)).astype(MM)
    return hi, lo


def _dot_split_lhs(x, w):
    hi, lo = _split(x)
    return jnp.dot(hi, w, preferred_element_type=F32) + jnp.dot(lo, w, preferred_element_type=F32)


def _dot_split_rhs(w, x):
    hi, lo = _split(x)
    return jnp.dot(w, hi, preferred_element_type=F32) + jnp.dot(w, lo, preferred_element_type=F32)


def _sigmoid(x):
    return 1.0 / (1.0 + jnp.exp(-x))


def _params(sem):
    return pltpu.CompilerParams(dimension_semantics=sem, vmem_limit_bytes=VMEM_LIMIT)


def _mod_row(mod_ref, row, j):
    return mod_ref[pl.ds(row, 1), j * D:(j + 1) * D]


def _mod_tile(mod_ref, b, i, j, nB, tm, L, tile0=0):
    if L % tm == 0:
        return _mod_row(mod_ref, jnp.where(i + tile0 < L // tm, nB, b), j)
    assert tile0 == 0 and tm > L
    rid = lax.broadcasted_iota(jnp.int32, (tm, 1), 0)
    is_ctx = jnp.logical_and(i == 0, rid < L)
    return jnp.where(is_ctx, _mod_row(mod_ref, nB, j), _mod_row(mod_ref, b, j))


def _row_tile(S):
    return TM_BIG if S % TM_BIG == 0 else TM


def _layer_norm(z, g, b):
    mu = jnp.mean(z, axis=-1, keepdims=True)
    zc = z - mu
    var = jnp.mean(zc * zc, axis=-1, keepdims=True)
    return zc * lax.rsqrt(var + LN_EPS) * g + b


def _ada_kernel(cv_ref, w_ref, b_ref, o_ref):
    x = cv_ref[...]
    x = x * _sigmoid(x)
    o_ref[...] = jnp.dot(x, w_ref[...], precision=HIGHEST, preferred_element_type=F32) + b_ref[...]


def _ada_call(cv, ada_w, ada_b):
    R = cv.shape[0]
    tn = 1536
    return pl.pallas_call(
        _ada_kernel,
        grid=(DEPTH, 6 * D // tn),
        in_specs=[pl.BlockSpec((R, D), lambda l, j: (0, 0)),
                  pl.BlockSpec((None, D, tn), lambda l, j: (l, 0, j)),
                  pl.BlockSpec((None, 1, tn), lambda l, j: (l, 0, j))],
        out_specs=pl.BlockSpec((None, R, tn), lambda l, j: (l, 0, j)),
        out_shape=jax.ShapeDtypeStruct((DEPTH, R, 6 * D), F32),
        compiler_params=_params(("arbitrary", "arbitrary")),
        name="ada_mod",
    )(cv, ada_w, ada_b.reshape(DEPTH, 1, 6 * D))


def _even_in_kernel(h_ref, mod_ref, wna_ref, wrw_ref, qkv_ref, prw_ref, *, nB, L):
    b = pl.program_id(0)
    i = pl.program_id(1)
    mt = functools.partial(_mod_tile, mod_ref, b, i, nB=nB, tm=h_ref.shape[0], L=L)
    u = (h_ref[...] * (1.0 + mt(1)) + mt(0)).astype(MM)
    res = jnp.dot(u, wna_ref[...], preferred_element_type=F32)
    qkv_ref[:, 0:NA_W] = (res[:, 0:NA_W] * 0.125).astype(qkv_ref.dtype)
    qkv_ref[:, NA_W:] = res[:, NA_W:].astype(qkv_ref.dtype)
    prw_ref[...] = jnp.dot(u, wrw_ref[...], preferred_element_type=F32)


def _even_in_call(h, mod, w_na, w_rw, nB, L):
    B, S, _ = h.shape
    R = mod.shape[0]
    tm = _row_tile(S)
    return pl.pallas_call(
        functools.partial(_even_in_kernel, nB=nB, L=L),
        grid=(B, S // tm),
        in_specs=[pl.BlockSpec((None, tm, D), lambda b, i: (b, i, 0)),
                  pl.BlockSpec((R, 6 * D), lambda b, i: (0, 0)),
                  pl.BlockSpec((D, 3 * NA_W), lambda b, i: (0, 0)),
                  pl.BlockSpec((D, RW_IN), lambda b, i: (0, 0))],
        out_specs=[pl.BlockSpec((None, tm, 3 * NA_W), lambda b, i: (b, i, 0)),
                   pl.BlockSpec((None, tm, RW_IN), lambda b, i: (b, i, 0))],
        out_shape=[jax.ShapeDtypeStruct((B, S, 3 * NA_W), MM),
                   jax.ShapeDtypeStruct((B, S, RW_IN), F32)],
        compiler_params=_params(("parallel", "arbitrary")),
        name="even_in_proj",
    )(h, mod, w_na, w_rw)


def _na_bias(rpb, rows):
    cq = np.arange(GRID_W)[:, None]
    ck = np.arange(GRID_W)[None, :]
    cs = np.clip(cq - NA_WIN_C // 2, 0, GRID_W - NA_WIN_C)
    ok_c = (ck >= cs) & (ck < cs + NA_WIN_C)
    dc = np.clip(ck - cq + NA_WIN_C - 1, 0, 2 * NA_WIN_C - 2)
    onehot = np.zeros((2 * NA_WIN_C - 1, GRID_W * GRID_W), np.float32)
    onehot[dc.ravel(), np.arange(GRID_W * GRID_W)] = 1.0
    rc = jnp.einsum('hrd,dn->hrn', rpb, jnp.asarray(onehot), precision=HIGHEST)
    rc = rc.reshape(rpb.shape[0], 2 * NA_WIN_R - 1, GRID_W, GRID_W)
    cases = []
    for r0 in (0, 2, 4, rows - 4, rows - 2):
        ws = int(np.clip(r0 - NA_WIN_R // 2, 0, rows - NA_KROWS))
        qrows = []
        for qi in range(NA_QB // GRID_W):
            rq = r0 + qi
            rs = int(np.clip(rq - NA_WIN_R // 2, 0, rows - NA_WIN_R))
            pieces = []
            for j in range(NA_KROWS):
                rk = ws + j
                if rs <= rk < rs + NA_WIN_R:
                    pieces.append(jnp.where(ok_c[None], rc[:, rk - rq + NA_WIN_R - 1], NEG))
                else:
                    pieces.append(jnp.full((rpb.shape[0], GRID_W, GRID_W), NEG, F32))
            qrows.append(jnp.concatenate(pieces, axis=-1))
        cases.append(jnp.concatenate(qrows, axis=-2))
    return jnp.stack(cases, axis=0)


def _na_kernel(q_ref, k_ref, v_ref, *rest, L, rows):
    bias_refs, o_ref = rest[:NA_NB], rest[NA_NB]
    i = pl.program_id(2)
    nq = L // (NA_NB * NA_QB)
    lane = lax.broadcasted_iota(jnp.int32, (NA_QB, 128), 1)
    lo = lane < 64
    kc = k_ref[0:L, :]
    vc = v_ref[0:L, :]
    ch = [(n, hh) for n in range(NA_NB) for hh in range(2)]
    qs = [q_ref[n * NA_QB:(n + 1) * NA_QB, :] for n in range(NA_NB)]
    qm = {(n, hh): jnp.where(lo if hh == 0 else jnp.logical_not(lo), qs[n], jnp.zeros_like(qs[n])) for n, hh in ch}

    def store(outs):
        for n in range(NA_NB):
            o_ref[n * NA_QB:(n + 1) * NA_QB, :] = jnp.where(lo, outs[(n, 0)], outs[(n, 1)]).astype(o_ref.dtype)

    @pl.when(i < nq)
    def _():
        s = {c: _dot_nt(qm[c], kc) for c in ch}
        m = {c: jnp.max(s[c], axis=-1, keepdims=True) for c in ch}
        p = {c: jnp.exp(s[c] - m[c]) for c in ch}
        l = {c: jnp.sum(p[c], axis=-1, keepdims=True) for c in ch}
        store({c: _dot(p[c], vc) * (1.0 / l[c]) for c in ch})

    @pl.when(i >= nq)
    def _():
        kl, vl = [], []
        for n in range(NA_NB):
            r0 = 2 * ((i - nq) * NA_NB + n)
            ws = jnp.clip(r0 - NA_WIN_R // 2, 0, rows - NA_KROWS)
            start = pl.multiple_of(L + GRID_W * ws, GRID_W)
            kl.append(k_ref[pl.ds(start, NA_NLOC), :])
            vl.append(v_ref[pl.ds(start, NA_NLOC), :])
        s_loc = {c: _dot_nt(qm[c], kl[c[0]]) + bias_refs[c[0]][c[1]] for c in ch}
        s_ctx = {c: _dot_nt(qm[c], kc) for c in ch}
        m = {c: jnp.maximum(jnp.max(s_loc[c], axis=-1, keepdims=True), jnp.max(s_ctx[c], axis=-1, keepdims=True))
             for c in ch}
        p_loc = {c: jnp.exp(s_loc[c] - m[c]) for c in ch}
        p_ctx = {c: jnp.exp(s_ctx[c] - m[c]) for c in ch}
        l = {c: jnp.sum(p_loc[c], axis=-1, keepdims=True) + jnp.sum(p_ctx[c], axis=-1, keepdims=True) for c in ch}
        store({c: (_dot(p_loc[c], vl[c[0]]) + _dot(p_ctx[c], vc)) * (1.0 / l[c]) for c in ch})


def _na_call(qkv, bias, L):
    B, S, _ = qkv.shape
    rows = (S - L) // GRID_W
    tq = NA_NB * NA_QB
    assert L % tq == 0 and (S - L) % tq == 0
    nq = L // tq
    nhp = NA_HEADS // 2

    def bias_idx(n):
        def idx(b, hp, i):
            r0 = 2 * ((i - nq) * NA_NB + n)
            c = jnp.where(r0 == 0, 0,
                          jnp.where(r0 == 2, 1, jnp.where(r0 == rows - 4, 3, jnp.where(r0 == rows - 2, 4, 2))))
            return (jnp.where(i < nq, 2, c), hp, 0, 0)
        return idx

    return pl.pallas_call(
        functools.partial(_na_kernel, L=L, rows=rows),
        grid=(B, nhp, S // tq),
        in_specs=[pl.BlockSpec((None, tq, 128), lambda b, hp, i: (b, i, hp)),
                  pl.BlockSpec((None, S, 128), lambda b, hp, i: (b, 0, nhp + hp)),
                  pl.BlockSpec((None, S, 128), lambda b, hp, i: (b, 0, 2 * nhp + hp))]
                 + [pl.BlockSpec((None, 2, NA_QB, NA_NLOC), bias_idx(n)) for n in range(NA_NB)],
        out_specs=pl.BlockSpec((None, tq, 128), lambda b, hp, i: (b, i, hp)),
        out_shape=jax.ShapeDtypeStruct((B, S, NA_W), MM),
        compiler_params=_params(("parallel", "parallel", "arbitrary")),
        name="na_attention",
    )(qkv, qkv, qkv, *([bias] * NA_NB))


def _rw_conv(p_ref, prev_ref, next_ref, c, conv_ref, nL, NC):
    p = p_ref[...]
    at_start = jnp.logical_or(c == 0, c == nL)
    at_end = jnp.logical_or(c == nL - 1, c == NC - 1)
    prow = jnp.where(at_start, 0.0, prev_ref[7:8, :])
    nrow = jnp.where(at_end, 0.0, next_ref[0:1, :])
    rid = lax.broadcasted_iota(jnp.int32, p.shape, 0)
    last = p.shape[0] - 1
    pm = jnp.where(rid == 0, prow, pltpu.roll(p, 1, axis=0))
    pp = jnp.where(rid == last, nrow, pltpu.roll(p, last, axis=0))
    return pm * conv_ref[0:1, :] + p * conv_ref[1:2, :] + pp * conv_ref[2:3, :]


def _rw_prep(pc, d, w0_ref, wup_ref, a0_ref, aup_ref, kk_ref, ka_ref, ones_ref, cum_ref):
    r = pc[:, 0:RW_W]
    k = pc[:, RW_W:2 * RW_W]
    v = pc[:, 2 * RW_W:3 * RW_W]
    xwa = pc[:, 3 * RW_W:3 * RW_W + 128]
    z = w0_ref[d] + _dot(jnp.tanh(xwa), wup_ref[d])
    lw = -math.exp(-0.5) * _sigmoid(z)
    a = _sigmoid(a0_ref[d] + _dot(xwa, aup_ref[d]))
    kk = k * kk_ref[...]
    n2 = _dot_split_lhs(kk * kk, ones_ref[...])
    kk = kk * (1.0 / jnp.maximum(jnp.sqrt(n2), 1e-12))
    keff = k * (1.0 + (a - 1.0) * ka_ref[...])
    av = -kk
    bv = kk * a
    lc = _dot_split_rhs(cum_ref[d], lw)
    ltot = lc[RW_CHUNK - 1:RW_CHUNK, :] if d == 0 else lc[0:1, :]
    at = av * jnp.exp(lc - lw)
    rt = r * jnp.exp(lc)
    ginv = jnp.exp(-lc)
    bt = bv * ginv
    kt = keff * ginv
    grest = jnp.exp(ltot - lc)
    bh = bv * grest
    kh = keff * grest
    gc = jnp.exp(ltot)

    return dict(at=at, rt=rt, bt=bt, kt=kt, bh=bh, kh=kh, v=v, gc=gc, keff=keff, r=r)


def _rw_chains(preps, mask_ref, g_scr):
    bdm = mask_ref[0].astype(MM)
    eye = mask_ref[1]
    ngrp = RW_W // RW_GRP
    order = {0: list(range(RW_NSUB)), 1: list(range(RW_NSUB - 1, -1, -1))}
    chains = [(d, g, order[d][n]) for n in range(RW_NSUB) for d in range(2) for g in range(ngrp)]
    ex = {}
    for c in chains:
        d, g, slot = c
        sl = slice(g * RW_GRP, (g + 1) * RW_GRP)
        p = preps[(d, slot)]
        e = {}
        for n in ("at", "rt", "bt", "kt", "bh", "kh", "v"):
            z = p[n][:, sl].astype(MM)
            e[n] = jnp.concatenate([z, z, z, z], axis=0) * bdm
        e["vt"] = e["v"].T
        e["gc"] = p["gc"][:, sl]
        ex[c] = e
    xab, xrb, xak, xrk = {}, {}, {}, {}
    for c in chains:
        ar = jnp.concatenate([ex[c]["at"], ex[c]["rt"]], axis=0)
        ms = mask_ref[2 + 2 * c[0]]
        mi = mask_ref[3 + 2 * c[0]]
        xb = _dot_nt(ex[c]["bt"], ar)
        xab[c] = (xb[:, 0:RW_GRP] * ms).astype(MM)
        xrb[c] = (xb[:, RW_GRP:] * mi).astype(MM)
        xk = _dot_nt(ex[c]["kt"], ar)
        xak[c] = (xk[:, 0:RW_GRP] * ms).astype(MM)
        xrk[c] = (xk[:, RW_GRP:] * mi).astype(MM)
    eyeb = eye.astype(MM)
    xp = {c: xab[c] * mask_ref[6].astype(MM) for c in chains}
    tt = {c: eyeb + xp[c] for c in chains}
    for _ in range(RW_INV_BASE.bit_length() - 2):
        xp = {c: _dot(xp[c], xp[c]).astype(MM) for c in chains}
        tt = {c: _dot(tt[c], eyeb + xp[c]).astype(MM) for c in chains}
    for lvl in range(RW_INV_LEVELS):
        cm = mask_ref[7 + lvl].astype(MM)
        tc = {c: _dot(tt[c], xab[c] * cm).astype(MM) for c in chains}
        tt = {c: _dot(eyeb + tc[c], tt[c]).astype(MM) for c in chains}
    ys = {}
    for n in range(RW_NSUB):
        cur = [(d, g, order[d][n]) for d in range(2) for g in range(ngrp)]
        gs = {c: g_scr[c[0], c[1]] for c in cur}
        gsb = {c: gs[c].astype(MM) for c in cur}
        w1 = {c: (_dot_nt(gsb[c], ex[c]["at"]) + _dot(ex[c]["vt"], xak[c])).astype(MM) for c in cur}
        ut = {c: _dot(w1[c], tt[c]).astype(MM) for c in cur}
        yt = {c: _dot_nt(gsb[c], ex[c]["rt"]) + _dot(ut[c], xrb[c]) + _dot(ex[c]["vt"], xrk[c]) for c in cur}
        for c in cur:
            g_scr[c[0], c[1]] = gs[c] * ex[c]["gc"] + _dot(ut[c], ex[c]["bh"]) + _dot(ex[c]["vt"], ex[c]["kh"])
        for c in cur:
            ybd = yt[c].T
            ys[c] = ybd[0:64] + ybd[64:128] + ybd[128:192] + ybd[192:256]
    return [jnp.concatenate([jnp.concatenate([ys[(d, g, slot)] for g in range(ngrp)], axis=1)
                             for slot in range(RW_NSUB)], axis=0) for d in range(2)]


def _rw_kernel(pf_ref, pfp_ref, pfn_ref, pr_ref, prp_ref, prn_ref, conv_ref, w0_ref, wup_ref, a0_ref, aup_ref,
               gup_ref, kk_ref, ka_ref, rk_ref, ones_ref, cum_ref, mask_ref, yf_ref, yr_ref, bg_ref, g_scr,
               *, nL, NC):
    i = pl.program_id(1)

    @pl.when(i == 0)
    def _():
        g_scr[...] = jnp.zeros_like(g_scr)

    cf = i
    cr = jnp.where(i < nL, nL - 1 - i, NC - 1 - i + nL)
    args = (w0_ref, wup_ref, a0_ref, aup_ref, kk_ref, ka_ref, ones_ref, cum_ref)

    pcf = _rw_conv(pf_ref, pfp_ref, pfn_ref, cf, conv_ref, nL, NC)
    pcr = _rw_conv(pr_ref, prp_ref, prn_ref, cr, conv_ref, nL, NC)
    preps = {}
    for n in range(RW_NSUB):
        sf, sr = n, RW_NSUB - 1 - n
        preps[(0, sf)] = _rw_prep(pcf[sf * RW_CHUNK:(sf + 1) * RW_CHUNK], 0, *args)
        preps[(1, sr)] = _rw_prep(pcr[sr * RW_CHUNK:(sr + 1) * RW_CHUNK], 1, *args)
    xwa = pcf[:, 3 * RW_W:3 * RW_W + 128]
    a_r = _sigmoid(a0_ref[1] + _dot(xwa, aup_ref[1]))
    keff_r = pcf[:, RW_W:2 * RW_W] * (1.0 + (a_r - 1.0) * ka_ref[...])
    keff_f = jnp.concatenate([preps[(0, s)]["keff"] for s in range(RW_NSUB)], axis=0)
    rv = pcf[:, 0:RW_W] * rk_ref[...]
    bsum = _dot_split_lhs(rv * (keff_f + keff_r), ones_ref[...]) * pcf[:, 2 * RW_W:3 * RW_W]
    bg_ref[:, 0:RW_W] = bsum
    bg_ref[:, RW_W:] = _dot(_sigmoid(pcf[:, 3 * RW_W + 128:]), gup_ref[...])

    y_f, y_r = _rw_chains(preps, mask_ref, g_scr)
    yf_ref[...] = y_f
    yr_ref[...] = y_r


def _rw_masks():
    n = 4 * RW_CHUNK
    idx = np.arange(n)
    hd = idx // RW_CHUNK
    t = idx % RW_CHUNK
    bd = hd[:, None] == hd[None, :]
    m = np.zeros((RW_NMASK, n, n), np.float32)
    m[0] = bd
    m[1] = np.eye(n)
    m[2] = bd & (t[:, None] < t[None, :])
    m[3] = bd & (t[:, None] <= t[None, :])
    m[4] = bd & (t[:, None] > t[None, :])
    m[5] = bd & (t[:, None] >= t[None, :])
    same = lambda w: bd & ((t[:, None] // w) == (t[None, :] // w))
    m[6] = same(RW_INV_BASE)
    for lvl in range(RW_INV_LEVELS):
        w = RW_INV_BASE << lvl
        m[7 + lvl] = same(2 * w) & ~same(w)
    tt = np.arange(RW_CHUNK)
    cum = np.stack([tt[None, :] <= tt[:, None], tt[None, :] >= tt[:, None]]).astype(np.float32)
    ch = np.arange(RW_W) // RW_HEAD
    ones = (ch[:, None] == ch[None, :]).astype(np.float32)
    return m, cum, ones


def _rw_call(prw, conv_w, w0, wup_pad, a0, aup_pad, gup, kk, ka, rk, L):
    B, S, _ = prw.shape
    rb = RW_NSUB * RW_CHUNK
    assert L % rb == 0 and S % rb == 0
    NC = S // rb
    nL = L // rb
    n8 = S // 8
    r8 = rb // 8
    m, cum, ones = _rw_masks()

    def cr_of(i):
        return jnp.where(i < nL, nL - 1 - i, NC - 1 - i + nL)

    full = lambda shape: pl.BlockSpec(shape, lambda b, i: (0,) * len(shape))
    in_specs = [
        pl.BlockSpec((None, rb, RW_IN), lambda b, i: (b, i, 0)),
        pl.BlockSpec((None, 8, RW_IN), lambda b, i: (b, jnp.maximum(i * r8 - 1, 0), 0)),
        pl.BlockSpec((None, 8, RW_IN), lambda b, i: (b, jnp.minimum(i * r8 + r8, n8 - 1), 0)),
        pl.BlockSpec((None, rb, RW_IN), lambda b, i: (b, cr_of(i), 0)),
        pl.BlockSpec((None, 8, RW_IN), lambda b, i: (b, jnp.maximum(cr_of(i) * r8 - 1, 0), 0)),
        pl.BlockSpec((None, 8, RW_IN), lambda b, i: (b, jnp.minimum(cr_of(i) * r8 + r8, n8 - 1), 0)),
        full((3, RW_IN)), full((2, 1, RW_W)), full((2, 128, RW_W)), full((2, 1, RW_W)), full((2, 128, RW_W)),
        full((128, RW_W)), full((1, RW_W)), full((1, RW_W)), full((1, RW_W)),
        full((RW_W, RW_W)), full((2, RW_CHUNK, RW_CHUNK)), full((RW_NMASK, 4 * RW_CHUNK, 4 * RW_CHUNK)),
    ]
    out_specs = [
        pl.BlockSpec((None, rb, RW_W), lambda b, i: (b, i, 0)),
        pl.BlockSpec((None, rb, RW_W), lambda b, i: (b, cr_of(i), 0)),
        pl.BlockSpec((None, rb, 2 * RW_W), lambda b, i: (b, i, 0)),
    ]
    return pl.pallas_call(
        functools.partial(_rw_kernel, nL=nL, NC=NC),
        grid=(B, NC),
        in_specs=in_specs,
        out_specs=out_specs,
        out_shape=[jax.ShapeDtypeStruct((B, S, RW_W), F32),
                   jax.ShapeDtypeStruct((B, S, RW_W), F32),
                   jax.ShapeDtypeStruct((B, S, 2 * RW_W), F32)],
        scratch_shapes=[pltpu.VMEM((2, RW_W // RW_GRP, RW_GRP, RW_GRP), F32)],
        compiler_params=_params(("parallel", "arbitrary")),
        name="rwkv7_chunked",
    )(prw, prw, prw, prw, prw, prw, conv_w, w0, wup_pad, a0, aup_pad, gup, kk, ka, rk,
      jnp.asarray(ones, MM), jnp.asarray(cum, MM), jnp.asarray(m))


def _res_ln(h_ref, gate, y, lng_ref, lnb_ref, o_ref):
    z = ALPHA * h_ref[...] + gate * y
    o_ref[...] = _layer_norm(z, lng_ref[...], lnb_ref[...])


def _even_out_kernel(h_ref, mod_ref, ona_ref, yf_ref, yr_ref, bg_ref, ones_ref, gng_ref, gnb_ref, wo_ref,
                     lng_ref, lnb_ref, o_ref, *, nB, L):
    b = pl.program_id(0)
    i = pl.program_id(1)
    gate = _mod_tile(mod_ref, b, i, 2, nB, h_ref.shape[0], L)
    y = yf_ref[...] + yr_ref[...]
    mu = _dot_split_lhs(y, ones_ref[...]) * (1.0 / RW_HEAD)
    yc = y - mu
    var = _dot_split_lhs(yc * yc, ones_ref[...]) * (1.0 / RW_HEAD)
    yn = yc * lax.rsqrt(var + RW_GN_EPS) * gng_ref[...] + gnb_ref[...]
    orw = (yn + bg_ref[:, 0:RW_W]) * bg_ref[:, RW_W:]
    yy = (jnp.dot(ona_ref[...], wo_ref[0:NA_W, :], preferred_element_type=F32)
          + jnp.dot(orw.astype(MM), wo_ref[NA_W:, :], preferred_element_type=F32))
    _res_ln(h_ref, gate, yy, lng_ref, lnb_ref, o_ref)


def _even_out_call(h, mod, ona, yf, yr, bg, gng, gnb, wo, lng, lnb, nB, L):
    B, S, _ = h.shape
    R = mod.shape[0]
    _, _, ones = _rw_masks()
    tm = _row_tile(S)
    tile = lambda w: pl.BlockSpec((None, tm, w), lambda b, i: (b, i, 0))
    full = lambda shape: pl.BlockSpec(shape, lambda b, i: (0,) * len(shape))
    return pl.pallas_call(
        functools.partial(_even_out_kernel, nB=nB, L=L),
        grid=(B, S // tm),
        in_specs=[tile(D), full((R, 6 * D)), tile(NA_W), tile(RW_W), tile(RW_W), tile(2 * RW_W),
                  full((RW_W, RW_W)), full((1, RW_W)), full((1, RW_W)), full((D, D)), full((1, D)), full((1, D))],
        out_specs=tile(D),
        out_shape=jax.ShapeDtypeStruct((B, S, D), F32),
        compiler_params=_params(("parallel", "arbitrary")),
        name="even_out_proj_ln",
    )(h, mod, ona, yf, yr, bg, jnp.asarray(ones, MM), gng, gnb, wo, lng, lnb)


def _odd_out_kernel(h_ref, mod_ref, oda_ref, wo_ref, lng_ref, lnb_ref, o_ref, *, nB, L):
    b = pl.program_id(0)
    i = pl.program_id(1)
    gate = _mod_tile(mod_ref, b, i, 2, nB, h_ref.shape[0], L)
    yy = jnp.dot(oda_ref[...], wo_ref[...], preferred_element_type=F32)
    _res_ln(h_ref, gate, yy, lng_ref, lnb_ref, o_ref)


def _odd_out_call(h, mod, oda, wo, lng, lnb, nB, L):
    B, S, _ = h.shape
    R = mod.shape[0]
    tm = _row_tile(S)
    tile = lambda w: pl.BlockSpec((None, tm, w), lambda b, i: (b, i, 0))
    full = lambda shape: pl.BlockSpec(shape, lambda b, i: (0,) * len(shape))
    return pl.pallas_call(
        functools.partial(_odd_out_kernel, nB=nB, L=L),
        grid=(B, S // tm),
        in_specs=[tile(D), full((R, 6 * D)), tile(DA_W), full((DA_W, D)), full((1, D)), full((1, D))],
        out_specs=tile(D),
        out_shape=jax.ShapeDtypeStruct((B, S, D), F32),
        compiler_params=_params(("parallel", "arbitrary")),
        name="odd_out_proj_ln",
    )(h, mod, oda, wo, lng, lnb)


def _odd_in_kernel(h_ref, mod_ref, w_ref, cos_ref, sin_ref, o_ref, *, nB, L):
    b = pl.program_id(0)
    i = pl.program_id(1)
    mt = functools.partial(_mod_tile, mod_ref, b, i, nB=nB, tm=h_ref.shape[0], L=L)
    u = (h_ref[...] * (1.0 + mt(1)) + mt(0)).astype(MM)
    res = jnp.dot(u, w_ref[...], preferred_element_type=F32)
    cs = cos_ref[...]
    sn = sin_ref[...]
    lane = lax.broadcasted_iota(jnp.int32, cs.shape, 1)
    first = (lane % 32) < 16
    for j in range(2 * DA_W // 128):
        zj = res[:, j * 128:(j + 1) * 128]
        sw = jnp.where(first, pltpu.roll(zj, 112, axis=1), pltpu.roll(zj, 16, axis=1))
        rot = zj * cs + sw * sn
        if j < DA_W // 128:
            rot = rot * DA_QSCALE
        o_ref[:, j * 128:(j + 1) * 128] = rot.astype(o_ref.dtype)
    o_ref[:, 2 * DA_W:] = res[:, 2 * DA_W:].astype(o_ref.dtype)


def _odd_in_call(h, mod, w, cos_t, sin_t, nB, L):
    B, S, _ = h.shape
    R = mod.shape[0]
    tm = _row_tile(S)
    return pl.pallas_call(
        functools.partial(_odd_in_kernel, nB=nB, L=L),
        grid=(B, S // tm),
        in_specs=[pl.BlockSpec((None, tm, D), lambda b, i: (b, i, 0)),
                  pl.BlockSpec((R, 6 * D), lambda b, i: (0, 0)),
                  pl.BlockSpec((D, 3 * DA_W), lambda b, i: (0, 0)),
                  pl.BlockSpec((tm, 128), lambda b, i: (i, 0)),
                  pl.BlockSpec((tm, 128), lambda b, i: (i, 0))],
        out_specs=pl.BlockSpec((None, tm, 3 * DA_W), lambda b, i: (b, i, 0)),
        out_shape=jax.ShapeDtypeStruct((B, S, 3 * DA_W), MM),
        compiler_params=_params(("parallel", "arbitrary")),
        name="odd_in_proj_rope",
    )(h, mod, w, cos_t, sin_t)


def _rope_tables(L, T):
    nf = 16
    inv = ROPE_THETA ** (-jnp.arange(nf, dtype=F32) / nf)
    t = jnp.arange(T)
    ang_r = (t // GRID_W).astype(F32)[:, None] * inv
    ang_c = (t % GRID_W).astype(F32)[:, None] * inv
    cos64 = jnp.concatenate([jnp.cos(ang_r), jnp.cos(ang_r), jnp.cos(ang_c), jnp.cos(ang_c)], -1)
    sin64 = jnp.concatenate([-jnp.sin(ang_r), jnp.sin(ang_r), -jnp.sin(ang_c), jnp.sin(ang_c)], -1)
    cos_t = jnp.concatenate([jnp.ones((L, 128), F32), jnp.tile(cos64, (1, 2))], 0)
    sin_t = jnp.concatenate([jnp.zeros((L, 128), F32), jnp.tile(sin64, (1, 2))], 0)
    return cos_t, sin_t


def _da_kernel(q_ref, k_ref, v_ref, lq1_ref, lk1_ref, lq2_ref, lk2_ref, sg_ref, o_ref, m_scr, acc_scr, s_scr, va_scr,
               *, L, tk, n_chunks, lam_init):
    i = pl.program_id(2)
    hs = range(DA_NH)
    hl = [slice(128 * hh, 128 * (hh + 1)) for hh in hs]

    @pl.when(i == 0)
    def _():
        for hh in hs:
            va_scr[hh, :, 0:128] = v_ref[:, hl[hh]]
            va_scr[hh, :, 128:256] = jnp.ones((va_scr.shape[1], 128), va_scr.dtype)

    lane = lax.broadcasted_iota(jnp.int32, (DA_TQ, 128), 1)
    lo = lane < 64
    qq = []
    for hh in hs:
        q = q_ref[:, hl[hh]]
        zero = jnp.zeros_like(q)
        qq.append(jnp.concatenate([jnp.where(lo, q, zero), jnp.where(lo, zero, q)], axis=0))

    def rows(j):
        return slice(j * tk, (j + 1) * tk)

    def scores(hh, j):
        return _dot_nt(qq[hh], k_ref[rows(j), hl[hh]])

    def softmax_pv(hh, s_ref, vb, first):
        ncol = s_ref.shape[1] // 128
        mx = s_ref[:, 0:128]
        for cb in range(1, ncol):
            mx = jnp.maximum(mx, s_ref[:, cb * 128:(cb + 1) * 128])
        m_cur = jnp.max(mx, axis=-1, keepdims=True)
        if first:
            m_new = jnp.broadcast_to(m_cur, m_scr.shape[1:])
            acc_scr[hh] = _dot(jnp.exp2(s_ref[...] - m_cur), vb)
        else:
            m_old = m_scr[hh]
            m_new = jnp.maximum(m_old, m_cur)
            alpha = jnp.exp2(m_old - m_new)
            p = jnp.exp2(s_ref[...] - jnp.tile(m_new, (1, ncol)))
            acc_scr[hh] = jnp.tile(alpha, (1, 2)) * acc_scr[hh] + _dot(p, vb)
        m_scr[hh] = m_new

    @pl.when(i == 0)
    def _():
        for hh in hs:
            s_scr[hh, 0, :, 0:L] = _dot_nt(qq[hh], k_ref[0:L, hl[hh]])
        for hh in hs:
            softmax_pv(hh, s_scr.at[hh, 0, :, 0:L], va_scr[hh, 0:L, :], True)

    @pl.when(i > 0)
    def _():
        for hh in hs:
            s_scr[hh, 0] = scores(hh, 0)
        for j in range(n_chunks):
            for hh in hs:
                if j + 1 < n_chunks:
                    s_scr[hh, (j + 1) % DA_NBUF] = scores(hh, j + 1)
                softmax_pv(hh, s_scr.at[hh, j % DA_NBUF], va_scr[hh, rows(j), :], j == 0)

    lam = (jnp.exp(jnp.sum(lq1_ref[...] * lk1_ref[...], axis=-1, keepdims=True))
           - jnp.exp(jnp.sum(lq2_ref[...] * lk2_ref[...], axis=-1, keepdims=True)) + lam_init)
    for hh in hs:
        o_all = acc_scr[hh, :, 0:128] * (1.0 / acc_scr[hh, :, 128:256])
        o = o_all[0:DA_TQ] - lam * o_all[DA_TQ:]
        o = o * lax.rsqrt(jnp.mean(o * o, axis=-1, keepdims=True) + 1e-5) * sg_ref[...] * (1.0 - lam_init)
        o_ref[:, hl[hh]] = o.astype(o_ref.dtype)


def _da_call(qkv, lq1, lk1, lq2, lk2, sg, L, lam_init):
    B, S, _ = qkv.shape
    tk = next(t for t in DA_TK if S % t == 0 and t >= L)
    n_chunks = S // tk
    nh = DA_HEADS // DA_NH
    w = 128 * DA_NH
    full = lambda shape: pl.BlockSpec(shape, lambda b, h, i: (0,) * len(shape))
    return pl.pallas_call(
        functools.partial(_da_kernel, L=L, tk=tk, n_chunks=n_chunks, lam_init=lam_init),
        grid=(B, nh, S // DA_TQ),
        in_specs=[pl.BlockSpec((None, DA_TQ, w), lambda b, h, i: (b, i, h)),
                  pl.BlockSpec((None, S, w), lambda b, h, i: (b, 0, nh + h)),
                  pl.BlockSpec((None, S, w), lambda b, h, i: (b, 0, 2 * nh + h)),
                  full((1, 64)), full((1, 64)), full((1, 64)), full((1, 64)), full((1, 128))],
        out_specs=pl.BlockSpec((None, DA_TQ, w), lambda b, h, i: (b, i, h)),
        out_shape=jax.ShapeDtypeStruct((B, S, DA_W), MM),
        scratch_shapes=[pltpu.VMEM((DA_NH, 2 * DA_TQ, 128), F32), pltpu.VMEM((DA_NH, 2 * DA_TQ, 256), F32),
                        pltpu.VMEM((DA_NH, DA_NBUF, 2 * DA_TQ, tk), F32), pltpu.VMEM((DA_NH, S, 256), MM)],
        compiler_params=_params(("arbitrary", "arbitrary", "arbitrary")),
        name="diff_attention",
    )(qkv, qkv, qkv, lq1, lk1, lq2, lk2, sg)


def _route(logits_t, rb):
    s = _sigmoid(logits_t)
    sel = s + rb
    x = [sel[e:e + 1, :] for e in range(N_EXPERTS)]
    tg = []
    for g in range(N_GROUPS):
        best = None
        for a in range(EPG):
            for b in range(a + 1, EPG):
                pair = x[EPG * g + a] + x[EPG * g + b]
                best = pair if best is None else jnp.maximum(best, pair)
        tg.append(best)
    rows = []
    gsel = []
    for g in range(N_GROUPS):
        chosen = None
        for g2 in range(N_GROUPS):
            if g2 == g:
                continue
            c = (tg[g] > tg[g2]) if g2 < g else (tg[g] >= tg[g2])
            chosen = c if chosen is None else jnp.logical_and(chosen, c)
        for a in range(EPG):
            rank = jnp.zeros_like(x[0])
            for b in range(EPG):
                if b == a:
                    continue
                ahead = (x[EPG * g + b] >= x[EPG * g + a]) if b < a else (x[EPG * g + b] > x[EPG * g + a])
                rank = rank + jnp.where(ahead, 1.0, 0.0)
            picked = jnp.logical_and(chosen, rank < 1.5)
            rows.append(jnp.where(picked, s[EPG * g + a:EPG * g + a + 1, :], 0.0))
        gsel.append(jnp.where(chosen, 1.0, 0.0))
    comb = jnp.concatenate(rows, axis=0)
    return comb * (1.0 / jnp.sum(comb, axis=0, keepdims=True)), jnp.concatenate(gsel, axis=0)


def _router_kernel(h_ref, mod_ref, rwt_ref, rb_ref, u_ref, rt_ref, *, nB, tile0, L):
    b = pl.program_id(0)
    i = pl.program_id(1)
    tm = h_ref.shape[0]
    mt = functools.partial(_mod_tile, mod_ref, b, i, nB=nB, tm=tm, L=L, tile0=tile0)
    u = h_ref[...] * (1.0 + mt(4)) + mt(3)
    u_ref[...] = u.astype(MM)
    logits_t = lax.dot_general(rwt_ref[...], u, (((1,), (1,)), ((), ())), precision=HIGHEST,
                               preferred_element_type=F32)
    comb_t, gsel_t = _route(logits_t, rb_ref[...])
    pad = jnp.zeros((MOE_RT_ROWS - N_EXPERTS - N_GROUPS, tm), F32)
    rt_ref[...] = jnp.concatenate([comb_t, gsel_t, pad], axis=0)


def _router_call(h, mod, rwt, rb, nB, tm, tile0, L):
    B, S, _ = h.shape
    R = mod.shape[0]
    nt = S // tm - tile0
    full = lambda shape: pl.BlockSpec(shape, lambda b, i: (0,) * len(shape))
    return pl.pallas_call(
        functools.partial(_router_kernel, nB=nB, tile0=tile0, L=L),
        grid=(B, nt),
        in_specs=[pl.BlockSpec((None, tm, D), lambda b, i: (b, i + tile0, 0)),
                  full((R, 6 * D)), full((N_EXPERTS, D)), full((N_EXPERTS, 1))],
        out_specs=[pl.BlockSpec((None, tm, D), lambda b, i: (b, i, 0)),
                   pl.BlockSpec((MOE_RT_ROWS, tm), lambda b, i: (0, b * nt + i))],
        out_shape=[jax.ShapeDtypeStruct((B, nt * tm, D), MM),
                   jax.ShapeDtypeStruct((MOE_RT_ROWS, B * nt * tm), F32)],
        compiler_params=_params(("parallel", "arbitrary")),
        name="moe_router",
    )(h, mod, rwt, rb)


def _expert_kernel(cnt_ref, u_ref, rt_ref, wg_ref, wu_ref, wd_ref, f_ref, xs_scr, cs_scr, fs_scr, pos_scr,
                   *, TE, NP, nt):
    t = pl.program_id(0)
    g = pl.program_id(1)
    lane = lax.broadcasted_iota(jnp.int32, (MOE_BLK, 128), 1)
    for hf in range(NP):
        base = hf * TE
        n = [cnt_ref[gg * nt + t * NP + hf] for gg in range(N_GROUPS)]
        off = [jnp.int32(0)]
        for gg in range(N_GROUPS - 1):
            off.append(off[-1] + n[gg])

        @pl.when(g == 0)
        def _(base=base, off=off):
            rt = rt_ref[:, base:base + TE]
            gs = rt[N_EXPERTS:N_EXPERTS + 8, :]
            r_i = lax.broadcasted_iota(jnp.int32, (TE, TE), 0)
            c_i = lax.broadcasted_iota(jnp.int32, (TE, TE), 1)
            before = jnp.where(r_i < c_i, 1.0, 0.0).astype(MM)
            cnt_before = jnp.dot(gs.astype(MM), before, preferred_element_type=F32)
            pos = jnp.zeros((1, TE), F32)
            for gg in range(N_GROUPS):
                pos = pos + gs[gg:gg + 1, :] * (off[gg].astype(F32) + cnt_before[gg:gg + 1, :])
            perm = jnp.where(r_i.astype(F32) == pos, 1.0, 0.0).astype(MM)
            xs_scr[base:base + TE, :] = jnp.dot(perm, u_ref[base:base + TE, :],
                                                preferred_element_type=F32).astype(MM)
            rt_pad = jnp.concatenate([rt, jnp.zeros((128 - MOE_RT_ROWS, TE), F32)], axis=0)
            hi, lo = _split(rt_pad)
            cs_scr[base:base + TE, :] = _dot_nt(perm, hi) + _dot_nt(perm, lo)
            pos_scr[base:base + TE, :] = jnp.broadcast_to(pos, (128, TE)).T
            fs_scr[base:base + TE, :] = jnp.zeros((TE, D), F32)

        lo_g = jnp.int32(0)
        n_g = jnp.int32(0)
        for gg in range(N_GROUPS):
            lo_g = jnp.where(g == gg, off[gg], lo_g)
            n_g = jnp.where(g == gg, n[gg], n_g)
        hi_g = lo_g + n_g
        for rb in range(TE // MOE_BLK):
            @pl.when(jnp.logical_and(lo_g < MOE_BLK * (rb + 1), hi_g > MOE_BLK * rb))
            def _(r0=base + rb * MOE_BLK):
                rows = slice(r0, r0 + MOE_BLK)
                xb = xs_scr[rows, :]
                c = cs_scr[rows, :]
                hg = [jnp.dot(xb, wg_ref[e], preferred_element_type=F32) for e in range(EPG)]
                hu = [jnp.dot(xb, wu_ref[e], preferred_element_type=F32) for e in range(EPG)]
                col = [jnp.sum(jnp.where(lane == EPG * g + e, c, 0.0), axis=-1, keepdims=True) for e in range(EPG)]
                he = jnp.concatenate([(hg[e] * _sigmoid(hg[e]) * hu[e] * col[e]).astype(MM) for e in range(EPG)],
                                     axis=1)
                fs_scr[rows, :] += jnp.dot(he, wd_ref[...], preferred_element_type=F32)

        @pl.when(g == N_GROUPS - 1)
        def _(base=base):
            c_i = lax.broadcasted_iota(jnp.int32, (TE, TE), 1)
            pos_t = jnp.tile(pos_scr[base:base + TE, :], (1, TE // 128))
            unperm = jnp.where(c_i.astype(F32) == pos_t, 1.0, 0.0).astype(MM)
            f_ref[base:base + TE, :] = jnp.dot(unperm, fs_scr[base:base + TE, :].astype(MM),
                                               preferred_element_type=F32).astype(f_ref.dtype)


def _expert_call(u, rt, wg, wu, wd):
    N = u.shape[0]
    TE = MOE_TE if N % MOE_TE == 0 else TM
    nt = N // TE
    NP = MOE_NP if nt % MOE_NP == 0 else 1
    cnt = jnp.sum(rt[N_EXPERTS:N_EXPERTS + N_GROUPS].reshape(N_GROUPS, nt, TE), axis=-1)
    cnt = cnt.astype(jnp.int32).reshape(N_GROUPS * nt)
    gw = EPG * D_EXPERT
    grid_spec = pltpu.PrefetchScalarGridSpec(
        num_scalar_prefetch=1,
        grid=(nt // NP, N_GROUPS),
        in_specs=[pl.BlockSpec((NP * TE, D), lambda t, g, c: (t, 0)),
                  pl.BlockSpec((MOE_RT_ROWS, NP * TE), lambda t, g, c: (0, t)),
                  pl.BlockSpec((None, EPG, D, D_EXPERT), lambda t, g, c: (g, 0, 0, 0)),
                  pl.BlockSpec((None, EPG, D, D_EXPERT), lambda t, g, c: (g, 0, 0, 0)),
                  pl.BlockSpec((None, gw, D), lambda t, g, c: (g, 0, 0))],
        out_specs=pl.BlockSpec((NP * TE, D), lambda t, g, c: (t, 0)),
        scratch_shapes=[pltpu.VMEM((NP * TE, D), MM), pltpu.VMEM((NP * TE, 128), F32),
                        pltpu.VMEM((NP * TE, D), F32), pltpu.VMEM((NP * TE, 128), F32)])
    return pl.pallas_call(
        functools.partial(_expert_kernel, TE=TE, NP=NP, nt=nt),
        grid_spec=grid_spec,
        out_shape=jax.ShapeDtypeStruct((N, D), MM),
        compiler_params=_params(("parallel", "arbitrary")),
        name="moe_experts",
    )(cnt, u, rt, wg, wu, wd)


def _moe_ln_kernel(h_ref, mod_ref, f_ref, lng_ref, lnb_ref, o_ref, *, nB, tile0, L):
    b = pl.program_id(0)
    i = pl.program_id(1)
    gate = _mod_tile(mod_ref, b, i, 5, nB, h_ref.shape[0], L, tile0)
    _res_ln(h_ref, gate, f_ref[...].astype(F32), lng_ref, lnb_ref, o_ref)


def _moe_ln_call(h, mod, f, lng, lnb, nB, tm, tile0, L):
    B, S, _ = h.shape
    R = mod.shape[0]
    nt = S // tm - tile0
    full = lambda shape: pl.BlockSpec(shape, lambda b, i: (0,) * len(shape))
    return pl.pallas_call(
        functools.partial(_moe_ln_kernel, nB=nB, tile0=tile0, L=L),
        grid=(B, nt),
        in_specs=[pl.BlockSpec((None, tm, D), lambda b, i: (b, i + tile0, 0)),
                  full((R, 6 * D)),
                  pl.BlockSpec((None, tm, D), lambda b, i: (b, i, 0)),
                  full((1, D)), full((1, D))],
        out_specs=pl.BlockSpec((None, tm, D), lambda b, i: (b, i, 0)),
        out_shape=jax.ShapeDtypeStruct((B, nt * tm, D), F32),
        compiler_params=_params(("parallel", "arbitrary")),
        name="moe_res_ln",
    )(h, mod, f, lng, lnb)


def _moe_call(h, mod, rwt, rb, wg, wu, wd, lng, lnb, nB, latent_only, L):
    B, S, _ = h.shape
    tm, tile0 = (TM, L // TM) if latent_only else (_row_tile(S), 0)
    u, rt = _router_call(h, mod, rwt, rb, nB, tm, tile0, L)
    f = _expert_call(u.reshape(-1, D), rt, wg.reshape(N_GROUPS, EPG, D, D_EXPERT),
                     wu.reshape(N_GROUPS, EPG, D, D_EXPERT), wd.reshape(N_GROUPS, EPG * D_EXPERT, D))
    return _moe_ln_call(h, mod, f.reshape(B, -1, D), lng, lnb, nB, tm, tile0, L)


def kernel(x, c, ctx, c_ctx, ada_w, ada_b, ln_g, ln_b, even_w_in, even_w_out, shift_w, na_rpb, rw_w0, rw_w_up, rw_a0, rw_a_up, rw_g_up, rw_k_k, rw_k_a, rw_r_k, rw_gn_g, rw_gn_b, odd_w_in, odd_w_out, da_lq1, da_lk1, da_lq2, da_lk2, da_subln_g, router_w, router_b, exp_w_gate, exp_w_up, exp_w_down):
    B, T, _ = x.shape
    L = ctx.shape[1]
    assert L == TM and T % TM == 0 and (T // GRID_W) >= 12
    rows = T // GRID_W
    R = -(-(B + 1) // 8) * 8

    cv = jnp.zeros((R, D), F32).at[:B].set(c).at[B].set(c_ctx)
    mods = _ada_call(cv, ada_w, ada_b)

    cos_t, sin_t = _rope_tables(L, T)
    rwt = router_w.T
    rb = router_b.reshape(N_EXPERTS, 1)
    zpad = jnp.zeros((2, 64, RW_W), F32)

    h = jnp.concatenate([ctx, x], axis=1)
    for l in range(DEPTH):
        mod = mods[l]
        i = l // 2
        lng = ln_g[l].reshape(2, 1, D)
        lnb = ln_b[l].reshape(2, 1, D)
        if l % 2 == 0:
            w_in = even_w_in[i].astype(MM)
            qkv, prw = _even_in_call(h, mod, w_in[:, :3 * NA_W], w_in[:, 3 * NA_W:], B, L)
            ona = _na_call(qkv, _na_bias(na_rpb[i], rows), L)
            yf, yr, bg = _rw_call(
                prw, shift_w[i], rw_w0[i].reshape(2, 1, RW_W),
                jnp.concatenate([rw_w_up[i], zpad], axis=1).astype(MM), rw_a0[i].reshape(2, 1, RW_W),
                jnp.concatenate([zpad, rw_a_up[i]], axis=1).astype(MM), rw_g_up[i].astype(MM),
                rw_k_k[i].reshape(1, RW_W), rw_k_a[i].reshape(1, RW_W), rw_r_k[i].reshape(1, RW_W), L)
            h = _even_out_call(h, mod, ona, yf, yr, bg, rw_gn_g[i].reshape(1, RW_W), rw_gn_b[i].reshape(1, RW_W),
                               even_w_out[i].astype(MM), lng[0], lnb[0], B, L)
        else:
            lam_init = 0.8 - 0.6 * math.exp(-0.3 * l)
            qkv = _odd_in_call(h, mod, odd_w_in[i].astype(MM), cos_t, sin_t, B, L)
            oda = _da_call(qkv, da_lq1[i].reshape(1, 64), da_lk1[i].reshape(1, 64), da_lq2[i].reshape(1, 64),
                           da_lk2[i].reshape(1, 64), da_subln_g[i].reshape(1, 128), L, lam_init)
            h = _odd_out_call(h, mod, oda, odd_w_out[i].astype(MM), lng[0], lnb[0], B, L)
        h = _moe_call(h, mod, rwt, rb, exp_w_gate[l].astype(MM), exp_w_up[l].astype(MM), exp_w_down[l].astype(MM),
                      lng[1], lnb[1], B, l == DEPTH - 1, L)
    return h
```

```python
import functools
import math

import numpy as np
import jax
import jax.numpy as jnp
from jax import lax
from jax.experimental import pallas as pl
from jax.experimental.pallas import tpu as pltpu

F32 = jnp.float32
MM = jnp.bfloat16
HIGHEST = lax.Precision.HIGHEST

D = 1024
DEPTH = 4
GRID_W = 64
NA_HEADS = 8
NA_W = 512
NA_WIN_R = 8
NA_WIN_C = 16
NA_QB = 128
NA_NB = 2
NA_KROWS = 9
NA_NLOC = NA_KROWS * GRID_W
RW_W = 512
RW_HEAD = 64
RW_IN = 1792
RW_GN_EPS = 64e-5
RW_CHUNK = 64
RW_GRP = 256
RW_NSUB = 4
RW_INV_BASE = 8
RW_INV_LEVELS = 3
RW_NMASK = 7 + RW_INV_LEVELS
DA_HEADS = 8
DA_W = 1024
ROPE_THETA = 10000.0
N_EXPERTS = 16
N_GROUPS = 4
EPG = 4
D_EXPERT = 512
MOE_RT_ROWS = 32
MOE_TE = 1024
MOE_NP = 1
MOE_BLK = 128
ALPHA = (2 * DEPTH) ** 0.25
LN_EPS = 1e-5
TM = 256
TM_BIG = 768
DA_TQ = 256
DA_NH = 2
DA_TK = (768, 512, 256)
DA_NBUF = 3
DA_QSCALE = 0.125 * math.log2(math.e)
NEG = -1e30
VMEM_LIMIT = 56 * 1024 * 1024


def _dot(a, b):
    return jnp.dot(a.astype(MM), b.astype(MM), preferred_element_type=F32)


def _dot_nt(a, b):
    return lax.dot_general(a.astype(MM), b.astype(MM), (((1,), (1,)), ((), ())), preferred_element_type=F32)


def _split(x):
    hi = x.astype(MM)
    lo = (x - hi.astype(F32)).astype(MM)
    return hi, lo


def _dot_split_lhs(x, w):
    hi, lo = _split(x)
    return jnp.dot(hi, w, preferred_element_type=F32) + jnp.dot(lo, w, preferred_element_type=F32)


def _dot_split_rhs(w, x):
    hi, lo = _split(x)
    return jnp.dot(w, hi, preferred_element_type=F32) + jnp.dot(w, lo, preferred_element_type=F32)


def _sigmoid(x):
    return 1.0 / (1.0 + jnp.exp(-x))


def _params(sem):
    return pltpu.CompilerParams(dimension_semantics=sem, vmem_limit_bytes=VMEM_LIMIT)


def _mod_row(mod_ref, row, j):
    return mod_ref[pl.ds(row, 1), j * D:(j + 1) * D]


def _mod_tile(mod_ref, b, i, j, nB, tm, L, tile0=0):
    if L % tm == 0:
        return _mod_row(mod_ref, jnp.where(i + tile0 < L // tm, nB, b), j)
    assert tile0 == 0 and tm > L
    rid = lax.broadcasted_iota(jnp.int32, (tm, 1), 0)
    is_ctx = jnp.logical_and(i == 0, rid < L)
    return jnp.where(is_ctx, _mod_row(mod_ref, nB, j), _mod_row(mod_ref, b, j))


def _row_tile(S):
    return TM_BIG if S % TM_BIG == 0 else TM


def _layer_norm(z, g, b):
    mu = jnp.mean(z, axis=-1, keepdims=True)
    zc = z - mu
    var = jnp.mean(zc * zc, axis=-1, keepdims=True)
    return zc * lax.rsqrt(var + LN_EPS) * g + b


def _ada_kernel(cv_ref, w_ref, b_ref, o_ref):
    x = cv_ref[...]
    x = x * _sigmoid(x)
    o_ref[...] = jnp.dot(x, w_ref[...], precision=HIGHEST, preferred_element_type=F32) + b_ref[...]


def _ada_call(cv, ada_w, ada_b):
    R = cv.shape[0]
    tn = 1536
    return pl.pallas_call(
        _ada_kernel,
        grid=(DEPTH, 6 * D // tn),
        in_specs=[pl.BlockSpec((R, D), lambda l, j: (0, 0)),
                  pl.BlockSpec((None, D, tn), lambda l, j: (l, 0, j)),
                  pl.BlockSpec((None, 1, tn), lambda l, j: (l, 0, j))],
        out_specs=pl.BlockSpec((None, R, tn), lambda l, j: (l, 0, j)),
        out_shape=jax.ShapeDtypeStruct((DEPTH, R, 6 * D), F32),
        compiler_params=_params(("arbitrary", "arbitrary")),
        name="ada_mod",
    )(cv, ada_w, ada_b.reshape(DEPTH, 1, 6 * D))


def _even_in_kernel(h_ref, mod_ref, wna_ref, wrw_ref, qkv_ref, prw_ref, *, nB, L):
    b = pl.program_id(0)
    i = pl.program_id(1)
    mt = functools.partial(_mod_tile, mod_ref, b, i, nB=nB, tm=h_ref.shape[0], L=L)
    u = (h_ref[...] * (1.0 + mt(1)) + mt(0)).astype(MM)
    res = jnp.dot(u, wna_ref[...], preferred_element_type=F32)
    qkv_ref[:, 0:NA_W] = (res[:, 0:NA_W] * 0.125).astype(qkv_ref.dtype)
    qkv_ref[:, NA_W:] = res[:, NA_W:].astype(qkv_ref.dtype)
    prw_ref[...] = jnp.dot(u, wrw_ref[...], preferred_element_type=F32)


def _even_in_call(h, mod, w_na, w_rw, nB, L):
    B, S, _ = h.shape
    R = mod.shape[0]
    tm = _row_tile(S)
    return pl.pallas_call(
        functools.partial(_even_in_kernel, nB=nB, L=L),
        grid=(B, S // tm),
        in_specs=[pl.BlockSpec((None, tm, D), lambda b, i: (b, i, 0)),
                  pl.BlockSpec((R, 6 * D), lambda b, i: (0, 0)),
                  pl.BlockSpec((D, 3 * NA_W), lambda b, i: (0, 0)),
                  pl.BlockSpec((D, RW_IN), lambda b, i: (0, 0))],
        out_specs=[pl.BlockSpec((None, tm, 3 * NA_W), lambda b, i: (b, i, 0)),
                   pl.BlockSpec((None, tm, RW_IN), lambda b, i: (b, i, 0))],
        out_shape=[jax.ShapeDtypeStruct((B, S, 3 * NA_W), MM),
                   jax.ShapeDtypeStruct((B, S, RW_IN), F32)],
        compiler_params=_params(("parallel", "arbitrary")),
        name="even_in_proj",
    )(h, mod, w_na, w_rw)


def _na_bias(rpb, rows):
    cq = np.arange(GRID_W)[:, None]
    ck = np.arange(GRID_W)[None, :]
    cs = np.clip(cq - NA_WIN_C // 2, 0, GRID_W - NA_WIN_C)
    ok_c = (ck >= cs) & (ck < cs + NA_WIN_C)
    dc = np.clip(ck - cq + NA_WIN_C - 1, 0, 2 * NA_WIN_C - 2)
    onehot = np.zeros((2 * NA_WIN_C - 1, GRID_W * GRID_W), np.float32)
    onehot[dc.ravel(), np.arange(GRID_W * GRID_W)] = 1.0
    rc = jnp.einsum('hrd,dn->hrn', rpb, jnp.asarray(onehot), precision=HIGHEST)
    rc = rc.reshape(rpb.shape[0], 2 * NA_WIN_R - 1, GRID_W, GRID_W)
    cases = []
    for r0 in (0, 2, 4, rows - 4, rows - 2):
        ws = int(np.clip(r0 - NA_WIN_R // 2, 0, rows - NA_KROWS))
        qrows = []
        for qi in range(NA_QB // GRID_W):
            rq = r0 + qi
            rs = int(np.clip(rq - NA_WIN_R // 2, 0, rows - NA_WIN_R))
            pieces = []
            for j in range(NA_KROWS):
                rk = ws + j
                if rs <= rk < rs + NA_WIN_R:
                    pieces.append(jnp.where(ok_c[None], rc[:, rk - rq + NA_WIN_R - 1], NEG))
                else:
                    pieces.append(jnp.full((rpb.shape[0], GRID_W, GRID_W), NEG, F32))
            qrows.append(jnp.concatenate(pieces, axis=-1))
        cases.append(jnp.concatenate(qrows, axis=-2))
    return jnp.stack(cases, axis=0)


def _na_kernel(q_ref, k_ref, v_ref, *rest, L, rows):
    bias_refs, o_ref = rest[:NA_NB], rest[NA_NB]
    i = pl.program_id(2)
    nq = L // (NA_NB * NA_QB)
    lane = lax.broadcasted_iota(jnp.int32, (NA_QB, 128), 1)
    lo = lane < 64
    kc = k_ref[0:L, :]
    vc = v_ref[0:L, :]
    ch = [(n, hh) for n in range(NA_NB) for hh in range(2)]
    qs = [q_ref[n * NA_QB:(n + 1) * NA_QB, :] for n in range(NA_NB)]
    qm = {(n, hh): jnp.where(lo if hh == 0 else jnp.logical_not(lo), qs[n], jnp.zeros_like(qs[n])) for n, hh in ch}

    def store(outs):
        for n in range(NA_NB):
            o_ref[n * NA_QB:(n + 1) * NA_QB, :] = jnp.where(lo, outs[(n, 0)], outs[(n, 1)]).astype(o_ref.dtype)

    @pl.when(i < nq)
    def _():
        s = {c: _dot_nt(qm[c], kc) for c in ch}
        m = {c: jnp.max(s[c], axis=-1, keepdims=True) for c in ch}
        p = {c: jnp.exp(s[c] - m[c]) for c in ch}
        l = {c: jnp.sum(p[c], axis=-1, keepdims=True) for c in ch}
        store({c: _dot(p[c], vc) * (1.0 / l[c]) for c in ch})

    @pl.when(i >= nq)
    def _():
        kl, vl = [], []
        for n in range(NA_NB):
            r0 = 2 * ((i - nq) * NA_NB + n)
            ws = jnp.clip(r0 - NA_WIN_R // 2, 0, rows - NA_KROWS)
            start = pl.multiple_of(L + GRID_W * ws, GRID_W)
            kl.append(k_ref[pl.ds(start, NA_NLOC), :])
            vl.append(v_ref[pl.ds(start, NA_NLOC), :])
        s_loc = {c: _dot_nt(qm[c], kl[c[0]]) + bias_refs[c[0]][c[1]] for c in ch}
        s_ctx = {c: _dot_nt(qm[c], kc) for c in ch}
        m = {c: jnp.maximum(jnp.max(s_loc[c], axis=-1, keepdims=True), jnp.max(s_ctx[c], axis=-1, keepdims=True))
             for c in ch}
        p_loc = {c: jnp.exp(s_loc[c] - m[c]) for c in ch}
        p_ctx = {c: jnp.exp(s_ctx[c] - m[c]) for c in ch}
        l = {c: jnp.sum(p_loc[c], axis=-1, keepdims=True) + jnp.sum(p_ctx[c], axis=-1, keepdims=True) for c in ch}
        store({c: (_dot(p_loc[c], vl[c[0]]) + _dot(p_ctx[c], vc)) * (1.0 / l[c]) for c in ch})


def _na_call(qkv, bias, L):
    B, S, _ = qkv.shape
    rows = (S - L) // GRID_W
    tq = NA_NB * NA_QB
    assert L % tq == 0 and (S - L) % tq == 0
    nq = L // tq
    nhp = NA_HEADS // 2

    def bias_idx(n):
        def idx(b, hp, i):
            r0 = 2 * ((i - nq) * NA_NB + n)
            c = jnp.where(r0 == 0, 0,
                          jnp.where(r0 == 2, 1, jnp.where(r0 == rows - 4, 3, jnp.where(r0 == rows - 2, 4, 2))))
            return (jnp.where(i < nq, 2, c), hp, 0, 0)
        return idx

    return pl.pallas_call(
        functools.partial(_na_kernel, L=L, rows=rows),
        grid=(B, nhp, S // tq),
        in_specs=[pl.BlockSpec((None, tq, 128), lambda b, hp, i: (b, i, hp)),
                  pl.BlockSpec((None, S, 128), lambda b, hp, i: (b, 0, nhp + hp)),
                  pl.BlockSpec((None, S, 128), lambda b, hp, i: (b, 0, 2 * nhp + hp))]
                 + [pl.BlockSpec((None, 2, NA_QB, NA_NLOC), bias_idx(n)) for n in range(NA_NB)],
        out_specs=pl.BlockSpec((None, tq, 128), lambda b, hp, i: (b, i, hp)),
        out_shape=jax.ShapeDtypeStruct((B, S, NA_W), MM),
        compiler_params=_params(("parallel", "parallel", "arbitrary")),
        name="na_attention",
    )(qkv, qkv, qkv, *([bias] * NA_NB))


def _rw_conv(p_ref, prev_ref, next_ref, c, conv_ref, nL, NC):
    p = p_ref[...]
    at_start = jnp.logical_or(c == 0, c == nL)
    at_end = jnp.logical_or(c == nL - 1, c == NC - 1)
    prow = jnp.where(at_start, 0.0, prev_ref[7:8, :])
    nrow = jnp.where(at_end, 0.0, next_ref[0:1, :])
    rid = lax.broadcasted_iota(jnp.int32, p.shape, 0)
    last = p.shape[0] - 1
    pm = jnp.where(rid == 0, prow, pltpu.roll(p, 1, axis=0))
    pp = jnp.where(rid == last, nrow, pltpu.roll(p, last, axis=0))
    return pm * conv_ref[0:1, :] + p * conv_ref[1:2, :] + pp * conv_ref[2:3, :]


def _rw_prep(pc, d, w0_ref, wup_ref, a0_ref, aup_ref, kk_ref, ka_ref, ones_ref, cum_ref):
    r = pc[:, 0:RW_W]
    k = pc[:, RW_W:2 * RW_W]
    v = pc[:, 2 * RW_W:3 * RW_W]
    xwa = pc[:, 3 * RW_W:3 * RW_W + 128]
    z = w0_ref[d] + _dot(jnp.tanh(xwa), wup_ref[d])
    lw = -math.exp(-0.5) * _sigmoid(z)
    a = _sigmoid(a0_ref[d] + _dot(xwa, aup_ref[d]))
    kk = k * kk_ref[...]
    n2 = _dot_split_lhs(kk * kk, ones_ref[...])
    kk = kk * (1.0 / jnp.maximum(jnp.sqrt(n2), 1e-12))
    keff = k * (1.0 + (a - 1.0) * ka_ref[...])
    av = -kk
    bv = kk * a
    lc = _dot_split_rhs(cum_ref[d], lw)
    ltot = lc[RW_CHUNK - 1:RW_CHUNK, :] if d == 0 else lc[0:1, :]
    at = av * jnp.exp(lc - lw)
    rt = r * jnp.exp(lc)
    ginv = jnp.exp(-lc)
    bt = bv * ginv
    kt = keff * ginv
    grest = jnp.exp(ltot - lc)
    bh = bv * grest
    kh = keff * grest
    gc = jnp.exp(ltot)

    return dict(at=at, rt=rt, bt=bt, kt=kt, bh=bh, kh=kh, v=v, gc=gc, keff=keff, r=r)


def _rw_chains(preps, mask_ref, g_scr):
    bdm = mask_ref[0].astype(MM)
    eye = mask_ref[1]
    ngrp = RW_W // RW_GRP
    order = {0: list(range(RW_NSUB)), 1: list(range(RW_NSUB - 1, -1, -1))}
    chains = [(d, g, order[d][n]) for n in range(RW_NSUB) for d in range(2) for g in range(ngrp)]
    ex = {}
    for c in chains:
        d, g, slot = c
        sl = slice(g * RW_GRP, (g + 1) * RW_GRP)
        p = preps[(d, slot)]
        e = {}
        for n in ("at", "rt", "bt", "kt", "bh", "kh", "v"):
            z = p[n][:, sl].astype(MM)
            e[n] = jnp.concatenate([z, z, z, z], axis=0) * bdm
        e["vt"] = e["v"].T
        e["gc"] = p["gc"][:, sl]
        ex[c] = e
    xab, xrb, xak, xrk = {}, {}, {}, {}
    for c in chains:
        ar = jnp.concatenate([ex[c]["at"], ex[c]["rt"]], axis=0)
        ms = mask_ref[2 + 2 * c[0]]
        mi = mask_ref[3 + 2 * c[0]]
        xb = _dot_nt(ex[c]["bt"], ar)
        xab[c] = (xb[:, 0:RW_GRP] * ms).astype(MM)
        xrb[c] = (xb[:, RW_GRP:] * mi).astype(MM)
        xk = _dot_nt(ex[c]["kt"], ar)
        xak[c] = (xk[:, 0:RW_GRP] * ms).astype(MM)
        xrk[c] = (xk[:, RW_GRP:] * mi).astype(MM)
    eyeb = eye.astype(MM)
    xp = {c: xab[c] * mask_ref[6].astype(MM) for c in chains}
    tt = {c: eyeb + xp[c] for c in chains}
    for _ in range(RW_INV_BASE.bit_length() - 2):
        xp = {c: _dot(xp[c], xp[c]).astype(MM) for c in chains}
        tt = {c: _dot(tt[c], eyeb + xp[c]).astype(MM) for c in chains}
    for lvl in range(RW_INV_LEVELS):
        cm = mask_ref[7 + lvl].astype(MM)
        tc = {c: _dot(tt[c], xab[c] * cm).astype(MM) for c in chains}
        tt = {c: _dot(eyeb + tc[c], tt[c]).astype(MM) for c in chains}
    ys = {}
    for n in range(RW_NSUB):
        cur = [(d, g, order[d][n]) for d in range(2) for g in range(ngrp)]
        gs = {c: g_scr[c[0], c[1]] for c in cur}
        gsb = {c: gs[c].astype(MM) for c in cur}
        w1 = {c: (_dot_nt(gsb[c], ex[c]["at"]) + _dot(ex[c]["vt"], xak[c])).astype(MM) for c in cur}
        ut = {c: _dot(w1[c], tt[c]).astype(MM) for c in cur}
        yt = {c: _dot_nt(gsb[c], ex[c]["rt"]) + _dot(ut[c], xrb[c]) + _dot(ex[c]["vt"], xrk[c]) for c in cur}
        for c in cur:
            g_scr[c[0], c[1]] = gs[c] * ex[c]["gc"] + _dot(ut[c], ex[c]["bh"]) + _dot(ex[c]["vt"], ex[c]["kh"])
        for c in cur:
            ybd = yt[c].T
            ys[c] = ybd[0:64] + ybd[64:128] + ybd[128:192] + ybd[192:256]
    return [jnp.concatenate([jnp.concatenate([ys[(d, g, slot)] for g in range(ngrp)], axis=1)
                             for slot in range(RW_NSUB)], axis=0) for d in range(2)]


def _rw_kernel(pf_ref, pfp_ref, pfn_ref, pr_ref, prp_ref, prn_ref, conv_ref, w0_ref, wup_ref, a0_ref, aup_ref,
               gup_ref, kk_ref, ka_ref, rk_ref, ones_ref, cum_ref, mask_ref, yf_ref, yr_ref, bg_ref, g_scr,
               *, nL, NC):
    i = pl.program_id(1)

    @pl.when(i == 0)
    def _():
        g_scr[...] = jnp.zeros_like(g_scr)

    cf = i
    cr = jnp.where(i < nL, nL - 1 - i, NC - 1 - i + nL)
    args = (w0_ref, wup_ref, a0_ref, aup_ref, kk_ref, ka_ref, ones_ref, cum_ref)

    pcf = _rw_conv(pf_ref, pfp_ref, pfn_ref, cf, conv_ref, nL, NC)
    pcr = _rw_conv(pr_ref, prp_ref, prn_ref, cr, conv_ref, nL, NC)
    preps = {}
    for n in range(RW_NSUB):
        sf, sr = n, RW_NSUB - 1 - n
        preps[(0, sf)] = _rw_prep(pcf[sf * RW_CHUNK:(sf + 1) * RW_CHUNK], 0, *args)
        preps[(1, sr)] = _rw_prep(pcr[sr * RW_CHUNK:(sr + 1) * RW_CHUNK], 1, *args)
    xwa = pcf[:, 3 * RW_W:3 * RW_W + 128]
    a_r = _sigmoid(a0_ref[1] + _dot(xwa, aup_ref[1]))
    keff_r = pcf[:, RW_W:2 * RW_W] * (1.0 + (a_r - 1.0) * ka_ref[...])
    keff_f = jnp.concatenate([preps[(0, s)]["keff"] for s in range(RW_NSUB)], axis=0)
    rv = pcf[:, 0:RW_W] * rk_ref[...]
    bsum = _dot_split_lhs(rv * (keff_f + keff_r), ones_ref[...]) * pcf[:, 2 * RW_W:3 * RW_W]
    bg_ref[:, 0:RW_W] = bsum
    bg_ref[:, RW_W:] = _dot(_sigmoid(pcf[:, 3 * RW_W + 128:]), gup_ref[...])

    y_f, y_r = _rw_chains(preps, mask_ref, g_scr)
    yf_ref[...] = y_f
    yr_ref[...] = y_r


def _rw_masks():
    n = 4 * RW_CHUNK
    idx = np.arange(n)
    hd = idx // RW_CHUNK
    t = idx % RW_CHUNK
    bd = hd[:, None] == hd[None, :]
    m = np.zeros((RW_NMASK, n, n), np.float32)
    m[0] = bd
    m[1] = np.eye(n)
    m[2] = bd & (t[:, None] < t[None, :])
    m[3] = bd & (t[:, None] <= t[None, :])
    m[4] = bd & (t[:, None] > t[None, :])
    m[5] = bd & (t[:, None] >= t[None, :])
    same = lambda w: bd & ((t[:, None] // w) == (t[None, :] // w))
    m[6] = same(RW_INV_BASE)
    for lvl in range(RW_INV_LEVELS):
        w = RW_INV_BASE << lvl
        m[7 + lvl] = same(2 * w) & ~same(w)
    tt = np.arange(RW_CHUNK)
    cum = np.stack([tt[None, :] <= tt[:, None], tt[None, :] >= tt[:, None]]).astype(np.float32)
    ch = np.arange(RW_W) // RW_HEAD
    ones = (ch[:, None] == ch[None, :]).astype(np.float32)
    return m, cum, ones


def _rw_call(prw, conv_w, w0, wup_pad, a0, aup_pad, gup, kk, ka, rk, L):
    B, S, _ = prw.shape
    rb = RW_NSUB * RW_CHUNK
    assert L % rb == 0 and S % rb == 0
    NC = S // rb
    nL = L // rb
    n8 = S // 8
    r8 = rb // 8
    m, cum, ones = _rw_masks()

    def cr_of(i):
        return jnp.where(i < nL, nL - 1 - i, NC - 1 - i + nL)

    full = lambda shape: pl.BlockSpec(shape, lambda b, i: (0,) * len(shape))
    in_specs = [
        pl.BlockSpec((None, rb, RW_IN), lambda b, i: (b, i, 0)),
        pl.BlockSpec((None, 8, RW_IN), lambda b, i: (b, jnp.maximum(i * r8 - 1, 0), 0)),
        pl.BlockSpec((None, 8, RW_IN), lambda b, i: (b, jnp.minimum(i * r8 + r8, n8 - 1), 0)),
        pl.BlockSpec((None, rb, RW_IN), lambda b, i: (b, cr_of(i), 0)),
        pl.BlockSpec((None, 8, RW_IN), lambda b, i: (b, jnp.maximum(cr_of(i) * r8 - 1, 0), 0)),
        pl.BlockSpec((None, 8, RW_IN), lambda b, i: (b, jnp.minimum(cr_of(i) * r8 + r8, n8 - 1), 0)),
        full((3, RW_IN)), full((2, 1, RW_W)), full((2, 128, RW_W)), full((2, 1, RW_W)), full((2, 128, RW_W)),
        full((128, RW_W)), full((1, RW_W)), full((1, RW_W)), full((1, RW_W)),
        full((RW_W, RW_W)), full((2, RW_CHUNK, RW_CHUNK)), full((RW_NMASK, 4 * RW_CHUNK, 4 * RW_CHUNK)),
    ]
    out_specs = [
        pl.BlockSpec((None, rb, RW_W), lambda b, i: (b, i, 0)),
        pl.BlockSpec((None, rb, RW_W), lambda b, i: (b, cr_of(i), 0)),
        pl.BlockSpec((None, rb, 2 * RW_W), lambda b, i: (b, i, 0)),
    ]
    return pl.pallas_call(
        functools.partial(_rw_kernel, nL=nL, NC=NC),
        grid=(B, NC),
        in_specs=in_specs,
        out_specs=out_specs,
        out_shape=[jax.ShapeDtypeStruct((B, S, RW_W), F32),
                   jax.ShapeDtypeStruct((B, S, RW_W), F32),
                   jax.ShapeDtypeStruct((B, S, 2 * RW_W), F32)],
        scratch_shapes=[pltpu.VMEM((2, RW_W // RW_GRP, RW_GRP, RW_GRP), F32)],
        compiler_params=_params(("parallel", "arbitrary")),
        name="rwkv7_chunked",
    )(prw, prw, prw, prw, prw, prw, conv_w, w0, wup_pad, a0, aup_pad, gup, kk, ka, rk,
      jnp.asarray(ones, MM), jnp.asarray(cum, MM), jnp.asarray(m))


def _res_ln(h_ref, gate, y, lng_ref, lnb_ref, o_ref):
    z = ALPHA * h_ref[...] + gate * y
    o_ref[...] = _layer_norm(z, lng_ref[...], lnb_ref[...])


def _even_out_kernel(h_ref, mod_ref, ona_ref, yf_ref, yr_ref, bg_ref, ones_ref, gng_ref, gnb_ref, wo_ref,
                     lng_ref, lnb_ref, o_ref, *, nB, L):
    b = pl.program_id(0)
    i = pl.program_id(1)
    gate = _mod_tile(mod_ref, b, i, 2, nB, h_ref.shape[0], L)
    y = yf_ref[...] + yr_ref[...]
    mu = _dot_split_lhs(y, ones_ref[...]) * (1.0 / RW_HEAD)
    yc = y - mu
    var = _dot_split_lhs(yc * yc, ones_ref[...]) * (1.0 / RW_HEAD)
    yn = yc * lax.rsqrt(var + RW_GN_EPS) * gng_ref[...] + gnb_ref[...]
    orw = (yn + bg_ref[:, 0:RW_W]) * bg_ref[:, RW_W:]
    yy = (jnp.dot(ona_ref[...], wo_ref[0:NA_W, :], preferred_element_type=F32)
          + jnp.dot(orw.astype(MM), wo_ref[NA_W:, :], preferred_element_type=F32))
    _res_ln(h_ref, gate, yy, lng_ref, lnb_ref, o_ref)


def _even_out_call(h, mod, ona, yf, yr, bg, gng, gnb, wo, lng, lnb, nB, L):
    B, S, _ = h.shape
    R = mod.shape[0]
    _, _, ones = _rw_masks()
    tm = _row_tile(S)
    tile = lambda w: pl.BlockSpec((None, tm, w), lambda b, i: (b, i, 0))
    full = lambda shape: pl.BlockSpec(shape, lambda b, i: (0,) * len(shape))
    return pl.pallas_call(
        functools.partial(_even_out_kernel, nB=nB, L=L),
        grid=(B, S // tm),
        in_specs=[tile(D), full((R, 6 * D)), tile(NA_W), tile(RW_W), tile(RW_W), tile(2 * RW_W),
                  full((RW_W, RW_W)), full((1, RW_W)), full((1, RW_W)), full((D, D)), full((1, D)), full((1, D))],
        out_specs=tile(D),
        out_shape=jax.ShapeDtypeStruct((B, S, D), F32),
        compiler_params=_params(("parallel", "arbitrary")),
        name="even_out_proj_ln",
    )(h, mod, ona, yf, yr, bg, jnp.asarray(ones, MM), gng, gnb, wo, lng, lnb)


def _odd_out_kernel(h_ref, mod_ref, oda_ref, wo_ref, lng_ref, lnb_ref, o_ref, *, nB, L):
    b = pl.program_id(0)
    i = pl.program_id(1)
    gate = _mod_tile(mod_ref, b, i, 2, nB, h_ref.shape[0], L)
    yy = jnp.dot(oda_ref[...], wo_ref[...], preferred_element_type=F32)
    _res_ln(h_ref, gate, yy, lng_ref, lnb_ref, o_ref)


def _odd_out_call(h, mod, oda, wo, lng, lnb, nB, L):
    B, S, _ = h.shape
    R = mod.shape[0]
    tm = _row_tile(S)
    tile = lambda w: pl.BlockSpec((None, tm, w), lambda b, i: (b, i, 0))
    full = lambda shape: pl.BlockSpec(shape, lambda b, i: (0,) * len(shape))
    return pl.pallas_call(
        functools.partial(_odd_out_kernel, nB=nB, L=L),
        grid=(B, S // tm),
        in_specs=[tile(D), full((R, 6 * D)), tile(DA_W), full((DA_W, D)), full((1, D)), full((1, D))],
        out_specs=tile(D),
        out_shape=jax.ShapeDtypeStruct((B, S, D), F32),
        compiler_params=_params(("parallel", "arbitrary")),
        name="odd_out_proj_ln",
    )(h, mod, oda, wo, lng, lnb)


def _odd_in_kernel(h_ref, mod_ref, w_ref, cos_ref, sin_ref, o_ref, *, nB, L):
    b = pl.program_id(0)
    i = pl.program_id(1)
    mt = functools.partial(_mod_tile, mod_ref, b, i, nB=nB, tm=h_ref.shape[0], L=L)
    u = (h_ref[...] * (1.0 + mt(1)) + mt(0)).astype(MM)
    res = jnp.dot(u, w_ref[...], preferred_element_type=F32)
    cs = cos_ref[...]
    sn = sin_ref[...]
    lane = lax.broadcasted_iota(jnp.int32, cs.shape, 1)
    first = (lane % 32) < 16
    for j in range(2 * DA_W // 128):
        zj = res[:, j * 128:(j + 1) * 128]
        sw = jnp.where(first, pltpu.roll(zj, 112, axis=1), pltpu.roll(zj, 16, axis=1))
        rot = zj * cs + sw * sn
        if j < DA_W // 128:
            rot = rot * DA_QSCALE
        o_ref[:, j * 128:(j + 1) * 128] = rot.astype(o_ref.dtype)
    o_ref[:, 2 * DA_W:] = res[:, 2 * DA_W:].astype(o_ref.dtype)


def _odd_in_call(h, mod, w, cos_t, sin_t, nB, L):
    B, S, _ = h.shape
    R = mod.shape[0]
    tm = _row_tile(S)
    return pl.pallas_call(
        functools.partial(_odd_in_kernel, nB=nB, L=L),
        grid=(B, S // tm),
        in_specs=[pl.BlockSpec((None, tm, D), lambda b, i: (b, i, 0)),
                  pl.BlockSpec((R, 6 * D), lambda b, i: (0, 0)),
                  pl.BlockSpec((D, 3 * DA_W), lambda b, i: (0, 0)),
                  pl.BlockSpec((tm, 128), lambda b, i: (i, 0)),
                  pl.BlockSpec((tm, 128), lambda b, i: (i, 0))],
        out_specs=pl.BlockSpec((None, tm, 3 * DA_W), lambda b, i: (b, i, 0)),
        out_shape=jax.ShapeDtypeStruct((B, S, 3 * DA_W), MM),
        compiler_params=_params(("parallel", "arbitrary")),
        name="odd_in_proj_rope",
    )(h, mod, w, cos_t, sin_t)


def _rope_tables(L, T):
    nf = 16
    inv = ROPE_THETA ** (-jnp.arange(nf, dtype=F32) / nf)
    t = jnp.arange(T)
    ang_r = (t // GRID_W).astype(F32)[:, None] * inv
    ang_c = (t % GRID_W).astype(F32)[:, None] * inv
    cos64 = jnp.concatenate([jnp.cos(ang_r), jnp.cos(ang_r), jnp.cos(ang_c), jnp.cos(ang_c)], -1)
    sin64 = jnp.concatenate([-jnp.sin(ang_r), jnp.sin(ang_r), -jnp.sin(ang_c), jnp.sin(ang_c)], -1)
    cos_t = jnp.concatenate([jnp.ones((L, 128), F32), jnp.tile(cos64, (1, 2))], 0)
    sin_t = jnp.concatenate([jnp.zeros((L, 128), F32), jnp.tile(sin64, (1, 2))], 0)
    return cos_t, sin_t


def _da_kernel(q_ref, k_ref, v_ref, lq1_ref, lk1_ref, lq2_ref, lk2_ref, sg_ref, o_ref, m_scr, acc_scr, s_scr, va_scr,
               *, L, tk, n_chunks, lam_init):
    i = pl.program_id(2)
    hs = range(DA_NH)
    hl = [slice(128 * hh, 128 * (hh + 1)) for hh in hs]

    @pl.when(i == 0)
    def _():
        for hh in hs:
            va_scr[hh, :, 0:128] = v_ref[:, hl[hh]]
            va_scr[hh, :, 128:256] = jnp.ones((va_scr.shape[1], 128), va_scr.dtype)

    lane = lax.broadcasted_iota(jnp.int32, (DA_TQ, 128), 1)
    lo = lane < 64
    qq = []
    for hh in hs:
        q = q_ref[:, hl[hh]]
        zero = jnp.zeros_like(q)
        qq.append(jnp.concatenate([jnp.where(lo, q, zero), jnp.where(lo, zero, q)], axis=0))

    def rows(j):
        return slice(j * tk, (j + 1) * tk)

    def scores(hh, j):
        return _dot_nt(qq[hh], k_ref[rows(j), hl[hh]])

    def softmax_pv(hh, s_ref, vb, first):
        ncol = s_ref.shape[1] // 128
        mx = s_ref[:, 0:128]
        for cb in range(1, ncol):
            mx = jnp.maximum(mx, s_ref[:, cb * 128:(cb + 1) * 128])
        m_cur = jnp.max(mx, axis=-1, keepdims=True)
        if first:
            m_new = jnp.broadcast_to(m_cur, m_scr.shape[1:])
            acc_scr[hh] = _dot(jnp.exp2(s_ref[...] - m_cur), vb)
        else:
            m_old = m_scr[hh]
            m_new = jnp.maximum(m_old, m_cur)
            alpha = jnp.exp2(m_old - m_new)
            p = jnp.exp2(s_ref[...] - jnp.tile(m_new, (1, ncol)))
            acc_scr[hh] = jnp.tile(alpha, (1, 2)) * acc_scr[hh] + _dot(p, vb)
        m_scr[hh] = m_new

    @pl.when(i == 0)
    def _():
        for hh in hs:
            s_scr[hh, 0, :, 0:L] = _dot_nt(qq[hh], k_ref[0:L, hl[hh]])
        for hh in hs:
            softmax_pv(hh, s_scr.at[hh, 0, :, 0:L], va_scr[hh, 0:L, :], True)

    @pl.when(i > 0)
    def _():
        for hh in hs:
            s_scr[hh, 0] = scores(hh, 0)
        for j in range(n_chunks):
            for hh in hs:
                if j + 1 < n_chunks:
                    s_scr[hh, (j + 1) % DA_NBUF] = scores(hh, j + 1)
                softmax_pv(hh, s_scr.at[hh, j % DA_NBUF], va_scr[hh, rows(j), :], j == 0)

    lam = (jnp.exp(jnp.sum(lq1_ref[...] * lk1_ref[...], axis=-1, keepdims=True))
           - jnp.exp(jnp.sum(lq2_ref[...] * lk2_ref[...], axis=-1, keepdims=True)) + lam_init)
    for hh in hs:
        o_all = acc_scr[hh, :, 0:128] * (1.0 / acc_scr[hh, :, 128:256])
        o = o_all[0:DA_TQ] - lam * o_all[DA_TQ:]
        o = o * lax.rsqrt(jnp.mean(o * o, axis=-1, keepdims=True) + 1e-5) * sg_ref[...] * (1.0 - lam_init)
        o_ref[:, hl[hh]] = o.astype(o_ref.dtype)


def _da_call(qkv, lq1, lk1, lq2, lk2, sg, L, lam_init):
    B, S, _ = qkv.shape
    tk = next(t for t in DA_TK if S % t == 0 and t >= L)
    n_chunks = S // tk
    nh = DA_HEADS // DA_NH
    w = 128 * DA_NH
    full = lambda shape: pl.BlockSpec(shape, lambda b, h, i: (0,) * len(shape))
    return pl.pallas_call(
        functools.partial(_da_kernel, L=L, tk=tk, n_chunks=n_chunks, lam_init=lam_init),
        grid=(B, nh, S // DA_TQ),
        in_specs=[pl.BlockSpec((None, DA_TQ, w), lambda b, h, i: (b, i, h)),
                  pl.BlockSpec((None, S, w), lambda b, h, i: (b, 0, nh + h)),
                  pl.BlockSpec((None, S, w), lambda b, h, i: (b, 0, 2 * nh + h)),
                  full((1, 64)), full((1, 64)), full((1, 64)), full((1, 64)), full((1, 128))],
        out_specs=pl.BlockSpec((None, DA_TQ, w), lambda b, h, i: (b, i, h)),
        out_shape=jax.ShapeDtypeStruct((B, S, DA_W), MM),
        scratch_shapes=[pltpu.VMEM((DA_NH, 2 * DA_TQ, 128), F32), pltpu.VMEM((DA_NH, 2 * DA_TQ, 256), F32),
                        pltpu.VMEM((DA_NH, DA_NBUF, 2 * DA_TQ, tk), F32), pltpu.VMEM((DA_NH, S, 256), MM)],
        compiler_params=_params(("arbitrary", "arbitrary", "arbitrary")),
        name="diff_attention",
    )(qkv, qkv, qkv, lq1, lk1, lq2, lk2, sg)


def _route(logits_t, rb):
    s = _sigmoid(logits_t)
    sel = s + rb
    x = [sel[e:e + 1, :] for e in range(N_EXPERTS)]
    tg = []
    for g in range(N_GROUPS):
        best = None
        for a in range(EPG):
            for b in range(a + 1, EPG):
                pair = x[EPG * g + a] + x[EPG * g + b]
                best = pair if best is None else jnp.maximum(best, pair)
        tg.append(best)
    rows = []
    gsel = []
    for g in range(N_GROUPS):
        chosen = None
        for g2 in range(N_GROUPS):
            if g2 == g:
                continue
            c = (tg[g] > tg[g2]) if g2 < g else (tg[g] >= tg[g2])
            chosen = c if chosen is None else jnp.logical_and(chosen, c)
        for a in range(EPG):
            rank = jnp.zeros_like(x[0])
            for b in range(EPG):
                if b == a:
                    continue
                ahead = (x[EPG * g + b] >= x[EPG * g + a]) if b < a else (x[EPG * g + b] > x[EPG * g + a])
                rank = rank + jnp.where(ahead, 1.0, 0.0)
            picked = jnp.logical_and(chosen, rank < 1.5)
            rows.append(jnp.where(picked, s[EPG * g + a:EPG * g + a + 1, :], 0.0))
        gsel.append(jnp.where(chosen, 1.0, 0.0))
    comb = jnp.concatenate(rows, axis=0)
    return comb * (1.0 / jnp.sum(comb, axis=0, keepdims=True)), jnp.concatenate(gsel, axis=0)


def _router_kernel(h_ref, mod_ref, rwt_ref, rb_ref, u_ref, rt_ref, *, nB, tile0, L):
    b = pl.program_id(0)
    i = pl.program_id(1)
    tm = h_ref.shape[0]
    mt = functools.partial(_mod_tile, mod_ref, b, i, nB=nB, tm=tm, L=L, tile0=tile0)
    u = h_ref[...] * (1.0 + mt(4)) + mt(3)
    u_ref[...] = u.astype(MM)
    logits_t = lax.dot_general(rwt_ref[...], u, (((1,), (1,)), ((), ())), precision=HIGHEST,
                               preferred_element_type=F32)
    comb_t, gsel_t = _route(logits_t, rb_ref[...])
    pad = jnp.zeros((MOE_RT_ROWS - N_EXPERTS - N_GROUPS, tm), F32)
    rt_ref[...] = jnp.concatenate([comb_t, gsel_t, pad], axis=0)


def _router_call(h, mod, rwt, rb, nB, tm, tile0, L):
    B, S, _ = h.shape
    R = mod.shape[0]
    nt = S // tm - tile0
    full = lambda shape: pl.BlockSpec(shape, lambda b, i: (0,) * len(shape))
    return pl.pallas_call(
        functools.partial(_router_kernel, nB=nB, tile0=tile0, L=L),
        grid=(B, nt),
        in_specs=[pl.BlockSpec((None, tm, D), lambda b, i: (b, i + tile0, 0)),
                  full((R, 6 * D)), full((N_EXPERTS, D)), full((N_EXPERTS, 1))],
        out_specs=[pl.BlockSpec((None, tm, D), lambda b, i: (b, i, 0)),
                   pl.BlockSpec((MOE_RT_ROWS, tm), lambda b, i: (0, b * nt + i))],
        out_shape=[jax.ShapeDtypeStruct((B, nt * tm, D), MM),
                   jax.ShapeDtypeStruct((MOE_RT_ROWS, B * nt * tm), F32)],
        compiler_params=_params(("parallel", "arbitrary")),
        name="moe_router",
    )(h, mod, rwt, rb)


def _expert_kernel(cnt_ref, u_ref, rt_ref, wg_ref, wu_ref, wd_ref, f_ref, xs_scr, cs_scr, fs_scr, pos_scr,
                   *, TE, NP, nt):
    t = pl.program_id(0)
    g = pl.program_id(1)
    lane = lax.broadcasted_iota(jnp.int32, (MOE_BLK, 128), 1)
    for hf in range(NP):
        base = hf * TE
        n = [cnt_ref[gg * nt + t * NP + hf] for gg in range(N_GROUPS)]
        off = [jnp.int32(0)]
        for gg in range(N_GROUPS - 1):
            off.append(off[-1] + n[gg])

        @pl.when(g == 0)
        def _(base=base, off=off):
            rt = rt_ref[:, base:base + TE]
            gs = rt[N_EXPERTS:N_EXPERTS + 8, :]
            r_i = lax.broadcasted_iota(jnp.int32, (TE, TE), 0)
            c_i = lax.broadcasted_iota(jnp.int32, (TE, TE), 1)
            before = jnp.where(r_i < c_i, 1.0, 0.0).astype(MM)
            cnt_before = jnp.dot(gs.astype(MM), before, preferred_element_type=F32)
            pos = jnp.zeros((1, TE), F32)
            for gg in range(N_GROUPS):
                pos = pos + gs[gg:gg + 1, :] * (off[gg].astype(F32) + cnt_before[gg:gg + 1, :])
            perm = jnp.where(r_i.astype(F32) == pos, 1.0, 0.0).astype(MM)
            xs_scr[base:base + TE, :] = jnp.dot(perm, u_ref[base:base + TE, :],
                                                preferred_element_type=F32).astype(MM)
            rt_pad = jnp.concatenate([rt, jnp.zeros((128 - MOE_RT_ROWS, TE), F32)], axis=0)
            hi, lo = _split(rt_pad)
            cs_scr[base:base + TE, :] = _dot_nt(perm, hi) + _dot_nt(perm, lo)
            pos_scr[base:base + TE, :] = jnp.broadcast_to(pos, (128, TE)).T
            fs_scr[base:base + TE, :] = jnp.zeros((TE, D), F32)

        lo_g = jnp.int32(0)
        n_g = jnp.int32(0)
        for gg in range(N_GROUPS):
            lo_g = jnp.where(g == gg, off[gg], lo_g)
            n_g = jnp.where(g == gg, n[gg], n_g)
        hi_g = lo_g + n_g
        for rb in range(TE // MOE_BLK):
            @pl.when(jnp.logical_and(lo_g < MOE_BLK * (rb + 1), hi_g > MOE_BLK * rb))
            def _(r0=base + rb * MOE_BLK):
                rows = slice(r0, r0 + MOE_BLK)
                xb = xs_scr[rows, :]
                c = cs_scr[rows, :]
                hg = [jnp.dot(xb, wg_ref[e], preferred_element_type=F32) for e in range(EPG)]
                hu = [jnp.dot(xb, wu_ref[e], preferred_element_type=F32) for e in range(EPG)]
                col = [jnp.sum(jnp.where(lane == EPG * g + e, c, 0.0), axis=-1, keepdims=True) for e in range(EPG)]
                he = jnp.concatenate([(hg[e] * _sigmoid(hg[e]) * hu[e] * col[e]).astype(MM) for e in range(EPG)],
                                     axis=1)
                fs_scr[rows, :] += jnp.dot(he, wd_ref[...], preferred_element_type=F32)

        @pl.when(g == N_GROUPS - 1)
        def _(base=base):
            c_i = lax.broadcasted_iota(jnp.int32, (TE, TE), 1)
            pos_t = jnp.tile(pos_scr[base:base + TE, :], (1, TE // 128))
            unperm = jnp.where(c_i.astype(F32) == pos_t, 1.0, 0.0).astype(MM)
            f_ref[base:base + TE, :] = jnp.dot(unperm, fs_scr[base:base + TE, :].astype(MM),
                                               preferred_element_type=F32).astype(f_ref.dtype)


def _expert_call(u, rt, wg, wu, wd):
    N = u.shape[0]
    TE = MOE_TE if N % MOE_TE == 0 else TM
    nt = N // TE
    NP = MOE_NP if nt % MOE_NP == 0 else 1
    cnt = jnp.sum(rt[N_EXPERTS:N_EXPERTS + N_GROUPS].reshape(N_GROUPS, nt, TE), axis=-1)
    cnt = cnt.astype(jnp.int32).reshape(N_GROUPS * nt)
    gw = EPG * D_EXPERT
    grid_spec = pltpu.PrefetchScalarGridSpec(
        num_scalar_prefetch=1,
        grid=(nt // NP, N_GROUPS),
        in_specs=[pl.BlockSpec((NP * TE, D), lambda t, g, c: (t, 0)),
                  pl.BlockSpec((MOE_RT_ROWS, NP * TE), lambda t, g, c: (0, t)),
                  pl.BlockSpec((None, EPG, D, D_EXPERT), lambda t, g, c: (g, 0, 0, 0)),
                  pl.BlockSpec((None, EPG, D, D_EXPERT), lambda t, g, c: (g, 0, 0, 0)),
                  pl.BlockSpec((None, gw, D), lambda t, g, c: (g, 0, 0))],
        out_specs=pl.BlockSpec((NP * TE, D), lambda t, g, c: (t, 0)),
        scratch_shapes=[pltpu.VMEM((NP * TE, D), MM), pltpu.VMEM((NP * TE, 128), F32),
                        pltpu.VMEM((NP * TE, D), F32), pltpu.VMEM((NP * TE, 128), F32)])
    return pl.pallas_call(
        functools.partial(_expert_kernel, TE=TE, NP=NP, nt=nt),
        grid_spec=grid_spec,
        out_shape=jax.ShapeDtypeStruct((N, D), MM),
        compiler_params=_params(("parallel", "arbitrary")),
        name="moe_experts",
    )(cnt, u, rt, wg, wu, wd)


def _moe_ln_kernel(h_ref, mod_ref, f_ref, lng_ref, lnb_ref, o_ref, *, nB, tile0, L):
    b = pl.program_id(0)
    i = pl.program_id(1)
    gate = _mod_tile(mod_ref, b, i, 5, nB, h_ref.shape[0], L, tile0)
    _res_ln(h_ref, gate, f_ref[...].astype(F32), lng_ref, lnb_ref, o_ref)


def _moe_ln_call(h, mod, f, lng, lnb, nB, tm, tile0, L):
    B, S, _ = h.shape
    R = mod.shape[0]
    nt = S // tm - tile0
    full = lambda shape: pl.BlockSpec(shape, lambda b, i: (0,) * len(shape))
    return pl.pallas_call(
        functools.partial(_moe_ln_kernel, nB=nB, tile0=tile0, L=L),
        grid=(B, nt),
        in_specs=[pl.BlockSpec((None, tm, D), lambda b, i: (b, i + tile0, 0)),
                  full((R, 6 * D)),
                  pl.BlockSpec((None, tm, D), lambda b, i: (b, i, 0)),
                  full((1, D)), full((1, D))],
        out_specs=pl.BlockSpec((None, tm, D), lambda b, i: (b, i, 0)),
        out_shape=jax.ShapeDtypeStruct((B, nt * tm, D), F32),
        compiler_params=_params(("parallel", "arbitrary")),
        name="moe_res_ln",
    )(h, mod, f, lng, lnb)


def _moe_call(h, mod, rwt, rb, wg, wu, wd, lng, lnb, nB, latent_only, L):
    B, S, _ = h.shape
    tm, tile0 = (TM, L // TM) if latent_only else (_row_tile(S), 0)
    u, rt = _router_call(h, mod, rwt, rb, nB, tm, tile0, L)
    f = _expert_call(u.reshape(-1, D), rt, wg.reshape(N_GROUPS, EPG, D, D_EXPERT),
                     wu.reshape(N_GROUPS, EPG, D, D_EXPERT), wd.reshape(N_GROUPS, EPG * D_EXPERT, D))
    return _moe_ln_call(h, mod, f.reshape(B, -1, D), lng, lnb, nB, tm, tile0, L)


def kernel(x, c, ctx, c_ctx, ada_w, ada_b, ln_g, ln_b, even_w_in, even_w_out, shift_w, na_rpb, rw_w0, rw_w_up, rw_a0, rw_a_up, rw_g_up, rw_k_k, rw_k_a, rw_r_k, rw_gn_g, rw_gn_b, odd_w_in, odd_w_out, da_lq1, da_lk1, da_lq2, da_lk2, da_subln_g, router_w, router_b, exp_w_gate, exp_w_up, exp_w_down):
    B, T, _ = x.shape
    L = ctx.shape[1]
    assert L == TM and T % TM == 0 and (T // GRID_W) >= 12
    rows = T // GRID_W
    R = -(-(B + 1) // 8) * 8

    cv = jnp.zeros((R, D), F32).at[:B].set(c).at[B].set(c_ctx)
    mods = _ada_call(cv, ada_w, ada_b)

    cos_t, sin_t = _rope_tables(L, T)
    rwt = router_w.T
    rb = router_b.reshape(N_EXPERTS, 1)
    zpad = jnp.zeros((2, 64, RW_W), F32)

    h = jnp.concatenate([ctx, x], axis=1)
    for l in range(DEPTH):
        mod = mods[l]
        i = l // 2
        lng = ln_g[l].reshape(2, 1, D)
        lnb = ln_b[l].reshape(2, 1, D)
        if l % 2 == 0:
            w_in = even_w_in[i].astype(MM)
            qkv, prw = _even_in_call(h, mod, w_in[:, :3 * NA_W], w_in[:, 3 * NA_W:], B, L)
            ona = _na_call(qkv, _na_bias(na_rpb[i], rows), L)
            yf, yr, bg = _rw_call(
                prw, shift_w[i], rw_w0[i].reshape(2, 1, RW_W),
                jnp.concatenate([rw_w_up[i], zpad], axis=1).astype(MM), rw_a0[i].reshape(2, 1, RW_W),
                jnp.concatenate([zpad, rw_a_up[i]], axis=1).astype(MM), rw_g_up[i].astype(MM),
                rw_k_k[i].reshape(1, RW_W), rw_k_a[i].reshape(1, RW_W), rw_r_k[i].reshape(1, RW_W), L)
            h = _even_out_call(h, mod, ona, yf, yr, bg, rw_gn_g[i].reshape(1, RW_W), rw_gn_b[i].reshape(1, RW_W),
                               even_w_out[i].astype(MM), lng[0], lnb[0], B, L)
        else:
            lam_init = 0.8 - 0.6 * math.exp(-0.3 * l)
            qkv = _odd_in_call(h, mod, odd_w_in[i].astype(MM), cos_t, sin_t, B, L)
            oda = _da_call(qkv, da_lq1[i].reshape(1, 64), da_lk1[i].reshape(1, 64), da_lq2[i].reshape(1, 64),
                           da_lk2[i].reshape(1, 64), da_subln_g[i].reshape(1, 128), L, lam_init)
            h = _odd_out_call(h, mod, oda, odd_w_out[i].astype(MM), lng[0], lnb[0], B, L)
        h = _moe_call(h, mod, rwt, rb, exp_w_gate[l].astype(MM), exp_w_up[l].astype(MM), exp_w_down[l].astype(MM),
                      lng[1], lnb[1], B, l == DEPTH - 1, L)
    return h
```
